```python
import math
import jax
import jax.numpy as jnp
from jax import lax
import numpy as np

D_MODEL = 2048
BATCH = 4
SEQ = 2048
DEPTH = 2

CTX_LEN = 256
GRID_W = 64

N_BRANCH = 4
BRANCH_WIDTH = D_MODEL // 2

ROPE_DIM = 64
ROPE_BASE = 10000.0

SSD_WIDTH = BRANCH_WIDTH
SSD_HEAD_DIM = 64
SSD_HEADS = SSD_WIDTH // SSD_HEAD_DIM
SSD_GROUPS = 2
SSD_STATE = 128
SSD_CHUNK = 128
SSD_CONV = 3
SSD_CONV_CH = SSD_WIDTH + 2 * SSD_GROUPS * SSD_STATE

MLA_HEADS = 8
MLA_NOPE = 128
MLA_ROPE = ROPE_DIM
MLA_V = 128
MLA_Q_LORA = 512
MLA_KV_LORA = 256
MLA_WIDTH = MLA_HEADS * MLA_V
Q_BLOCK = 128

GQA_HEADS = 16
GQA_KV_HEADS = 2
GQA_HEAD_DIM = ROPE_DIM
GQA_WIDTH = GQA_HEADS * GQA_HEAD_DIM
GQA_KV_WIDTH = GQA_KV_HEADS * GQA_HEAD_DIM
WINDOW = 128
WIN_BLOCK = 128

HY_WIDTH = BRANCH_WIDTH
HY_CONV = 3
HY_POS_EMB = 33
HY_BANDS = (HY_POS_EMB - 1) // 2
HY_FILTER_HIDDEN = 64
HY_FAST_DECAY = 0.3
HY_SLOW_DECAY = 1.5
HY_DECAY_TARGET = 0.01

LN_EPS = 1e-6
NEG_INF = -1e30
DEEPNORM_ALPHA = (2 * DEPTH) ** 0.25
DEEPNORM_BETA = (8 * DEPTH) ** -0.25

IN_SPLITS = (
    SSD_WIDTH,
    SSD_CONV_CH,
    2 * SSD_HEADS,
    MLA_Q_LORA,
    MLA_KV_LORA,
    MLA_ROPE,
    MLA_WIDTH,
    GQA_WIDTH,
    GQA_KV_WIDTH,
    GQA_KV_WIDTH,
    GQA_WIDTH,
    3 * HY_WIDTH,
    HY_WIDTH,
    N_BRANCH * D_MODEL,
)
IN_WIDTH = sum(IN_SPLITS)

kernel_name = 'hybrid_ssd_mla_swa_hyena_prefix_dit'


def _split_cols(t):
    out, start = [], 0
    for size in IN_SPLITS:
        out.append(t[..., start:start + size])
        start += size
    return out


def _layernorm(t):
    tf = t.astype(jnp.float32)
    mu = jnp.mean(tf, axis=-1, keepdims=True)
    var = jnp.mean(jnp.square(tf - mu), axis=-1, keepdims=True)
    return ((tf - mu) * lax.rsqrt(var + LN_EPS)).astype(t.dtype)


def _rmsnorm(t, w):
    tf = t.astype(jnp.float32)
    out = tf * lax.rsqrt(jnp.mean(jnp.square(tf), axis=-1, keepdims=True) + LN_EPS)
    return (out * w.astype(jnp.float32)).astype(t.dtype)


def _dwconv_centred(u, w, b):
    k = w.shape[0]
    y = lax.conv_general_dilated(u, w[:, None, :].astype(u.dtype), window_strides=(1,),
                                 padding=[(k // 2, k // 2)], dimension_numbers=('NWC', 'WIO', 'NWC'),
                                 feature_group_count=u.shape[-1])
    return y + b


def _axial_rope_tables(n_tok, rot_dim):
    rows = n_tok // GRID_W
    row = jnp.repeat(jnp.arange(rows, dtype=jnp.float32), GRID_W)
    col = jnp.tile(jnp.arange(GRID_W, dtype=jnp.float32), rows)
    n_freq = rot_dim // 4
    inv_freq = ROPE_BASE ** (-jnp.arange(n_freq, dtype=jnp.float32) / n_freq)
    ang = jnp.concatenate([row[:, None] * inv_freq[None, :], col[:, None] * inv_freq[None, :]], axis=-1)
    return jnp.cos(ang), jnp.sin(ang)


def _apply_rope(t, cos, sin):
    half = t.shape[-1] // 2
    shape = (t.shape[1],) + (1,) * (t.ndim - 3) + (half,)
    cos = cos.reshape(shape).astype(t.dtype)
    sin = sin.reshape(shape).astype(t.dtype)
    t1, t2 = t[..., :half], t[..., half:]
    return jnp.concatenate([t1 * cos - t2 * sin, t1 * sin + t2 * cos], axis=-1)


def _segsum(a):
    t = a.shape[-1]
    rep = jnp.broadcast_to(a[..., :, None], a.shape + (t,))
    rep = jnp.where(jnp.tril(jnp.ones((t, t), dtype=bool), -1), rep, 0.0)
    cs = jnp.cumsum(rep, axis=-2)
    return jnp.where(jnp.tril(jnp.ones((t, t), dtype=bool), 0), cs, -jnp.inf)


def _ssd_chunked(xs, dt, a, bm, cm, init_state, with_y):
    bsz, n, nh, hp = xs.shape
    nc = n // SSD_CHUNK
    rep = nh // bm.shape[2]
    bh = jnp.repeat(bm, rep, axis=2).reshape(bsz, nc, SSD_CHUNK, nh, -1)
    xd = (xs * dt[..., None]).reshape(bsz, nc, SSD_CHUNK, nh, hp)
    la = (dt * a).reshape(bsz, nc, SSD_CHUNK, nh).transpose(0, 3, 1, 2)
    la_cum = jnp.cumsum(la, axis=-1)
    decay_to_end = jnp.exp(la_cum[..., -1:] - la_cum)
    local = jnp.einsum('bclhn,bhcl,bclhp->bchpn', bh, decay_to_end, xd)
    states = jnp.concatenate([init_state[:, None], local], axis=1)
    chunk_decay = jnp.exp(_segsum(jnp.pad(la_cum[..., -1], ((0, 0), (0, 0), (1, 0)))))
    states = jnp.einsum('bhzc,bchpn->bzhpn', chunk_decay, states)
    final_state = states[:, -1]
    if not with_y:
        return None, final_state
    ch = jnp.repeat(cm, rep, axis=2).reshape(bsz, nc, SSD_CHUNK, nh, -1)
    scores = jnp.einsum('bclhn,bcshn->bhcls', ch, bh) * jnp.exp(_segsum(la))
    y_diag = jnp.einsum('bhcls,bcshp->bclhp', scores, xd)
    y_off = jnp.einsum('bclhn,bchpn,bhcl->bclhp', ch, states[:, :-1], jnp.exp(la_cum))
    return (y_diag + y_off).reshape(bsz, n, nh, hp), final_state


def _ssd_prep(xbc, dt_raw, conv_w, conv_b, dt_bias):
    bsz, n, _ = xbc.shape
    u = jax.nn.silu(_dwconv_centred(xbc, conv_w, conv_b)).astype(jnp.float32)
    gn = SSD_GROUPS * SSD_STATE
    xs = u[..., :SSD_WIDTH].reshape(bsz, n, SSD_HEADS, SSD_HEAD_DIM)
    bm = u[..., SSD_WIDTH:SSD_WIDTH + gn].reshape(bsz, n, SSD_GROUPS, SSD_STATE)
    cm = u[..., SSD_WIDTH + gn:].reshape(bsz, n, SSD_GROUPS, SSD_STATE)
    dt = jax.nn.softplus(dt_raw.astype(jnp.float32).reshape(bsz, n, 2, SSD_HEADS) + dt_bias.astype(jnp.float32))
    return xs, bm, cm, dt


def _ssd_bidir(xs, bm, cm, dt, a, init_f, init_b, with_y):
    y_f, s_f = _ssd_chunked(xs, dt[:, :, 0], a[0], bm, cm, init_f, with_y)
    flip = lambda t: jnp.flip(t, axis=1)
    y_b, s_b = _ssd_chunked(flip(xs), flip(dt[:, :, 1]), a[1], flip(bm), flip(cm), init_b, with_y)
    y = y_f + flip(y_b) if with_y else None
    return y, s_f, s_b


def _ssd_branch(xbc, dt_raw, xbc_c, dt_raw_c, conv_w, conv_b, dt_bias, a_log, d_skip, with_ctx_out):
    bsz = xbc.shape[0]
    a = -jnp.exp(a_log.astype(jnp.float32))
    d_skip = d_skip.astype(jnp.float32)[:, None]
    zero = jnp.zeros((bsz, SSD_HEADS, SSD_HEAD_DIM, SSD_STATE), jnp.float32)
    xs_c, bm_c, cm_c, dt_c = _ssd_prep(xbc_c, dt_raw_c, conv_w, conv_b, dt_bias)
    y_c, s_f, s_b = _ssd_bidir(xs_c, bm_c, cm_c, dt_c, a, zero, zero, with_ctx_out)
    xs, bm, cm, dt = _ssd_prep(xbc, dt_raw, conv_w, conv_b, dt_bias)
    y, _, _ = _ssd_bidir(xs, bm, cm, dt, a, s_f, s_b, True)
    y = (y + xs * d_skip).reshape(bsz, xbc.shape[1], SSD_WIDTH)
    if with_ctx_out:
        y_c = (y_c + xs_c * d_skip).reshape(bsz, xbc_c.shape[1], SSD_WIDTH)
    return y, y_c


def _mla_attend(qn, qp, kn, kp, v, scale):
    s = (jnp.einsum('bqhd,bkhd->bhqk', qn, kn) + jnp.einsum('bqhr,bkr->bhqk', qp, kp)).astype(jnp.float32) * scale
    p = jax.nn.softmax(s, axis=-1).astype(v.dtype)
    return jnp.einsum('bhqk,bkhd->bqhd', p, v)


def _sweep_query_blocks(fn, *qs):
    bsz, n = qs[0].shape[:2]
    nb = n // Q_BLOCK
    blocks = tuple(jnp.moveaxis(q.reshape((bsz, nb, Q_BLOCK) + q.shape[2:]), 1, 0) for q in qs)
    out = lax.map(lambda blk: fn(*blk), blocks)
    out = jnp.moveaxis(out, 0, 1)
    return out.reshape((bsz, n) + out.shape[3:])


def _mla_branch(cq, ckv, kpe, cq_c, ckv_c, kpe_c, q_norm, w_uq, kv_norm, w_ukv, cos, sin, with_ctx_out):
    bsz, n, _ = cq.shape
    n_c = ckv_c.shape[1]
    scale = (MLA_NOPE + MLA_ROPE) ** -0.5

    def up_q(t):
        q = (_rmsnorm(t, q_norm) @ w_uq).reshape(t.shape[0], t.shape[1], MLA_HEADS, MLA_NOPE + MLA_ROPE)
        return q[..., :MLA_NOPE], q[..., MLA_NOPE:]

    def up_kv(t):
        kv = (_rmsnorm(t, kv_norm) @ w_ukv).reshape(t.shape[0], t.shape[1], MLA_HEADS, MLA_NOPE + MLA_V)
        return kv[..., :MLA_NOPE], kv[..., MLA_NOPE:]

    kn_c, v_c = up_kv(ckv_c)
    qn, qp = up_q(cq)
    qp = _apply_rope(qp, cos, sin)
    kn, v = up_kv(ckv)
    kp = _apply_rope(kpe, cos, sin)
    kn_all = jnp.concatenate([kn_c, kn], axis=1)
    kp_all = jnp.concatenate([kpe_c, kp], axis=1)
    v_all = jnp.concatenate([v_c, v], axis=1)
    y = _sweep_query_blocks(lambda a, b: _mla_attend(a, b, kn_all, kp_all, v_all, scale), qn, qp)
    y = y.reshape(bsz, n, MLA_WIDTH)
    y_c = None
    if with_ctx_out:
        qn_c, qp_c = up_q(cq_c)
        y_c = _mla_attend(qn_c, qp_c, kn_c, kpe_c, v_c, scale).reshape(bsz, n_c, MLA_WIDTH)
    return y, y_c


def _gqa_branch(q, k, v, q_c, k_c, v_c, sink, cos, sin, with_ctx_out):
    bsz, n, _ = q.shape
    n_c = k_c.shape[1]
    g = GQA_HEADS // GQA_KV_HEADS
    scale = GQA_HEAD_DIM ** -0.5
    q = _apply_rope(q.reshape(bsz, n, GQA_KV_HEADS, g, GQA_HEAD_DIM), cos, sin)
    k = _apply_rope(k.reshape(bsz, n, GQA_KV_HEADS, GQA_HEAD_DIM), cos, sin)
    v = v.reshape(bsz, n, GQA_KV_HEADS, GQA_HEAD_DIM)
    k_c = k_c.reshape(bsz, n_c, GQA_KV_HEADS, GQA_HEAD_DIM)
    v_c = v_c.reshape(bsz, n_c, GQA_KV_HEADS, GQA_HEAD_DIM)
    sink = sink.astype(jnp.float32).reshape(GQA_KV_HEADS, g)

    nb = n // WIN_BLOCK
    nw = 3 * WIN_BLOCK
    qb = q.reshape(bsz, nb, WIN_BLOCK, GQA_KV_HEADS, g, GQA_HEAD_DIM)

    def neighbours(t):
        tp = jnp.pad(t.reshape(bsz, nb, WIN_BLOCK, GQA_KV_HEADS, GQA_HEAD_DIM),
                     ((0, 0), (1, 1), (0, 0), (0, 0), (0, 0)))
        return jnp.concatenate([tp[:, :-2], tp[:, 1:-1], tp[:, 2:]], axis=2)

    kw, vw = neighbours(k), neighbours(v)
    s_loc = jnp.einsum('bnqkgd,bnjkd->bnkgqj', qb, kw).astype(jnp.float32) * scale
    blk = jnp.arange(nb)[:, None]
    qpos = blk * WIN_BLOCK + jnp.arange(WIN_BLOCK)[None, :]
    kpos = (blk - 1) * WIN_BLOCK + jnp.arange(nw)[None, :]
    valid = ((jnp.abs(qpos[:, :, None] - kpos[:, None, :]) <= WINDOW)
             & (kpos[:, None, :] >= 0) & (kpos[:, None, :] < n))
    s_loc = jnp.where(valid[None, :, None, None], s_loc, NEG_INF)
    s_ctx = jnp.einsum('bnqkgd,bjkd->bnkgqj', qb, k_c).astype(jnp.float32) * scale
    s_sink = jnp.broadcast_to(sink[None, None, :, :, None, None], s_loc.shape[:-1] + (1,))
    p = jax.nn.softmax(jnp.concatenate([s_loc, s_ctx, s_sink], axis=-1), axis=-1).astype(v.dtype)
    y = (jnp.einsum('bnkgqj,bnjkd->bnqkgd', p[..., :nw], vw)
         + jnp.einsum('bnkgqj,bjkd->bnqkgd', p[..., nw:nw + n_c], v_c))
    y = y.reshape(bsz, n, GQA_WIDTH)

    y_c = None
    if with_ctx_out:
        qc = q_c.reshape(bsz, n_c, GQA_KV_HEADS, g, GQA_HEAD_DIM)
        s = jnp.einsum('bqkgd,bjkd->bkgqj', qc, k_c).astype(jnp.float32) * scale
        s_sink_c = jnp.broadcast_to(sink[None, :, :, None, None], s.shape[:-1] + (1,))
        pc = jax.nn.softmax(jnp.concatenate([s, s_sink_c], axis=-1), axis=-1).astype(v_c.dtype)
        y_c = jnp.einsum('bkgqj,bjkd->bqkgd', pc[..., :n_c], v_c).reshape(bsz, n_c, GQA_WIDTH)
    return y, y_c


def _hyena_filter(n, w1, b1, w2, b2, w3, freq):
    f32 = jnp.float32
    t = jnp.linspace(0.0, 1.0, n, dtype=f32)[:, None]
    w_ang = (2.0 * math.pi / n) * jnp.arange(n, dtype=f32)[:, None]
    bands = jnp.linspace(1e-4, HY_BANDS - 1, HY_BANDS, dtype=f32)[None, :]
    feats = jnp.concatenate([t, jnp.cos(bands * w_ang), -jnp.sin(bands * w_ang)], axis=-1)
    freq = freq.astype(f32)
    hdn = jnp.sin(freq[0] * (feats @ w1.astype(f32) + b1.astype(f32)))
    hdn = jnp.sin(freq[1] * (hdn @ w2.astype(f32) + b2.astype(f32)))
    h = hdn @ w3.astype(f32)
    deltas = jnp.abs(jnp.linspace(math.log(HY_DECAY_TARGET) / HY_FAST_DECAY,
                                  math.log(HY_DECAY_TARGET) / HY_SLOW_DECAY, HY_WIDTH, dtype=f32))
    h = h * jnp.exp(-t * jnp.tile(deltas, 2)[None, :])
    h_fwd, h_bwd = h[:, :HY_WIDTH], h[:, HY_WIDTH:]
    return jnp.concatenate([h_fwd, jnp.zeros((1, HY_WIDTH), f32), h_bwd[:0:-1]], axis=0)


def _bidir_fftconv(u, filt, d_skip):
    n = u.shape[1]
    uf = jnp.fft.rfft(u.astype(jnp.float32), n=2 * n, axis=1)
    kf = jnp.fft.rfft(filt, axis=0)
    y = jnp.fft.irfft(uf * kf[None], n=2 * n, axis=1)[:, :n]
    return (y + u.astype(jnp.float32) * d_skip.astype(jnp.float32)).astype(u.dtype)


def _hyena_seq(xv, conv_w, conv_b, w1, b1, w2, b2, w3, freq, d_skip):
    n = xv.shape[1]
    u = _dwconv_centred(xv, conv_w, conv_b)
    x0, x1, v = u[..., :HY_WIDTH], u[..., HY_WIDTH:2 * HY_WIDTH], u[..., 2 * HY_WIDTH:]
    filt = _hyena_filter(n, w1, b1, w2, b2, w3, freq)
    return x0 * _bidir_fftconv(x1 * v, filt, d_skip)


def _merge(res, gate, merge_logits, ys, zs, ssd_norm_w, w_branch, w_out, ln_g, ln_b):
    y_ssd, y_mla, y_gqa, y_hy = ys
    z_ssd, z_mla, z_gqa, z_hy = zs
    dt = res.dtype
    lead = y_ssd.shape[:2]
    ssd_g = (y_ssd * jax.nn.silu(z_ssd.astype(jnp.float32))).reshape(lead + (SSD_GROUPS, SSD_WIDTH // SSD_GROUPS))
    ssd_g = _rmsnorm(ssd_g, ssd_norm_w.reshape(SSD_GROUPS, -1)).reshape(lead + (SSD_WIDTH,)).astype(dt)
    gated = jnp.stack([ssd_g,
                       y_mla * jax.nn.silu(z_mla),
                       y_gqa * jax.nn.silu(z_gqa),
                       y_hy * jax.nn.silu(z_hy)], axis=2)
    branch = jnp.einsum('blkw,kwd->blkd', gated, w_branch)
    gates = jax.nn.sigmoid(merge_logits.reshape(merge_logits.shape[:-1] + (N_BRANCH, D_MODEL)))
    mixed = jnp.sum(gates * branch, axis=2) @ w_out
    return _layernorm(DEEPNORM_ALPHA * res + gate * mixed) * ln_g + ln_b


def _layer(x, ctx, c, c_ctx, w_ada, b_ada, w_in, ssd_conv_w, ssd_conv_b, ssd_dt_bias, ssd_a_log, ssd_d,
           ssd_norm_w, mla_q_norm, mla_w_uq, mla_kv_norm, mla_w_ukv, gqa_sink, hy_conv_w, hy_conv_b,
           hy_w1, hy_b1, hy_w2, hy_b2, hy_w3, hy_freq, hy_d, w_branch, w_out, ln_g, ln_b, cos, sin, with_ctx_out):
    mod = jax.nn.silu(c) @ w_ada + b_ada
    shift, scale, gate = jnp.split(mod[:, None, :], 3, axis=-1)
    mod_c = jax.nn.silu(c_ctx) @ w_ada + b_ada
    shift_c, scale_c, gate_c = jnp.split(mod_c, 3, axis=-1)
    h = _layernorm(x) * (1.0 + scale) + shift
    h_c = _layernorm(ctx) * (1.0 + scale_c) + shift_c
    (s_z, s_xbc, s_dt, m_cq, m_ckv, m_kpe, m_z, g_q, g_k, g_v, g_z, hy_xv, hy_z, merge) = _split_cols(h @ w_in)
    (s_z_c, s_xbc_c, s_dt_c, m_cq_c, m_ckv_c, m_kpe_c, m_z_c, g_q_c, g_k_c, g_v_c, g_z_c, hy_xv_c, hy_z_c,
     merge_c) = _split_cols(h_c @ w_in)

    y_ssd, y_ssd_c = _ssd_branch(s_xbc, s_dt, s_xbc_c, s_dt_c, ssd_conv_w, ssd_conv_b, ssd_dt_bias, ssd_a_log,
                                 ssd_d, with_ctx_out)
    y_mla, y_mla_c = _mla_branch(m_cq, m_ckv, m_kpe, m_cq_c, m_ckv_c, m_kpe_c, mla_q_norm, mla_w_uq,
                                 mla_kv_norm, mla_w_ukv, cos, sin, with_ctx_out)
    y_gqa, y_gqa_c = _gqa_branch(g_q, g_k, g_v, g_q_c, g_k_c, g_v_c, gqa_sink, cos, sin, with_ctx_out)
    y_hy = _hyena_seq(hy_xv, hy_conv_w, hy_conv_b, hy_w1, hy_b1, hy_w2, hy_b2, hy_w3, hy_freq, hy_d)
    x_new = _merge(x, gate, merge, (y_ssd, y_mla, y_gqa, y_hy), (s_z, m_z, g_z, hy_z),
                   ssd_norm_w, w_branch, w_out, ln_g, ln_b)
    if not with_ctx_out:
        return x_new, None
    y_hy_c = _hyena_seq(hy_xv_c, hy_conv_w, hy_conv_b, hy_w1, hy_b1, hy_w2, hy_b2, hy_w3, hy_freq, hy_d)
    ctx_new = _merge(ctx, gate_c, merge_c, (y_ssd_c, y_mla_c, y_gqa_c, y_hy_c), (s_z_c, m_z_c, g_z_c, hy_z_c),
                     ssd_norm_w, w_branch, w_out, ln_g, ln_b)
    return x_new, ctx_new


def setup_inputs(seed: int = 0) -> dict:
    key = jax.random.key(seed)
    ks = iter(jax.random.split(key, 64))
    f32 = jnp.float32
    d = D_MODEL
    L = DEPTH

    def nrm(shape, scale):
        return jax.random.normal(next(ks), shape, f32) * scale

    u_dt = jax.random.uniform(next(ks), (L, 2, SSD_HEADS), f32)
    dt0 = jnp.exp(u_dt * (math.log(0.1) - math.log(0.001)) + math.log(0.001))
    ssd_dt_bias = dt0 + jnp.log(-jnp.expm1(-dt0))
    ssd_a_log = jnp.log(jax.random.uniform(next(ks), (L, 2, SSD_HEADS), f32, 1.0, 16.0))
    return {
        'x': nrm((BATCH, SEQ, d), 1.0),
        'c': nrm((BATCH, d), 1.0),
        'ctx': nrm((BATCH, CTX_LEN, d), 1.0),
        'c_ctx': nrm((d,), 1.0),
        'w_ada': nrm((L, d, 3 * d), d ** -0.5),
        'b_ada': nrm((L, 3 * d), 0.02),
        'w_in': nrm((L, d, IN_WIDTH), d ** -0.5),
        'ssd_conv_w': nrm((L, SSD_CONV, SSD_CONV_CH), SSD_CONV ** -0.5),
        'ssd_conv_b': nrm((L, SSD_CONV_CH), 0.02),
        'ssd_dt_bias': ssd_dt_bias,
        'ssd_a_log': ssd_a_log,
        'ssd_d': 1.0 + nrm((L, SSD_HEADS), 0.1),
        'ssd_norm_w': 1.0 + nrm((L, SSD_WIDTH), 0.02),
        'mla_q_norm': 1.0 + nrm((L, MLA_Q_LORA), 0.02),
        'mla_w_uq': nrm((L, MLA_Q_LORA, MLA_HEADS * (MLA_NOPE + MLA_ROPE)), MLA_Q_LORA ** -0.5),
        'mla_kv_norm': 1.0 + nrm((L, MLA_KV_LORA), 0.02),
        'mla_w_ukv': nrm((L, MLA_KV_LORA, MLA_HEADS * (MLA_NOPE + MLA_V)), MLA_KV_LORA ** -0.5),
        'gqa_sink': nrm((L, GQA_HEADS), 0.5),
        'hy_conv_w': nrm((L, HY_CONV, 3 * HY_WIDTH), HY_CONV ** -0.5),
        'hy_conv_b': nrm((L, 3 * HY_WIDTH), 0.02),
        'hy_w1': nrm((L, HY_POS_EMB, HY_FILTER_HIDDEN), HY_POS_EMB ** -0.5),
        'hy_b1': nrm((L, HY_FILTER_HIDDEN), 0.1),
        'hy_w2': nrm((L, HY_FILTER_HIDDEN, HY_FILTER_HIDDEN), HY_FILTER_HIDDEN ** -0.5),
        'hy_b2': nrm((L, HY_FILTER_HIDDEN), 0.1),
        'hy_w3': nrm((L, HY_FILTER_HIDDEN, 2 * HY_WIDTH), 0.1 * HY_FILTER_HIDDEN ** -0.5),
        'hy_freq': 1.0 + nrm((L, 2, HY_FILTER_HIDDEN), 0.1),
        'hy_d': nrm((L, HY_WIDTH), 0.5),
        'w_branch': nrm((L, N_BRANCH, BRANCH_WIDTH, d), DEEPNORM_BETA * BRANCH_WIDTH ** -0.5),
        'w_out': nrm((L, d, d), DEEPNORM_BETA * d ** -0.5),
        'ln_g': 1.0 + nrm((L, d), 0.02),
        'ln_b': nrm((L, d), 0.02),
    }


def reference(x, c, ctx, c_ctx, w_ada, b_ada, w_in, ssd_conv_w, ssd_conv_b, ssd_dt_bias, ssd_a_log, ssd_d,
              ssd_norm_w, mla_q_norm, mla_w_uq, mla_kv_norm, mla_w_ukv, gqa_sink, hy_conv_w, hy_conv_b,
              hy_w1, hy_b1, hy_w2, hy_b2, hy_w3, hy_freq, hy_d, w_branch, w_out, ln_g, ln_b):
    n_lat = x.shape[1]
    cos, sin = _axial_rope_tables(n_lat, ROPE_DIM)
    for i in range(DEPTH):
        x, ctx = _layer(x, ctx, c, c_ctx, w_ada[i], b_ada[i], w_in[i], ssd_conv_w[i], ssd_conv_b[i],
                        ssd_dt_bias[i], ssd_a_log[i], ssd_d[i], ssd_norm_w[i], mla_q_norm[i], mla_w_uq[i],
                        mla_kv_norm[i], mla_w_ukv[i], gqa_sink[i], hy_conv_w[i], hy_conv_b[i], hy_w1[i],
                        hy_b1[i], hy_w2[i], hy_b2[i], hy_w3[i], hy_freq[i], hy_d[i], w_branch[i], w_out[i],
                        ln_g[i], ln_b[i], cos, sin, i < DEPTH - 1)
    return x
```

```python
import functools
import math

import jax
import jax.numpy as jnp
from jax import lax
from jax.experimental import pallas as pl
from jax.experimental.pallas import tpu as pltpu

D_MODEL = 2048
DEPTH = 2
GRID_W = 64
N_BRANCH = 4
BRANCH_WIDTH = D_MODEL // 2
ROPE_DIM = 64
ROPE_BASE = 10000.0
SSD_WIDTH = BRANCH_WIDTH
SSD_HEAD_DIM = 64
SSD_HEADS = SSD_WIDTH // SSD_HEAD_DIM
SSD_GROUPS = 2
SSD_STATE = 128
SSD_CHUNK = 128
SSD_CONV_CH = SSD_WIDTH + 2 * SSD_GROUPS * SSD_STATE
MLA_HEADS = 8
MLA_NOPE = 128
MLA_ROPE = ROPE_DIM
MLA_V = 128
MLA_Q_LORA = 512
MLA_KV_LORA = 256
MLA_WIDTH = MLA_HEADS * MLA_V
Q_BLOCK = 128
GQA_HEADS = 16
GQA_KV_HEADS = 2
GQA_HEAD_DIM = ROPE_DIM
GQA_WIDTH = GQA_HEADS * GQA_HEAD_DIM
GQA_KV_WIDTH = GQA_KV_HEADS * GQA_HEAD_DIM
WINDOW = 128
WIN_BLOCK = 128
HY_WIDTH = BRANCH_WIDTH
HY_POS_EMB = 33
HY_BANDS = (HY_POS_EMB - 1) // 2
HY_FAST_DECAY = 0.3
HY_SLOW_DECAY = 1.5
HY_DECAY_TARGET = 0.01
LN_EPS = 1e-6
NEG_INF = -1e30
DEEPNORM_ALPHA = (2 * DEPTH) ** 0.25

IN_SPLITS = (
    SSD_WIDTH, SSD_CONV_CH, 2 * SSD_HEADS, MLA_Q_LORA, MLA_KV_LORA, MLA_ROPE, MLA_WIDTH, GQA_WIDTH,
    GQA_KV_WIDTH, GQA_KV_WIDTH, GQA_WIDTH, 3 * HY_WIDTH, HY_WIDTH, N_BRANCH * D_MODEL,
)

VMEM_LIMIT_BYTES = 48 * 1024 * 1024


def _mm_kernel(a_ref, b_ref, o_ref):
    o_ref[...] = jnp.dot(a_ref[...], b_ref[...], preferred_element_type=jnp.float32).astype(o_ref.dtype)


def _matmul(a, b, out_dtype=jnp.float32, tm=1024, tn=1024):
    m, k = a.shape
    n = b.shape[1]
    tm = min(tm, m)
    tn = min(tn, n)
    a = a.astype(jnp.bfloat16)
    b = b.astype(jnp.bfloat16)
    return pl.pallas_call(
        _mm_kernel,
        grid=(pl.cdiv(m, tm), pl.cdiv(n, tn)),
        in_specs=[pl.BlockSpec((tm, k), lambda i, j: (i, 0)),
                  pl.BlockSpec((k, tn), lambda i, j: (0, j))],
        out_specs=pl.BlockSpec((tm, tn), lambda i, j: (i, j)),
        out_shape=jax.ShapeDtypeStruct((m, n), out_dtype),
        compiler_params=pltpu.CompilerParams(dimension_semantics=("parallel", "arbitrary"),
                                             vmem_limit_bytes=VMEM_LIMIT_BYTES),
    )(a, b)


def _mm(a, b):
    lead = a.shape[:-1]
    out = _matmul(a.reshape(-1, a.shape[-1]), b)
    return out.reshape(lead + (b.shape[1],))


def _split_cols(t):
    out, start = [], 0
    for size in IN_SPLITS:
        out.append(t[..., start:start + size])
        start += size
    return out


def _layernorm(t):
    mu = jnp.mean(t, axis=-1, keepdims=True)
    var = jnp.mean(jnp.square(t - mu), axis=-1, keepdims=True)
    return (t - mu) * lax.rsqrt(var + LN_EPS)


def _rmsnorm(t, w):
    return t * lax.rsqrt(jnp.mean(jnp.square(t), axis=-1, keepdims=True) + LN_EPS) * w


def _dwconv_centred(u, w, b):
    k = w.shape[0]
    y = lax.conv_general_dilated(u, w[:, None, :], window_strides=(1,), padding=[(k // 2, k // 2)],
                                 dimension_numbers=('NWC', 'WIO', 'NWC'), feature_group_count=u.shape[-1])
    return y + b


def _axial_rope_tables(n_tok, rot_dim):
    rows = n_tok // GRID_W
    row = jnp.repeat(jnp.arange(rows, dtype=jnp.float32), GRID_W)
    col = jnp.tile(jnp.arange(GRID_W, dtype=jnp.float32), rows)
    n_freq = rot_dim // 4
    inv_freq = ROPE_BASE ** (-jnp.arange(n_freq, dtype=jnp.float32) / n_freq)
    ang = jnp.concatenate([row[:, None] * inv_freq[None, :], col[:, None] * inv_freq[None, :]], axis=-1)
    return jnp.cos(ang), jnp.sin(ang)


def _apply_rope(t, cos, sin):
    half = t.shape[-1] // 2
    shape = (t.shape[1],) + (1,) * (t.ndim - 3) + (half,)
    cos = cos.reshape(shape)
    sin = sin.reshape(shape)
    t1, t2 = t[..., :half], t[..., half:]
    return jnp.concatenate([t1 * cos - t2 * sin, t1 * sin + t2 * cos], axis=-1)


def _segsum(a):
    t = a.shape[-1]
    rep = jnp.broadcast_to(a[..., :, None], a.shape + (t,))
    rep = jnp.where(jnp.tril(jnp.ones((t, t), dtype=bool), -1), rep, 0.0)
    cs = jnp.cumsum(rep, axis=-2)
    return jnp.where(jnp.tril(jnp.ones((t, t), dtype=bool), 0), cs, -jnp.inf)


def _ssd_chunked(xs, dt, a, bm, cm, init_state, with_y):
    bsz, n, nh, hp = xs.shape
    nc = n // SSD_CHUNK
    rep = nh // bm.shape[2]
    bh = jnp.repeat(bm, rep, axis=2).reshape(bsz, nc, SSD_CHUNK, nh, -1)
    xd = (xs * dt[..., None]).reshape(bsz, nc, SSD_CHUNK, nh, hp)
    la = (dt * a).reshape(bsz, nc, SSD_CHUNK, nh).transpose(0, 3, 1, 2)
    la_cum = jnp.cumsum(la, axis=-1)
    decay_to_end = jnp.exp(la_cum[..., -1:] - la_cum)
    local = jnp.einsum('bclhn,bhcl,bclhp->bchpn', bh, decay_to_end, xd)
    states = jnp.concatenate([init_state[:, None], local], axis=1)
    chunk_decay = jnp.exp(_segsum(jnp.pad(la_cum[..., -1], ((0, 0), (0, 0), (1, 0)))))
    states = jnp.einsum('bhzc,bchpn->bzhpn', chunk_decay, states)
    final_state = states[:, -1]
    if not with_y:
        return None, final_state
    ch = jnp.repeat(cm, rep, axis=2).reshape(bsz, nc, SSD_CHUNK, nh, -1)
    scores = jnp.einsum('bclhn,bcshn->bhcls', ch, bh) * jnp.exp(_segsum(la))
    y_diag = jnp.einsum('bhcls,bcshp->bclhp', scores, xd)
    y_off = jnp.einsum('bclhn,bchpn,bhcl->bclhp', ch, states[:, :-1], jnp.exp(la_cum))
    return (y_diag + y_off).reshape(bsz, n, nh, hp), final_state


def _ssd_prep(xbc, dt_raw, conv_w, conv_b, dt_bias):
    bsz, n, _ = xbc.shape
    u = jax.nn.silu(_dwconv_centred(xbc, conv_w, conv_b))
    gn = SSD_GROUPS * SSD_STATE
    xs = u[..., :SSD_WIDTH].reshape(bsz, n, SSD_HEADS, SSD_HEAD_DIM)
    bm = u[..., SSD_WIDTH:SSD_WIDTH + gn].reshape(bsz, n, SSD_GROUPS, SSD_STATE)
    cm = u[..., SSD_WIDTH + gn:].reshape(bsz, n, SSD_GROUPS, SSD_STATE)
    dt = jax.nn.softplus(dt_raw.reshape(bsz, n, 2, SSD_HEADS) + dt_bias)
    return xs, bm, cm, dt


def _ssd_bidir(xs, bm, cm, dt, a, init_f, init_b, with_y):
    y_f, s_f = _ssd_chunked(xs, dt[:, :, 0], a[0], bm, cm, init_f, with_y)
    flip = lambda t: jnp.flip(t, axis=1)
    y_b, s_b = _ssd_chunked(flip(xs), flip(dt[:, :, 1]), a[1], flip(bm), flip(cm), init_b, with_y)
    y = y_f + flip(y_b) if with_y else None
    return y, s_f, s_b


def _ssd_branch(xbc, dt_raw, xbc_c, dt_raw_c, conv_w, conv_b, dt_bias, a_log, d_skip, with_ctx_out):
    bsz = xbc.shape[0]
    a = -jnp.exp(a_log)
    d_skip = d_skip[:, None]
    zero = jnp.zeros((bsz, SSD_HEADS, SSD_HEAD_DIM, SSD_STATE), jnp.float32)
    xs_c, bm_c, cm_c, dt_c = _ssd_prep(xbc_c, dt_raw_c, conv_w, conv_b, dt_bias)
    y_c, s_f, s_b = _ssd_bidir(xs_c, bm_c, cm_c, dt_c, a, zero, zero, with_ctx_out)
    xs, bm, cm, dt = _ssd_prep(xbc, dt_raw, conv_w, conv_b, dt_bias)
    y, _, _ = _ssd_bidir(xs, bm, cm, dt, a, s_f, s_b, True)
    y = (y + xs * d_skip).reshape(bsz, xbc.shape[1], SSD_WIDTH)
    if with_ctx_out:
        y_c = (y_c + xs_c * d_skip).reshape(bsz, xbc_c.shape[1], SSD_WIDTH)
    return y, y_c


def _mla_attend(qn, qp, kn, kp, v, scale):
    s = (jnp.einsum('bqhd,bkhd->bhqk', qn, kn) + jnp.einsum('bqhr,bkr->bhqk', qp, kp)) * scale
    p = jax.nn.softmax(s, axis=-1)
    return jnp.einsum('bhqk,bkhd->bqhd', p, v)


def _sweep_query_blocks(fn, *qs):
    bsz, n = qs[0].shape[:2]
    nb = n // Q_BLOCK
    blocks = tuple(jnp.moveaxis(q.reshape((bsz, nb, Q_BLOCK) + q.shape[2:]), 1, 0) for q in qs)
    out = lax.map(lambda blk: fn(*blk), blocks)
    out = jnp.moveaxis(out, 0, 1)
    return out.reshape((bsz, n) + out.shape[3:])


def _mla_branch(cq, ckv, kpe, cq_c, ckv_c, kpe_c, q_norm, w_uq, kv_norm, w_ukv, cos, sin, with_ctx_out):
    bsz, n, _ = cq.shape
    n_c = ckv_c.shape[1]
    scale = (MLA_NOPE + MLA_ROPE) ** -0.5

    def up_q(t):
        q = _mm(_rmsnorm(t, q_norm), w_uq).reshape(t.shape[0], t.shape[1], MLA_HEADS, MLA_NOPE + MLA_ROPE)
        return q[..., :MLA_NOPE], q[..., MLA_NOPE:]

    def up_kv(t):
        kv = _mm(_rmsnorm(t, kv_norm), w_ukv).reshape(t.shape[0], t.shape[1], MLA_HEADS, MLA_NOPE + MLA_V)
        return kv[..., :MLA_NOPE], kv[..., MLA_NOPE:]

    kn_c, v_c = up_kv(ckv_c)
    qn, qp = up_q(cq)
    qp = _apply_rope(qp, cos, sin)
    kn, v = up_kv(ckv)
    kp = _apply_rope(kpe, cos, sin)
    kn_all = jnp.concatenate([kn_c, kn], axis=1)
    kp_all = jnp.concatenate([kpe_c, kp], axis=1)
    v_all = jnp.concatenate([v_c, v], axis=1)
    y = _sweep_query_blocks(lambda a, b: _mla_attend(a, b, kn_all, kp_all, v_all, scale), qn, qp)
    y = y.reshape(bsz, n, MLA_WIDTH)
    y_c = None
    if with_ctx_out:
        qn_c, qp_c = up_q(cq_c)
        y_c = _mla_attend(qn_c, qp_c, kn_c, kpe_c, v_c, scale).reshape(bsz, n_c, MLA_WIDTH)
    return y, y_c


def _gqa_branch(q, k, v, q_c, k_c, v_c, sink, cos, sin, with_ctx_out):
    bsz, n, _ = q.shape
    n_c = k_c.shape[1]
    g = GQA_HEADS // GQA_KV_HEADS
    scale = GQA_HEAD_DIM ** -0.5
    q = _apply_rope(q.reshape(bsz, n, GQA_KV_HEADS, g, GQA_HEAD_DIM), cos, sin)
    k = _apply_rope(k.reshape(bsz, n, GQA_KV_HEADS, GQA_HEAD_DIM), cos, sin)
    v = v.reshape(bsz, n, GQA_KV_HEADS, GQA_HEAD_DIM)
    k_c = k_c.reshape(bsz, n_c, GQA_KV_HEADS, GQA_HEAD_DIM)
    v_c = v_c.reshape(bsz, n_c, GQA_KV_HEADS, GQA_HEAD_DIM)
    sink = sink.reshape(GQA_KV_HEADS, g)

    nb = n // WIN_BLOCK
    nw = 3 * WIN_BLOCK
    qb = q.reshape(bsz, nb, WIN_BLOCK, GQA_KV_HEADS, g, GQA_HEAD_DIM)

    def neighbours(t):
        tp = jnp.pad(t.reshape(bsz, nb, WIN_BLOCK, GQA_KV_HEADS, GQA_HEAD_DIM),
                     ((0, 0), (1, 1), (0, 0), (0, 0), (0, 0)))
        return jnp.concatenate([tp[:, :-2], tp[:, 1:-1], tp[:, 2:]], axis=2)

    kw, vw = neighbours(k), neighbours(v)
    s_loc = jnp.einsum('bnqkgd,bnjkd->bnkgqj', qb, kw) * scale
    blk = jnp.arange(nb)[:, None]
    qpos = blk * WIN_BLOCK + jnp.arange(WIN_BLOCK)[None, :]
    kpos = (blk - 1) * WIN_BLOCK + jnp.arange(nw)[None, :]
    valid = ((jnp.abs(qpos[:, :, None] - kpos[:, None, :]) <= WINDOW)
             & (kpos[:, None, :] >= 0) & (kpos[:, None, :] < n))
    s_loc = jnp.where(valid[None, :, None, None], s_loc, NEG_INF)
    s_ctx = jnp.einsum('bnqkgd,bjkd->bnkgqj', qb, k_c) * scale
    s_sink = jnp.broadcast_to(sink[None, None, :, :, None, None], s_loc.shape[:-1] + (1,))
    p = jax.nn.softmax(jnp.concatenate([s_loc, s_ctx, s_sink], axis=-1), axis=-1)
    y = (jnp.einsum('bnkgqj,bnjkd->bnqkgd', p[..., :nw], vw)
         + jnp.einsum('bnkgqj,bjkd->bnqkgd', p[..., nw:nw + n_c], v_c))
    y = y.reshape(bsz, n, GQA_WIDTH)

    y_c = None
    if with_ctx_out:
        qc = q_c.reshape(bsz, n_c, GQA_KV_HEADS, g, GQA_HEAD_DIM)
        s = jnp.einsum('bqkgd,bjkd->bkgqj', qc, k_c) * scale
        s_sink_c = jnp.broadcast_to(sink[None, :, :, None, None], s.shape[:-1] + (1,))
        pc = jax.nn.softmax(jnp.concatenate([s, s_sink_c], axis=-1), axis=-1)
        y_c = jnp.einsum('bkgqj,bjkd->bqkgd', pc[..., :n_c], v_c).reshape(bsz, n_c, GQA_WIDTH)
    return y, y_c


def _hyena_filter(n, w1, b1, w2, b2, w3, freq):
    f32 = jnp.float32
    t = jnp.linspace(0.0, 1.0, n, dtype=f32)[:, None]
    w_ang = (2.0 * math.pi / n) * jnp.arange(n, dtype=f32)[:, None]
    bands = jnp.linspace(1e-4, HY_BANDS - 1, HY_BANDS, dtype=f32)[None, :]
    feats = jnp.concatenate([t, jnp.cos(bands * w_ang), -jnp.sin(bands * w_ang)], axis=-1)
    hp = lax.Precision.HIGHEST
    hdn = jnp.sin(freq[0] * (jnp.dot(feats, w1, precision=hp) + b1))
    hdn = jnp.sin(freq[1] * (jnp.dot(hdn, w2, precision=hp) + b2))
    h = jnp.dot(hdn, w3, precision=hp)
    deltas = jnp.abs(jnp.linspace(math.log(HY_DECAY_TARGET) / HY_FAST_DECAY,
                                  math.log(HY_DECAY_TARGET) / HY_SLOW_DECAY, HY_WIDTH, dtype=f32))
    h = h * jnp.exp(-t * jnp.tile(deltas, 2)[None, :])
    h_fwd, h_bwd = h[:, :HY_WIDTH], h[:, HY_WIDTH:]
    return jnp.concatenate([h_fwd, jnp.zeros((1, HY_WIDTH), f32), h_bwd[:0:-1]], axis=0)


def _bidir_fftconv(u, filt, d_skip):
    n = u.shape[1]
    uf = jnp.fft.rfft(u, n=2 * n, axis=1)
    kf = jnp.fft.rfft(filt, axis=0)
    y = jnp.fft.irfft(uf * kf[None], n=2 * n, axis=1)[:, :n]
    return y + u * d_skip


def _hyena_seq(xv, conv_w, conv_b, w1, b1, w2, b2, w3, freq, d_skip):
    n = xv.shape[1]
    u = _dwconv_centred(xv, conv_w, conv_b)
    x0, x1, v = u[..., :HY_WIDTH], u[..., HY_WIDTH:2 * HY_WIDTH], u[..., 2 * HY_WIDTH:]
    filt = _hyena_filter(n, w1, b1, w2, b2, w3, freq)
    return x0 * _bidir_fftconv(x1 * v, filt, d_skip)


def _merge(res, gate, merge_logits, ys, zs, ssd_norm_w, w_branch, w_out, ln_g, ln_b):
    y_ssd, y_mla, y_gqa, y_hy = ys
    z_ssd, z_mla, z_gqa, z_hy = zs
    lead = y_ssd.shape[:2]
    ssd_g = (y_ssd * jax.nn.silu(z_ssd)).reshape(lead + (SSD_GROUPS, SSD_WIDTH // SSD_GROUPS))
    ssd_g = _rmsnorm(ssd_g, ssd_norm_w.reshape(SSD_GROUPS, -1)).reshape(lead + (SSD_WIDTH,))
    gated = [ssd_g, y_mla * jax.nn.silu(z_mla), y_gqa * jax.nn.silu(z_gqa), y_hy * jax.nn.silu(z_hy)]
    gates = jax.nn.sigmoid(merge_logits.reshape(merge_logits.shape[:-1] + (N_BRANCH, D_MODEL)))
    mixed = sum(gates[..., k, :] * _mm(gated[k], w_branch[k]) for k in range(N_BRANCH))
    mixed = _mm(mixed, w_out)
    return _layernorm(DEEPNORM_ALPHA * res + gate * mixed) * ln_g + ln_b


def _layer(x, ctx, c, c_ctx, w_ada, b_ada, w_in, ssd_conv_w, ssd_conv_b, ssd_dt_bias, ssd_a_log, ssd_d,
           ssd_norm_w, mla_q_norm, mla_w_uq, mla_kv_norm, mla_w_ukv, gqa_sink, hy_conv_w, hy_conv_b,
           hy_w1, hy_b1, hy_w2, hy_b2, hy_w3, hy_freq, hy_d, w_branch, w_out, ln_g, ln_b, cos, sin, with_ctx_out):
    mod = _mm(jax.nn.silu(jnp.concatenate([c, c_ctx[None]], axis=0)), w_ada) + b_ada
    shift, scale, gate = jnp.split(mod[:-1, None, :], 3, axis=-1)
    shift_c, scale_c, gate_c = jnp.split(mod[-1], 3, axis=-1)
    h = _layernorm(x) * (1.0 + scale) + shift
    h_c = _layernorm(ctx) * (1.0 + scale_c) + shift_c
    w_in_bf = w_in.astype(jnp.bfloat16)
    (s_z, s_xbc, s_dt, m_cq, m_ckv, m_kpe, m_z, g_q, g_k, g_v, g_z, hy_xv, hy_z, merge) = _split_cols(
        _mm(h, w_in_bf))
    (s_z_c, s_xbc_c, s_dt_c, m_cq_c, m_ckv_c, m_kpe_c, m_z_c, g_q_c, g_k_c, g_v_c, g_z_c, hy_xv_c, hy_z_c,
     merge_c) = _split_cols(_mm(h_c, w_in_bf))

    y_ssd, y_ssd_c = _ssd_branch(s_xbc, s_dt, s_xbc_c, s_dt_c, ssd_conv_w, ssd_conv_b, ssd_dt_bias, ssd_a_log,
                                 ssd_d, with_ctx_out)
    y_mla, y_mla_c = _mla_branch(m_cq, m_ckv, m_kpe, m_cq_c, m_ckv_c, m_kpe_c, mla_q_norm, mla_w_uq,
                                 mla_kv_norm, mla_w_ukv, cos, sin, with_ctx_out)
    y_gqa, y_gqa_c = _gqa_branch(g_q, g_k, g_v, g_q_c, g_k_c, g_v_c, gqa_sink, cos, sin, with_ctx_out)
    y_hy = _hyena_seq(hy_xv, hy_conv_w, hy_conv_b, hy_w1, hy_b1, hy_w2, hy_b2, hy_w3, hy_freq, hy_d)
    x_new = _merge(x, gate, merge, (y_ssd, y_mla, y_gqa, y_hy), (s_z, m_z, g_z, hy_z),
                   ssd_norm_w, w_branch, w_out, ln_g, ln_b)
    if not with_ctx_out:
        return x_new, None
    y_hy_c = _hyena_seq(hy_xv_c, hy_conv_w, hy_conv_b, hy_w1, hy_b1, hy_w2, hy_b2, hy_w3, hy_freq, hy_d)
    ctx_new = _merge(ctx, gate_c, merge_c, (y_ssd_c, y_mla_c, y_gqa_c, y_hy_c), (s_z_c, m_z_c, g_z_c, hy_z_c),
                     ssd_norm_w, w_branch, w_out, ln_g, ln_b)
    return x_new, ctx_new


def kernel(x, c, ctx, c_ctx, w_ada, b_ada, w_in, ssd_conv_w, ssd_conv_b, ssd_dt_bias, ssd_a_log, ssd_d,
           ssd_norm_w, mla_q_norm, mla_w_uq, mla_kv_norm, mla_w_ukv, gqa_sink, hy_conv_w, hy_conv_b,
           hy_w1, hy_b1, hy_w2, hy_b2, hy_w3, hy_freq, hy_d, w_branch, w_out, ln_g, ln_b):
    cos, sin = _axial_rope_tables(x.shape[1], ROPE_DIM)
    for i in range(DEPTH):
        x, ctx = _layer(x, ctx, c, c_ctx, w_ada[i], b_ada[i], w_in[i], ssd_conv_w[i], ssd_conv_b[i],
                        ssd_dt_bias[i], ssd_a_log[i], ssd_d[i], ssd_norm_w[i], mla_q_norm[i], mla_w_uq[i],
                        mla_kv_norm[i], mla_w_ukv[i], gqa_sink[i], hy_conv_w[i], hy_conv_b[i], hy_w1[i],
                        hy_b1[i], hy_w2[i], hy_b2[i], hy_w3[i], hy_freq[i], hy_d[i], w_branch[i], w_out[i],
                        ln_g[i], ln_b[i], cos, sin, i < DEPTH - 1)
    return x
```

```python
import functools
import math

import jax
import jax.numpy as jnp
from jax import lax
from jax.experimental import pallas as pl
from jax.experimental.pallas import tpu as pltpu

D_MODEL = 2048
DEPTH = 2
GRID_W = 64
N_BRANCH = 4
BRANCH_WIDTH = D_MODEL // 2
ROPE_DIM = 64
ROPE_BASE = 10000.0
SSD_WIDTH = BRANCH_WIDTH
SSD_HEAD_DIM = 64
SSD_HEADS = SSD_WIDTH // SSD_HEAD_DIM
SSD_GROUPS = 2
SSD_STATE = 128
SSD_CHUNK = 128
SSD_CONV_CH = SSD_WIDTH + 2 * SSD_GROUPS * SSD_STATE
MLA_HEADS = 8
MLA_NOPE = 128
MLA_ROPE = ROPE_DIM
MLA_V = 128
MLA_Q_LORA = 512
MLA_KV_LORA = 256
MLA_WIDTH = MLA_HEADS * MLA_V
GQA_HEADS = 16
GQA_KV_HEADS = 2
GQA_HEAD_DIM = ROPE_DIM
GQA_WIDTH = GQA_HEADS * GQA_HEAD_DIM
GQA_KV_WIDTH = GQA_KV_HEADS * GQA_HEAD_DIM
WINDOW = 128
HY_WIDTH = BRANCH_WIDTH
HY_POS_EMB = 33
HY_BANDS = (HY_POS_EMB - 1) // 2
HY_FILTER_HIDDEN = 64
HY_FAST_DECAY = 0.3
HY_SLOW_DECAY = 1.5
HY_DECAY_TARGET = 0.01
LN_EPS = 1e-6
NEG_INF = -1e30
DEEPNORM_ALPHA = (2 * DEPTH) ** 0.25

IN_SPLITS = (
    SSD_WIDTH, SSD_CONV_CH, 2 * SSD_HEADS, MLA_Q_LORA, MLA_KV_LORA, MLA_ROPE, MLA_WIDTH, GQA_WIDTH,
    GQA_KV_WIDTH, GQA_KV_WIDTH, GQA_WIDTH, 3 * HY_WIDTH, HY_WIDTH, N_BRANCH * D_MODEL,
)

LANE = 128
ROW_TILE = 256
CHUNK = SSD_CHUNK
VMEM_LIMIT_BYTES = 56 * 1024 * 1024

P_MERGE = 0
P_HY_X0 = 8192
P_HY_X1 = 9216
P_HY_V = 10240
P_HY_Z = 11264
P_SSD_Z = 12288
P_SSD_X = 13312
P_MLA_Z = 14336
P_GQA_Q = 15360
P_GQA_Z = 16384
P_MLA_CQ = 17408
P_SSD_B = 17920
P_SSD_C = 18176
P_MLA_CKV = 18432
P_GQA_K = 18688
P_GQA_V = 18816
P_SMALL = 18944
P_WIDTH = 19072
SMALL_DT = 64

F32 = jnp.float32
BF16 = jnp.bfloat16


def _cparams(*sem):
    return pltpu.CompilerParams(dimension_semantics=sem, vmem_limit_bytes=VMEM_LIMIT_BYTES)


def _split3(v):
    h1 = v.astype(BF16)
    r1 = v - h1.astype(F32)
    h2 = r1.astype(BF16)
    h3 = (r1 - h2.astype(F32)).astype(BF16)
    return h1, h2, h3


def _dot(a, b):
    return jnp.dot(a, b, preferred_element_type=F32)


def _dot_nt(a, b):
    return lax.dot_general(a, b, (((1,), (1,)), ((), ())), preferred_element_type=F32)


def _dot_tn(a, b):
    return lax.dot_general(a, b, (((0,), (0,)), ((), ())), preferred_element_type=F32)


def _dot_exact_rhs(a, b_bf16, pieces):
    parts = _split3(a)[:pieces]
    out = _dot(parts[0], b_bf16)
    for p in parts[1:]:
        out = out + _dot(p, b_bf16)
    return out


def _dot3(a, b):
    a1, a2, _ = _split3(a)
    b1, b2, _ = _split3(b)
    return _dot(a1, b1) + (_dot(a1, b2) + _dot(a2, b1))


def _silu(v):
    return v * jax.nn.sigmoid(v)


def _softplus(v):
    return jnp.maximum(v, 0.0) + jnp.log(1.0 + jnp.exp(-jnp.abs(v)))


def _rotswap(t):
    lane = lax.broadcasted_iota(jnp.int32, t.shape, 1)
    return jnp.where((lane & 63) < 32, pltpu.roll(t, 96, 1), pltpu.roll(t, 32, 1))


def _rope(t, cos, sin_signed):
    return t * cos + _rotswap(t) * sin_signed


def _mm_kernel(a_ref, b_ref, o_ref):
    o_ref[...] = _dot(a_ref[...], b_ref[...]).astype(o_ref.dtype)


def _matmul(a, b, out_dtype, tm, tn):
    m, k = a.shape
    n = b.shape[1]
    tm = min(tm, m)
    tn = min(tn, n)
    return pl.pallas_call(
        _mm_kernel,
        grid=(pl.cdiv(m, tm), pl.cdiv(n, tn)),
        in_specs=[pl.BlockSpec((tm, k), lambda i, j: (i, 0)),
                  pl.BlockSpec((k, tn), lambda i, j: (0, j))],
        out_specs=pl.BlockSpec((tm, tn), lambda i, j: (i, j)),
        out_shape=jax.ShapeDtypeStruct((m, n), out_dtype),
        compiler_params=_cparams("parallel", "arbitrary"),
    )(a.astype(BF16), b.astype(BF16))


def _mod_row(i, tiles_per_sample, n_samples):
    return jnp.where(i % tiles_per_sample == 0, n_samples, i // tiles_per_sample)


def _ln_mod_kernel(x_ref, mod_ref, h_ref):
    x = x_ref[...]
    mu = jnp.mean(x, axis=-1, keepdims=True)
    xc = x - mu
    var = jnp.mean(xc * xc, axis=-1, keepdims=True)
    m = mod_ref[0]
    h_ref[...] = (xc * lax.rsqrt(var + LN_EPS) * (1.0 + m[1:2]) + m[0:1]).astype(h_ref.dtype)


def _ln_mod(xx, mod, tps, nb):
    tt, d = xx.shape
    return pl.pallas_call(
        _ln_mod_kernel,
        grid=(tt // ROW_TILE,),
        in_specs=[pl.BlockSpec((ROW_TILE, d), lambda i: (i, 0)),
                  pl.BlockSpec((1, 3, d), lambda i: (_mod_row(i, tps, nb), 0, 0))],
        out_specs=pl.BlockSpec((ROW_TILE, d), lambda i: (i, 0)),
        out_shape=jax.ShapeDtypeStruct((tt, d), BF16),
        compiler_params=_cparams("parallel"),
    )(xx, mod)


def _repack_w_in(w_in):
    parts, start = [], 0
    for size in IN_SPLITS:
        parts.append(w_in[:, start:start + size])
        start += size
    s_z, s_xbc, s_dt, m_cq, m_ckv, m_kpe, m_z, g_q, g_k, g_v, g_z, hy_xv, hy_z, merge = parts
    gn = SSD_GROUPS * SSD_STATE
    small = jnp.concatenate([m_kpe, s_dt, jnp.zeros((w_in.shape[0], LANE - MLA_ROPE - 2 * SSD_HEADS), w_in.dtype)],
                            axis=1)
    order = [merge, hy_xv, hy_z, s_z, s_xbc[:, :SSD_WIDTH], m_z, g_q, g_z, m_cq,
             s_xbc[:, SSD_WIDTH:SSD_WIDTH + gn], s_xbc[:, SSD_WIDTH + gn:], m_ckv, g_k, g_v, small]
    return jnp.concatenate(order, axis=1).astype(BF16)


def _conv_chunk(src_ref, w, bias, c, n_ctx_chunks, n_chunks):
    tb = n_chunks * CHUNK
    r0 = pl.multiple_of(c * CHUNK, CHUNK)
    cur = src_ref[pl.ds(r0, CHUNK), :].astype(F32)
    rp = pl.multiple_of(jnp.maximum(r0 - 16, 0), 16)
    rn = pl.multiple_of(jnp.minimum(r0 + CHUNK, tb - 16), 16)
    prev_row = src_ref[pl.ds(rp, 16), :].astype(F32)[15:16]
    next_row = src_ref[pl.ds(rn, 16), :].astype(F32)[0:1]
    has_prev = jnp.logical_and(c != 0, c != n_ctx_chunks)
    has_next = jnp.logical_and(c != n_ctx_chunks - 1, c != n_chunks - 1)
    prev_row = jnp.where(has_prev, prev_row, 0.0)
    next_row = jnp.where(has_next, next_row, 0.0)
    row = lax.broadcasted_iota(jnp.int32, cur.shape, 0)
    sp = jnp.where(row == 0, prev_row, pltpu.roll(cur, 1, 0))
    sn = jnp.where(row == CHUNK - 1, next_row, pltpu.roll(cur, CHUNK - 1, 0))
    return sp * w[0:1] + cur * w[1:2] + sn * w[2:3] + bias


def _ssd_kernel(x_ref, b_ref, c_ref, sm_ref, z_ref, cwx_ref, cbx_ref, cwb_ref, cbb_ref, cwc_ref, cbc_ref,
                dtb_ref, alog_ref, dsk_ref, nw_ref, o_ref, ux, ub, uc, yacc, st, *, n_ctx_chunks, n_chunks):
    g = pl.program_id(1)
    hp = ux.shape[1]
    nh = hp // SSD_HEAD_DIM

    def conv_body(c, carry):
        r0 = pl.multiple_of(c * CHUNK, CHUNK)
        ux[pl.ds(r0, CHUNK), :] = _silu(_conv_chunk(x_ref, cwx_ref[...], cbx_ref[...], c, n_ctx_chunks,
                                                    n_chunks)).astype(ux.dtype)
        ub[pl.ds(r0, CHUNK), :] = _silu(_conv_chunk(b_ref, cwb_ref[...], cbb_ref[...], c, n_ctx_chunks,
                                                    n_chunks)).astype(ub.dtype)
        uc[pl.ds(r0, CHUNK), :] = _silu(_conv_chunk(c_ref, cwc_ref[...], cbc_ref[...], c, n_ctx_chunks,
                                                    n_chunks)).astype(uc.dtype)
        return carry

    lax.fori_loop(0, n_chunks, conv_body, 0)

    ri = lax.broadcasted_iota(jnp.int32, (CHUNK, CHUNK), 0)
    ci = lax.broadcasted_iota(jnp.int32, (CHUNK, CHUNK), 1)
    lane_hp = lax.broadcasted_iota(jnp.int32, (CHUNK, hp), 1)
    lane128 = lax.broadcasted_iota(jnp.int32, (CHUNK, LANE), 1)
    a_all = -jnp.exp(alog_ref[0])
    dtb = dtb_ref[0]

    def run_direction(dirn):
        tri = (ci <= ri) if dirn == 0 else (ci >= ri)
        tri_bf = jnp.where(tri, 1.0, 0.0).astype(BF16)
        expand = jnp.where(lax.broadcasted_iota(jnp.int32, (LANE, hp), 0)
                           == dirn * 16 + (lax.broadcasted_iota(jnp.int32, (LANE, hp), 1) >> 6),
                           1.0, 0.0).astype(BF16)
        last = CHUNK - 1 if dirn == 0 else 0
        st[...] = jnp.zeros_like(st)

        def body(i, carry):
            if dirn == 0:
                c = i
            else:
                c = jnp.where(i < n_ctx_chunks, n_ctx_chunks - 1 - i, n_chunks - 1 + n_ctx_chunks - i)
            r0 = pl.multiple_of(c * CHUNK, CHUNK)
            xs = ux[pl.ds(r0, CHUNK), :].astype(F32)
            bc = ub[pl.ds(r0, CHUNK), :]
            cc = uc[pl.ds(r0, CHUNK), :]
            sm = sm_ref[pl.ds(r0, CHUNK), :].astype(F32)
            sm = jnp.where(g == 0, pltpu.roll(sm, LANE - SMALL_DT, 1), pltpu.roll(sm, LANE - SMALL_DT - nh, 1))
            dt = _softplus(sm + dtb)
            la = dt * a_all
            cum = _dot_exact_rhs_lhs(tri_bf, la)
            cx = _dot_exact_rhs(cum, expand, 2)
            dtx = _dot_exact_rhs(dt, expand, 2)
            cx_last = cx[last:last + 1]
            ecum = jnp.exp(cx)
            dec_end = jnp.exp(cx_last - cx)
            xd = xs * dtx
            xd_bf = xd.astype(BF16)
            gmat = _dot_nt(cc, bc)
            cum_t = cum.T
            cols = []
            for j in range(nh // 2):
                sc = []
                for h in (2 * j, 2 * j + 1):
                    k = dirn * 16 + h
                    diff = cum[:, k:k + 1] - cum_t[k:k + 1, :]
                    lmat = jnp.exp(jnp.where(tri, diff, NEG_INF))
                    sc.append((gmat * lmat).astype(BF16))
                xcol = xd_bf[:, j * LANE:(j + 1) * LANE]
                lo = lane128 < SSD_HEAD_DIM
                rhs = jnp.concatenate([jnp.where(lo, xcol, jnp.zeros_like(xcol)),
                                       jnp.where(lo, jnp.zeros_like(xcol), xcol)], axis=0)
                cols.append(_dot(jnp.concatenate(sc, axis=1), rhs))
            y = jnp.concatenate(cols, axis=1)
            s_prev = st[...]
            y = y + _dot(cc, s_prev.astype(BF16)) * ecum
            st[...] = jnp.exp(cx_last) * s_prev + _dot_tn(bc, (xd * dec_end).astype(BF16))
            if dirn == 0:
                yacc[pl.ds(r0, CHUNK), :] = y + xs * dsk_ref[...]
            else:
                tot = (yacc[pl.ds(r0, CHUNK), :] + y) * _silu(z_ref[pl.ds(r0, CHUNK), :].astype(F32))
                ms = jnp.mean(tot * tot, axis=-1, keepdims=True)
                o_ref[pl.ds(r0, CHUNK), :] = (tot * lax.rsqrt(ms + LN_EPS) * nw_ref[...]).astype(o_ref.dtype)
            return carry

        lax.fori_loop(0, n_chunks, body, 0)

    run_direction(0)
    run_direction(1)


def _dot_exact_rhs_lhs(tri_bf, v):
    v1, v2, v3 = _split3(v)
    return _dot(tri_bf, v1) + (_dot(tri_bf, v2) + _dot(tri_bf, v3))


def _ssd(p3, conv_w, conv_b, dt_bias, a_log, d_skip, norm_w, n_ctx):
    nb, tb, _ = p3.shape
    hp = SSD_WIDTH // SSD_GROUPS
    n_chunks = tb // CHUNK
    gn = SSD_GROUPS * SSD_STATE
    cwx, cwb, cwc = conv_w[:, :SSD_WIDTH], conv_w[:, SSD_WIDTH:SSD_WIDTH + gn], conv_w[:, SSD_WIDTH + gn:]
    cb = conv_b[None, :]
    cbx, cbb, cbc = cb[:, :SSD_WIDTH], cb[:, SSD_WIDTH:SSD_WIDTH + gn], cb[:, SSD_WIDTH + gn:]
    hg = SSD_HEADS // SSD_GROUPS

    def per_group(v):
        v = v.reshape(2, SSD_GROUPS, hg).transpose(1, 0, 2)
        v = jnp.pad(v, ((0, 0), (0, 0), (0, 16 - hg))).reshape(SSD_GROUPS, 1, 32)
        return jnp.pad(v, ((0, 0), (0, 0), (0, LANE - 32)))

    dsk = jnp.repeat(d_skip, SSD_HEAD_DIM)[None, :]
    kern = functools.partial(_ssd_kernel, n_ctx_chunks=n_ctx // CHUNK, n_chunks=n_chunks)
    col = lambda off, w: (lambda b, g: (b, 0, off // w + g))
    par = lambda b, g: (0, g)
    return pl.pallas_call(
        kern,
        grid=(nb, SSD_GROUPS),
        in_specs=[pl.BlockSpec((None, tb, hp), col(P_SSD_X, hp)),
                  pl.BlockSpec((None, tb, SSD_STATE), col(P_SSD_B, SSD_STATE)),
                  pl.BlockSpec((None, tb, SSD_STATE), col(P_SSD_C, SSD_STATE)),
                  pl.BlockSpec((None, tb, LANE), lambda b, g: (b, 0, P_SMALL // LANE)),
                  pl.BlockSpec((None, tb, hp), col(P_SSD_Z, hp)),
                  pl.BlockSpec((3, hp), par), pl.BlockSpec((1, hp), par),
                  pl.BlockSpec((3, SSD_STATE), par), pl.BlockSpec((1, SSD_STATE), par),
                  pl.BlockSpec((3, SSD_STATE), par), pl.BlockSpec((1, SSD_STATE), par),
                  pl.BlockSpec((1, 1, LANE), lambda b, g: (g, 0, 0)),
                  pl.BlockSpec((1, 1, LANE), lambda b, g: (g, 0, 0)),
                  pl.BlockSpec((1, hp), par), pl.BlockSpec((1, hp), par)],
        out_specs=pl.BlockSpec((None, tb, hp), lambda b, g: (b, 0, g)),
        out_shape=jax.ShapeDtypeStruct((nb, tb, SSD_WIDTH), BF16),
        scratch_shapes=[pltpu.VMEM((tb, hp), BF16), pltpu.VMEM((tb, SSD_STATE), BF16),
                        pltpu.VMEM((tb, SSD_STATE), BF16), pltpu.VMEM((tb, hp), F32),
                        pltpu.VMEM((SSD_STATE, hp), F32)],
        compiler_params=_cparams("parallel", "parallel"),
    )(p3, p3, p3, p3, p3, cwx, cbx, cwb, cbb, cwc, cbc, per_group(dt_bias), per_group(a_log), dsk,
      norm_w[None, :])


def _rope_tables(n_ctx, n_lat):
    rows = n_lat // GRID_W
    row = jnp.repeat(jnp.arange(rows, dtype=F32), GRID_W)
    colp = jnp.tile(jnp.arange(GRID_W, dtype=F32), rows)
    n_freq = ROPE_DIM // 4
    inv_freq = ROPE_BASE ** (-jnp.arange(n_freq, dtype=F32) / n_freq)
    ang = jnp.concatenate([row[:, None] * inv_freq[None, :], colp[:, None] * inv_freq[None, :]], axis=-1)
    cos, sin = jnp.cos(ang), jnp.sin(ang)
    cos2 = jnp.tile(cos, (1, 4))
    sin2 = jnp.tile(jnp.concatenate([-sin, sin], axis=-1), (1, 2))
    cos2 = jnp.concatenate([jnp.ones((n_ctx, LANE), F32), cos2], axis=0)
    sin2 = jnp.concatenate([jnp.zeros((n_ctx, LANE), F32), sin2], axis=0)
    return cos2, sin2


def _attn_prep_kernel(cq_ref, ckv_ref, sm_ref, gk_ref, gv_ref, cos_ref, sin_ref, qn_ref, wq_ref, kn_ref,
                      wk_ref, wv_ref, q_ref, k_ref, v_ref, kg_ref, vg_ref):
    cos, sin = cos_ref[...], sin_ref[...]
    lane = lax.broadcasted_iota(jnp.int32, cos.shape, 1)
    scale = (MLA_NOPE + MLA_ROPE) ** -0.5
    cq = cq_ref[...].astype(F32)
    cqn = cq * lax.rsqrt(jnp.mean(cq * cq, axis=-1, keepdims=True) + LN_EPS) * qn_ref[...]
    q = _dot(cqn.astype(BF16), wq_ref[...])
    ckv = ckv_ref[...].astype(F32)
    ckvn = (ckv * lax.rsqrt(jnp.mean(ckv * ckv, axis=-1, keepdims=True) + LN_EPS) * kn_ref[...]).astype(BF16)
    kn = _dot(ckvn, wk_ref[...])
    v_ref[...] = _dot(ckvn, wv_ref[...]).astype(v_ref.dtype)
    kpe = jnp.where(lane < MLA_ROPE, _rope(sm_ref[...].astype(F32), cos, sin), 0.0).astype(k_ref.dtype)
    for h in range(MLA_HEADS):
        o = 2 * h * LANE
        q_ref[:, o:o + LANE] = (q[:, o:o + LANE] * scale).astype(q_ref.dtype)
        q_ref[:, o + LANE:o + 2 * LANE] = (_rope(q[:, o + LANE:o + 2 * LANE], cos, sin) * scale).astype(q_ref.dtype)
        k_ref[:, o:o + LANE] = kn[:, h * LANE:(h + 1) * LANE].astype(k_ref.dtype)
        k_ref[:, o + LANE:o + 2 * LANE] = kpe
    kr = _rope(gk_ref[...].astype(F32), cos, sin)
    lo = lane < GQA_HEAD_DIM
    for src, dst in ((kr, kg_ref), (gv_ref[...].astype(F32), vg_ref)):
        sw = pltpu.roll(src, GQA_HEAD_DIM, 1)
        dst[:, 0:LANE] = jnp.where(lo, src, sw).astype(dst.dtype)
        dst[:, LANE:2 * LANE] = jnp.where(lo, sw, src).astype(dst.dtype)


def _attn_prep(p, cos2, sin2, q_norm, w_uq, kv_norm, w_ukv, tps):
    tt = p.shape[0]
    hq = MLA_NOPE + MLA_ROPE
    wq = w_uq.reshape(MLA_Q_LORA, MLA_HEADS, hq)
    wq = jnp.pad(wq, ((0, 0), (0, 0), (0, 2 * LANE - hq))).reshape(MLA_Q_LORA, MLA_HEADS * 2 * LANE).astype(BF16)
    wkv = w_ukv.reshape(MLA_KV_LORA, MLA_HEADS, MLA_NOPE + MLA_V)
    wk = wkv[:, :, :MLA_NOPE].reshape(MLA_KV_LORA, MLA_HEADS * MLA_NOPE).astype(BF16)
    wv = wkv[:, :, MLA_NOPE:].reshape(MLA_KV_LORA, MLA_HEADS * MLA_V).astype(BF16)
    full = lambda shape: pl.BlockSpec(shape, lambda i: (0, 0))
    pcol = lambda off, w: pl.BlockSpec((ROW_TILE, w), lambda i: (i, off // w))
    rows = lambda w: pl.BlockSpec((ROW_TILE, w), lambda i: (i, 0))
    tab = pl.BlockSpec((ROW_TILE, LANE), lambda i: (i % tps, 0))
    return pl.pallas_call(
        _attn_prep_kernel,
        grid=(tt // ROW_TILE,),
        in_specs=[pcol(P_MLA_CQ, MLA_Q_LORA), pcol(P_MLA_CKV, MLA_KV_LORA), pcol(P_SMALL, LANE),
                  pcol(P_GQA_K, LANE), pcol(P_GQA_V, LANE), tab, tab,
                  full((1, MLA_Q_LORA)), full(wq.shape), full((1, MLA_KV_LORA)), full(wk.shape), full(wv.shape)],
        out_specs=[rows(MLA_HEADS * 2 * LANE), rows(MLA_HEADS * 2 * LANE), rows(MLA_WIDTH),
                   rows(2 * LANE), rows(2 * LANE)],
        out_shape=[jax.ShapeDtypeStruct((tt, MLA_HEADS * 2 * LANE), BF16),
                   jax.ShapeDtypeStruct((tt, MLA_HEADS * 2 * LANE), BF16),
                   jax.ShapeDtypeStruct((tt, MLA_WIDTH), BF16),
                   jax.ShapeDtypeStruct((tt, 2 * LANE), BF16),
                   jax.ShapeDtypeStruct((tt, 2 * LANE), BF16)],
        compiler_params=_cparams("parallel"),
    )(p, p, p, p, p, cos2, sin2, q_norm[None, :], wq, kv_norm[None, :], wk, wv)


def _mla_kernel(q_ref, k_ref, v_ref, z_ref, o_ref, *, n_ctx):
    def attend(n_keys):
        s = _dot_nt(q_ref[...], k_ref[0:n_keys, :])
        m = jnp.max(s, axis=-1, keepdims=True)
        p = jnp.exp(s - m)
        l = jnp.sum(p, axis=-1, keepdims=True)
        o = _dot(p.astype(BF16), v_ref[0:n_keys, :]) / l
        o_ref[...] = (o * _silu(z_ref[...].astype(F32))).astype(o_ref.dtype)

    @pl.when(pl.program_id(2) == 0)
    def _():
        attend(n_ctx)

    @pl.when(pl.program_id(2) != 0)
    def _():
        attend(k_ref.shape[0])


def _mla(qm, km, vm, p, nb, tb, n_ctx):
    tq = n_ctx
    tps = tb // tq
    row = lambda b, h, i: b * tps + i
    return pl.pallas_call(
        functools.partial(_mla_kernel, n_ctx=n_ctx),
        grid=(nb, MLA_HEADS, tps),
        in_specs=[pl.BlockSpec((tq, 2 * LANE), lambda b, h, i: (row(b, h, i), h)),
                  pl.BlockSpec((tb, 2 * LANE), lambda b, h, i: (b, h)),
                  pl.BlockSpec((tb, LANE), lambda b, h, i: (b, h)),
                  pl.BlockSpec((tq, LANE), lambda b, h, i: (row(b, h, i), P_MLA_Z // LANE + h))],
        out_specs=pl.BlockSpec((tq, LANE), lambda b, h, i: (row(b, h, i), h)),
        out_shape=jax.ShapeDtypeStruct((nb * tb, MLA_WIDTH), BF16),
        compiler_params=_cparams("parallel", "parallel", "arbitrary"),
    )(qm, km, vm, p)


def _gqa_kernel(sink_ref, q_ref, kp_ref, ko_ref, kn_ref, kc_ref, vp_ref, vo_ref, vn_ref, vc_ref, cos_ref,
                sin_ref, z_ref, o_ref, *, n_ctx_blocks):
    blk = pl.program_id(1)
    n_blk = pl.num_programs(1)
    is_lat = blk >= n_ctx_blocks
    has_prev = blk > n_ctx_blocks
    has_next = jnp.logical_and(is_lat, blk < n_blk - 1)
    cos, sin = cos_ref[...], sin_ref[...]
    scale = GQA_HEAD_DIM ** -0.5
    w = q_ref.shape[0]
    lane = lax.broadcasted_iota(jnp.int32, (w, LANE), 1)
    lo = lane < GQA_HEAD_DIM
    hpk = GQA_HEADS // GQA_KV_HEADS
    ri = lax.broadcasted_iota(jnp.int32, (hpk * w, w), 0) & (w - 1)
    ci = lax.broadcasted_iota(jnp.int32, (hpk * w, w), 1)
    outs = []
    for kh in range(GQA_KV_HEADS):
        rows = []
        for cidx in range(hpk // 2):
            col = kh * (hpk // 2) + cidx
            qc = (_rope(q_ref[:, col * LANE:(col + 1) * LANE].astype(F32), cos, sin) * scale).astype(BF16)
            rows.append(jnp.where(lo, qc, jnp.zeros_like(qc)))
            rows.append(jnp.where(lo, jnp.zeros_like(qc), qc))
        lhs = jnp.concatenate(rows, axis=0)
        ksl = slice(kh * LANE, (kh + 1) * LANE)
        s_p = jnp.where(jnp.logical_and(ci >= ri, has_prev), _dot_nt(lhs, kp_ref[:, ksl]), NEG_INF)
        s_o = jnp.where(is_lat, _dot_nt(lhs, ko_ref[:, ksl]), NEG_INF)
        s_n = jnp.where(jnp.logical_and(ci <= ri, has_next), _dot_nt(lhs, kn_ref[:, ksl]), NEG_INF)
        s = jnp.concatenate([s_p, s_o, s_n, _dot_nt(lhs, kc_ref[:, ksl])], axis=1)
        vall = jnp.concatenate([vp_ref[:, ksl], vo_ref[:, ksl], vn_ref[:, ksl], vc_ref[:, ksl]], axis=0)
        sink = jnp.concatenate([jnp.full((w, 1), sink_ref[kh * hpk + h], F32) for h in range(hpk)], axis=0)
        m = jnp.maximum(jnp.max(s, axis=-1, keepdims=True), sink)
        pr = jnp.exp(s - m)
        l = jnp.sum(pr, axis=-1, keepdims=True) + jnp.exp(sink - m)
        o = _dot(pr.astype(BF16), vall) / l
        for cidx in range(hpk // 2):
            outs.append(jnp.where(lo, o[(2 * cidx) * w:(2 * cidx + 1) * w],
                                  o[(2 * cidx + 1) * w:(2 * cidx + 2) * w]))
    y = jnp.concatenate(outs, axis=1)
    o_ref[...] = (y * _silu(z_ref[...].astype(F32))).astype(o_ref.dtype)


def _gqa(p, kg, vg, cos2, sin2, sink, nb, tb, n_ctx):
    w = WINDOW
    bps = tb // w
    row = lambda b, i: b * bps + i
    kv = lambda f: pl.BlockSpec((w, 2 * LANE), lambda b, i: (f(b, i), 0))
    prev = lambda b, i: b * bps + jnp.maximum(i - 1, 0)
    nxt = lambda b, i: b * bps + jnp.minimum(i + 1, bps - 1)
    ctxs = pl.BlockSpec((n_ctx, 2 * LANE), lambda b, i: (b * (tb // n_ctx), 0))
    tab = pl.BlockSpec((w, LANE), lambda b, i: (i, 0))
    return pl.pallas_call(
        functools.partial(_gqa_kernel, n_ctx_blocks=n_ctx // w),
        grid=(nb, bps),
        in_specs=[pl.BlockSpec(memory_space=pltpu.SMEM),
                  pl.BlockSpec((w, GQA_WIDTH), lambda b, i: (row(b, i), P_GQA_Q // GQA_WIDTH)),
                  kv(prev), kv(row), kv(nxt), ctxs, kv(prev), kv(row), kv(nxt), ctxs, tab, tab,
                  pl.BlockSpec((w, GQA_WIDTH), lambda b, i: (row(b, i), P_GQA_Z // GQA_WIDTH))],
        out_specs=pl.BlockSpec((w, GQA_WIDTH), lambda b, i: (row(b, i), 0)),
        out_shape=jax.ShapeDtypeStruct((nb * tb, GQA_WIDTH), BF16),
        compiler_params=_cparams("parallel", "arbitrary"),
    )(sink, p, kg, kg, kg, kg, vg, vg, vg, vg, cos2, sin2, p)


def _dft_tables(n):
    k = jnp.arange(n, dtype=jnp.int32)
    km = (k[:, None] * k[None, :]) % (2 * n)
    ang = km.astype(F32) * (math.pi / n)
    return jnp.cos(ang).astype(BF16), jnp.sin(ang).astype(BF16)


def _hyena_feats(n):
    t = jnp.linspace(0.0, 1.0, n, dtype=F32)[:, None]
    w_ang = (2.0 * math.pi / n) * jnp.arange(n, dtype=F32)[:, None]
    bands = jnp.linspace(1e-4, HY_BANDS - 1, HY_BANDS, dtype=F32)[None, :]
    feats = jnp.concatenate([t, jnp.cos(bands * w_ang), -jnp.sin(bands * w_ang)], axis=-1)
    return jnp.pad(feats, ((0, 0), (0, LANE - HY_POS_EMB)))


def _hyena_filter_kernel(feats_ref, w1_ref, b1_ref, w2_ref, b2_ref, w3f_ref, w3b_ref, freq_ref, dl_ref,
                         cos_ref, sin_ref, kr_ref, ks_ref, kq_ref, hs, hd, *, n):
    kt = pl.program_id(1)
    tk = cos_ref.shape[0]

    @pl.when(kt == 0)
    def _():
        feats = feats_ref[...]
        t = feats[:, 0:1]
        hdn = jnp.sin(freq_ref[0:1, :] * (_dot3(feats, w1_ref[...]) + b1_ref[...]))
        hdn = jnp.sin(freq_ref[1:2, :] * (_dot3(hdn, w2_ref[...]) + b2_ref[...]))
        decay = jnp.exp(-t * dl_ref[...])
        hf = _dot3(hdn, w3f_ref[...]) * decay
        hb = _dot3(hdn, w3b_ref[...]) * decay
        row = lax.broadcasted_iota(jnp.int32, hb.shape, 0)
        hb = jnp.where(row == 0, 0.0, hb)
        hsum = hf + hb
        hs[...] = hsum.astype(hs.dtype)
        hd[...] = (hf - hb).astype(hd.dtype)
        alt = jnp.where((row & 1) == 0, 1.0, -1.0)
        kq_ref[...] = jnp.sum(hsum * alt, axis=0, keepdims=True) * (1.0 / (2 * n))

    kidx = kt * tk + lax.broadcasted_iota(jnp.int32, (tk, 1), 0)
    wk = jnp.where(kidx == 0, 0.5 / n, 1.0 / n)
    kr_ref[...] = _dot(cos_ref[...], hs[...]) * wk
    ks_ref[...] = _dot(sin_ref[...], hd[...]) * wk


def _hyena_filter(n, cos_t, sin_t, w1, b1, w2, b2, w3, freq):
    cbw = 512
    tk = min(256, n)
    deltas = jnp.abs(jnp.linspace(math.log(HY_DECAY_TARGET) / HY_FAST_DECAY,
                                  math.log(HY_DECAY_TARGET) / HY_SLOW_DECAY, HY_WIDTH, dtype=F32))[None, :]
    w1p = jnp.pad(w1, ((0, LANE - HY_POS_EMB), (0, 0)))
    full = lambda a: pl.BlockSpec(a.shape, lambda c, k: (0,) * a.ndim)
    nblk = HY_WIDTH // cbw
    kern = functools.partial(_hyena_filter_kernel, n=n)
    feats = _hyena_feats(n)
    return pl.pallas_call(
        kern,
        grid=(nblk, n // tk),
        in_specs=[full(feats), full(w1p), pl.BlockSpec((1, HY_FILTER_HIDDEN), lambda c, k: (0, 0)),
                  full(w2), pl.BlockSpec((1, HY_FILTER_HIDDEN), lambda c, k: (0, 0)),
                  pl.BlockSpec((HY_FILTER_HIDDEN, cbw), lambda c, k: (0, c)),
                  pl.BlockSpec((HY_FILTER_HIDDEN, cbw), lambda c, k: (0, nblk + c)),
                  full(freq), pl.BlockSpec((1, cbw), lambda c, k: (0, c)),
                  pl.BlockSpec((tk, n), lambda c, k: (k, 0)), pl.BlockSpec((tk, n), lambda c, k: (k, 0))],
        out_specs=[pl.BlockSpec((tk, cbw), lambda c, k: (k, c)), pl.BlockSpec((tk, cbw), lambda c, k: (k, c)),
                   pl.BlockSpec((1, cbw), lambda c, k: (0, c))],
        out_shape=[jax.ShapeDtypeStruct((n, HY_WIDTH), F32), jax.ShapeDtypeStruct((n, HY_WIDTH), F32),
                   jax.ShapeDtypeStruct((1, HY_WIDTH), F32)],
        scratch_shapes=[pltpu.VMEM((n, cbw), BF16), pltpu.VMEM((n, cbw), BF16)],
        compiler_params=_cparams("parallel", "arbitrary"),
    )(feats, w1p, b1[None, :], w2, b2[None, :], w3, w3, freq, deltas, cos_t, sin_t)


def _spectral_product(a, bq, kr, ks):
    return (a * kr - bq * ks).astype(BF16), (a * ks + bq * kr).astype(BF16)


def _hyena_kernel(x0_ref, x1_ref, v_ref, z_ref, cw0_ref, cb0_ref, cw1_ref, cb1_ref, cwv_ref, cbv_ref, d_ref,
                  cr_ref, sr_ref, cc_ref, sc_ref, kr_ref, ks_ref, kq_ref,
                  ccx_ref, scx_ref, krx_ref, ksx_ref, kqx_ref, o_ref, u, g0, yacc,
                  *, n_ctx, n_chunks, with_ctx):
    kt = pl.program_id(2)
    n_kt = pl.num_programs(2)
    n_ctx_chunks = n_ctx // CHUNK
    tb = n_chunks * CHUNK

    @pl.when(kt == 0)
    def _():
        def conv_body(c, carry):
            r0 = pl.multiple_of(c * CHUNK, CHUNK)
            x1c = _conv_chunk(x1_ref, cw1_ref[...], cb1_ref[...], c, n_ctx_chunks, n_chunks)
            vc = _conv_chunk(v_ref, cwv_ref[...], cbv_ref[...], c, n_ctx_chunks, n_chunks)
            u[pl.ds(r0, CHUNK), :] = (x1c * vc).astype(u.dtype)
            x0c = _conv_chunk(x0_ref, cw0_ref[...], cb0_ref[...], c, n_ctx_chunks, n_chunks)
            g0[pl.ds(r0, CHUNK), :] = (x0c * _silu(z_ref[pl.ds(r0, CHUNK), :].astype(F32))).astype(g0.dtype)
            return carry

        lax.fori_loop(0, n_chunks, conv_body, 0)
        yacc[...] = jnp.zeros_like(yacc)

    ul = u[n_ctx:tb, :]
    yr, ys = _spectral_product(_dot(cr_ref[...], ul), _dot(sr_ref[...], ul), kr_ref[...], ks_ref[...])
    yacc[...] += _dot(cc_ref[...], yr) + _dot(sc_ref[...], ys)

    @pl.when(kt == n_kt - 1)
    def _():
        def finish(uv, y, kq, r_lo, r_hi):
            uf = uv.astype(F32)
            alt = jnp.where((lax.broadcasted_iota(jnp.int32, uf.shape, 0) & 1) == 0, 1.0, -1.0)
            unyq = jnp.sum(uf * alt, axis=0, keepdims=True)
            y = y + alt * (unyq * kq) + uf * d_ref[...]
            o_ref[r_lo:r_hi, :] = (y * g0[r_lo:r_hi, :].astype(F32)).astype(o_ref.dtype)

        finish(ul, yacc[...], kq_ref[...], n_ctx, tb)
        if with_ctx:
            uc = u[0:n_ctx, :]
            yr, ys = _spectral_product(_dot(ccx_ref[...], uc), _dot(scx_ref[...], uc), krx_ref[...], ksx_ref[...])
            finish(uc, _dot(ccx_ref[...], yr) + _dot(scx_ref[...], ys), kqx_ref[...], 0, n_ctx)
        else:
            o_ref[0:n_ctx, :] = jnp.zeros((n_ctx, o_ref.shape[1]), o_ref.dtype)


def _hyena(p3, conv_w, conv_b, d_skip, tabs, filt, tabs_c, filt_c, n_ctx, with_ctx):
    nb, tb, _ = p3.shape
    n_lat = tb - n_ctx
    cbw = 512
    tk = 256
    cos_t, sin_t = tabs
    kr, ks, kq = filt
    cos_c, sin_c = tabs_c
    krc, ksc, kqc = filt_c
    cb = conv_b[None, :]
    one = pl.Buffered(1)
    pcol = lambda off: pl.BlockSpec((None, tb, cbw), lambda b, c, k: (b, 0, off // cbw + c), pipeline_mode=one)
    par = lambda rows, sec: pl.BlockSpec((rows, cbw), lambda b, c, k: (0, sec * (HY_WIDTH // cbw) + c))
    kern = functools.partial(_hyena_kernel, n_ctx=n_ctx, n_chunks=tb // CHUNK, with_ctx=with_ctx)
    small = lambda a: pl.BlockSpec(a.shape, lambda b, c, k: (0, 0))
    chan = lambda rows: pl.BlockSpec((rows, cbw), lambda b, c, k: (0, c))
    return pl.pallas_call(
        kern,
        grid=(nb, HY_WIDTH // cbw, n_lat // tk),
        in_specs=[pcol(P_HY_X0), pcol(P_HY_X1), pcol(P_HY_V), pcol(P_HY_Z),
                  par(3, 0), par(1, 0), par(3, 1), par(1, 1), par(3, 2), par(1, 2), chan(1),
                  pl.BlockSpec((tk, n_lat), lambda b, c, k: (k, 0)), pl.BlockSpec((tk, n_lat), lambda b, c, k: (k, 0)),
                  pl.BlockSpec((n_lat, tk), lambda b, c, k: (0, k)), pl.BlockSpec((n_lat, tk), lambda b, c, k: (0, k)),
                  pl.BlockSpec((tk, cbw), lambda b, c, k: (k, c)), pl.BlockSpec((tk, cbw), lambda b, c, k: (k, c)),
                  chan(1), small(cos_c), small(sin_c), chan(n_ctx), chan(n_ctx), chan(1)],
        out_specs=pl.BlockSpec((None, tb, cbw), lambda b, c, k: (b, 0, c)),
        out_shape=jax.ShapeDtypeStruct((nb, tb, HY_WIDTH), BF16),
        scratch_shapes=[pltpu.VMEM((tb, cbw), BF16), pltpu.VMEM((tb, cbw), BF16), pltpu.VMEM((n_lat, cbw), F32)],
        compiler_params=_cparams("parallel", "parallel", "arbitrary"),
    )(p3, p3, p3, p3, conv_w, cb, conv_w, cb, conv_w, cb, d_skip[None, :], cos_t, sin_t, cos_t, sin_t,
      kr, ks, kq, cos_c, sin_c, krc, ksc, kqc)


def _mix_kernel(g0_ref, g1_ref, g2_ref, g3_ref, lg_ref, w_ref, o_ref, acc):
    k = pl.program_id(1)
    for kk, g_ref in enumerate((g0_ref, g1_ref, g2_ref, g3_ref)):
        @pl.when(k == kk)
        def _(g_ref=g_ref, kk=kk):
            contrib = jax.nn.sigmoid(lg_ref[...].astype(F32)) * _dot(g_ref[...], w_ref[0])
            if kk == 0:
                acc[...] = contrib
            else:
                acc[...] += contrib

    @pl.when(k == N_BRANCH - 1)
    def _():
        o_ref[...] = acc[...].astype(o_ref.dtype)


def _mix(gated, p, w_branch_bf):
    tt = p.shape[0]
    tm = 512
    assert tt % tm == 0
    g_spec = pl.BlockSpec((tm, BRANCH_WIDTH), lambda i, k: (i, 0))
    return pl.pallas_call(
        _mix_kernel,
        grid=(tt // tm, N_BRANCH),
        in_specs=[g_spec, g_spec, g_spec, g_spec,
                  pl.BlockSpec((tm, D_MODEL), lambda i, k: (i, P_MERGE // D_MODEL + k)),
                  pl.BlockSpec((1, BRANCH_WIDTH, D_MODEL), lambda i, k: (k, 0, 0))],
        out_specs=pl.BlockSpec((tm, D_MODEL), lambda i, k: (i, 0)),
        out_shape=jax.ShapeDtypeStruct((tt, D_MODEL), BF16),
        scratch_shapes=[pltpu.VMEM((tm, D_MODEL), F32)],
        compiler_params=_cparams("parallel", "arbitrary"),
    )(*gated, p, w_branch_bf)


def _out_kernel(a_ref, w_ref, x_ref, mod_ref, g_ref, b_ref, *rest, with_next):
    mixed = _dot(a_ref[...], w_ref[...])
    r = DEEPNORM_ALPHA * x_ref[...] + mod_ref[0][2:3] * mixed
    mu = jnp.mean(r, axis=-1, keepdims=True)
    rc = r - mu
    var = jnp.mean(rc * rc, axis=-1, keepdims=True)
    xn = rc * lax.rsqrt(var + LN_EPS) * g_ref[...] + b_ref[...]
    if with_next:
        modn_ref, o_ref, h_ref = rest
        o_ref[...] = xn
        mu = jnp.mean(xn, axis=-1, keepdims=True)
        xc = xn - mu
        var = jnp.mean(xc * xc, axis=-1, keepdims=True)
        mn = modn_ref[0]
        h_ref[...] = (xc * lax.rsqrt(var + LN_EPS) * (1.0 + mn[1:2]) + mn[0:1]).astype(h_ref.dtype)
    else:
        (o_ref,) = rest
        o_ref[...] = xn


def _out_next(mixed, w_out_bf, xx, mod, ln_g, ln_b, mod_next, tps, nb):
    tt, d = xx.shape
    tile = pl.BlockSpec((ROW_TILE, d), lambda i: (i, 0))
    modspec = pl.BlockSpec((1, 3, d), lambda i: (_mod_row(i, tps, nb), 0, 0))
    vec = pl.BlockSpec((1, d), lambda i: (0, 0))
    return pl.pallas_call(
        functools.partial(_out_kernel, with_next=True),
        grid=(tt // ROW_TILE,),
        in_specs=[tile, pl.BlockSpec((d, d), lambda i: (0, 0)), tile, modspec, vec, vec, modspec],
        out_specs=[tile, tile],
        out_shape=[jax.ShapeDtypeStruct((tt, d), F32), jax.ShapeDtypeStruct((tt, d), BF16)],
        compiler_params=_cparams("parallel"),
    )(mixed, w_out_bf, xx, mod, ln_g[None, :], ln_b[None, :], mod_next)


def _out_last(mixed, w_out_bf, xx, mod, ln_g, ln_b, tps, nb, n_ctx):
    tt, d = xx.shape
    off = n_ctx // ROW_TILE
    tile = pl.BlockSpec((ROW_TILE, d), lambda b, j: (b * tps + off + j, 0))
    vec = pl.BlockSpec((1, d), lambda b, j: (0, 0))
    return pl.pallas_call(
        functools.partial(_out_kernel, with_next=False),
        grid=(nb, tps - off),
        in_specs=[tile, pl.BlockSpec((d, d), lambda b, j: (0, 0)), tile,
                  pl.BlockSpec((1, 3, d), lambda b, j: (b, 0, 0)), vec, vec],
        out_specs=pl.BlockSpec((None, ROW_TILE, d), lambda b, j: (b, j, 0)),
        out_shape=jax.ShapeDtypeStruct((nb, (tps - off) * ROW_TILE, d), F32),
        compiler_params=_cparams("parallel", "parallel"),
    )(mixed, w_out_bf, xx, mod, ln_g[None, :], ln_b[None, :])


def kernel(x, c, ctx, c_ctx, w_ada, b_ada, w_in, ssd_conv_w, ssd_conv_b, ssd_dt_bias, ssd_a_log, ssd_d,
           ssd_norm_w, mla_q_norm, mla_w_uq, mla_kv_norm, mla_w_ukv, gqa_sink, hy_conv_w, hy_conv_b,
           hy_w1, hy_b1, hy_w2, hy_b2, hy_w3, hy_freq, hy_d, w_branch, w_out, ln_g, ln_b):
    nb, n_lat, d = x.shape
    n_ctx = ctx.shape[1]
    tb = n_ctx + n_lat
    tt = nb * tb
    tps = tb // ROW_TILE
    assert d == D_MODEL and n_ctx == ROW_TILE and n_lat % 512 == 0 and n_lat % GRID_W == 0

    xx = jnp.concatenate([ctx, x], axis=1).reshape(tt, d)
    cos2, sin2 = _rope_tables(n_ctx, n_lat)
    tabs, tabs_c = _dft_tables(n_lat), _dft_tables(n_ctx)
    cvec = jax.nn.silu(jnp.concatenate([c, c_ctx[None]], axis=0))
    mods = [(_matmul(cvec, w_ada[i], F32, 8, 1024) + b_ada[i]).reshape(nb + 1, 3, d) for i in range(DEPTH)]

    h = _ln_mod(xx, mods[0], tps, nb)
    out = None
    for i in range(DEPTH):
        last = i == DEPTH - 1
        p = _matmul(h, _repack_w_in(w_in[i]), BF16, tb, 512)
        p3 = p.reshape(nb, tb, P_WIDTH)
        g_ssd = _ssd(p3, ssd_conv_w[i], ssd_conv_b[i], ssd_dt_bias[i], ssd_a_log[i], ssd_d[i], ssd_norm_w[i],
                     n_ctx).reshape(tt, SSD_WIDTH)
        qm, km, vm, kg, vg = _attn_prep(p, cos2, sin2, mla_q_norm[i], mla_w_uq[i], mla_kv_norm[i], mla_w_ukv[i],
                                        tps)
        g_mla = _mla(qm, km, vm, p, nb, tb, n_ctx)
        g_gqa = _gqa(p, kg, vg, cos2, sin2, gqa_sink[i], nb, tb, n_ctx)
        fargs = (hy_w1[i], hy_b1[i], hy_w2[i], hy_b2[i], hy_w3[i], hy_freq[i])
        filt = _hyena_filter(n_lat, *tabs, *fargs)
        filt_c = _hyena_filter(n_ctx, *tabs_c, *fargs)
        g_hy = _hyena(p3, hy_conv_w[i], hy_conv_b[i], hy_d[i], tabs, filt, tabs_c, filt_c, n_ctx,
                      not last).reshape(tt, HY_WIDTH)
        mixed = _mix((g_ssd, g_mla, g_gqa, g_hy), p, w_branch[i].astype(BF16))
        w_out_bf = w_out[i].astype(BF16)
        if last:
            out = _out_last(mixed, w_out_bf, xx, mods[i], ln_g[i], ln_b[i], tps, nb, n_ctx)
        else:
            xx, h = _out_next(mixed, w_out_bf, xx, mods[i], ln_g[i], ln_b[i], mods[i + 1], tps, nb)
    return out
```

```python
import functools
import math

import jax
import jax.numpy as jnp
from jax import lax
from jax.experimental import pallas as pl
from jax.experimental.pallas import tpu as pltpu

D_MODEL = 2048
DEPTH = 2
GRID_W = 64
N_BRANCH = 4
BRANCH_WIDTH = D_MODEL // 2
ROPE_DIM = 64
ROPE_BASE = 10000.0
SSD_WIDTH = BRANCH_WIDTH
SSD_HEAD_DIM = 64
SSD_HEADS = SSD_WIDTH // SSD_HEAD_DIM
SSD_GROUPS = 2
SSD_STATE = 128
SSD_CHUNK = 128
SSD_CONV_CH = SSD_WIDTH + 2 * SSD_GROUPS * SSD_STATE
MLA_HEADS = 8
MLA_NOPE = 128
MLA_ROPE = ROPE_DIM
MLA_V = 128
MLA_Q_LORA = 512
MLA_KV_LORA = 256
MLA_WIDTH = MLA_HEADS * MLA_V
GQA_HEADS = 16
GQA_KV_HEADS = 2
GQA_HEAD_DIM = ROPE_DIM
GQA_WIDTH = GQA_HEADS * GQA_HEAD_DIM
GQA_KV_WIDTH = GQA_KV_HEADS * GQA_HEAD_DIM
WINDOW = 128
HY_WIDTH = BRANCH_WIDTH
HY_POS_EMB = 33
HY_BANDS = (HY_POS_EMB - 1) // 2
HY_FILTER_HIDDEN = 64
HY_FAST_DECAY = 0.3
HY_SLOW_DECAY = 1.5
HY_DECAY_TARGET = 0.01
LN_EPS = 1e-6
NEG_INF = -1e30
DEEPNORM_ALPHA = (2 * DEPTH) ** 0.25

IN_SPLITS = (
    SSD_WIDTH, SSD_CONV_CH, 2 * SSD_HEADS, MLA_Q_LORA, MLA_KV_LORA, MLA_ROPE, MLA_WIDTH, GQA_WIDTH,
    GQA_KV_WIDTH, GQA_KV_WIDTH, GQA_WIDTH, 3 * HY_WIDTH, HY_WIDTH, N_BRANCH * D_MODEL,
)

LANE = 128
ROW_TILE = 256
CHUNK = SSD_CHUNK
VMEM_LIMIT_BYTES = 56 * 1024 * 1024

P_MERGE = 0
P_HY_X0 = 8192
P_HY_X1 = 9216
P_HY_V = 10240
P_HY_Z = 11264
P_SSD_Z = 12288
P_SSD_X = 13312
P_MLA_Z = 14336
P_GQA_Q = 15360
P_GQA_Z = 16384
P_MLA_CQ = 17408
P_SSD_B = 17920
P_SSD_C = 18176
P_MLA_CKV = 18432
P_GQA_K = 18688
P_GQA_V = 18816
P_SMALL = 18944
P_WIDTH = 19072
SMALL_DT = 64

F32 = jnp.float32
BF16 = jnp.bfloat16


def _cparams(*sem):
    return pltpu.CompilerParams(dimension_semantics=sem, vmem_limit_bytes=VMEM_LIMIT_BYTES)


def _split3(v):
    h1 = v.astype(BF16)
    r1 = v - h1.astype(F32)
    h2 = r1.astype(BF16)
    h3 = (r1 - h2.astype(F32)).astype(BF16)
    return h1, h2, h3


def _dot(a, b):
    return jnp.dot(a, b, preferred_element_type=F32)


def _dot_nt(a, b):
    return lax.dot_general(a, b, (((1,), (1,)), ((), ())), preferred_element_type=F32)


def _dot_tn(a, b):
    return lax.dot_general(a, b, (((0,), (0,)), ((), ())), preferred_element_type=F32)


def _dot_exact_rhs(a, b_bf16, pieces):
    parts = _split3(a)[:pieces]
    out = _dot(parts[0], b_bf16)
    for p in parts[1:]:
        out = out + _dot(p, b_bf16)
    return out


def _dot3(a, b):
    a1, a2, _ = _split3(a)
    b1, b2, _ = _split3(b)
    return _dot(a1, b1) + (_dot(a1, b2) + _dot(a2, b1))


def _silu(v):
    return v * jax.nn.sigmoid(v)


def _softplus(v):
    return jnp.maximum(v, 0.0) + jnp.log(1.0 + jnp.exp(-jnp.abs(v)))


def _rotswap(t):
    lane = lax.broadcasted_iota(jnp.int32, t.shape, 1)
    return jnp.where((lane & 63) < 32, pltpu.roll(t, 96, 1), pltpu.roll(t, 32, 1))


def _rope(t, cos, sin_signed):
    return t * cos + _rotswap(t) * sin_signed


def _mm_kernel(a_ref, b_ref, o_ref):
    o_ref[...] = _dot(a_ref[...], b_ref[...]).astype(o_ref.dtype)


def _matmul(a, b, out_dtype, tm, tn):
    m, k = a.shape
    n = b.shape[1]
    tm = min(tm, m)
    tn = min(tn, n)
    return pl.pallas_call(
        _mm_kernel,
        grid=(pl.cdiv(m, tm), pl.cdiv(n, tn)),
        in_specs=[pl.BlockSpec((tm, k), lambda i, j: (i, 0)),
                  pl.BlockSpec((k, tn), lambda i, j: (0, j))],
        out_specs=pl.BlockSpec((tm, tn), lambda i, j: (i, j)),
        out_shape=jax.ShapeDtypeStruct((m, n), out_dtype),
        compiler_params=_cparams("parallel", "arbitrary"),
    )(a.astype(BF16), b.astype(BF16))


def _mod_row(i, tiles_per_sample, n_samples):
    return jnp.where(i % tiles_per_sample == 0, n_samples, i // tiles_per_sample)


def _ln_mod_kernel(x_ref, mod_ref, h_ref):
    x = x_ref[...]
    mu = jnp.mean(x, axis=-1, keepdims=True)
    xc = x - mu
    var = jnp.mean(xc * xc, axis=-1, keepdims=True)
    m = mod_ref[0]
    h_ref[...] = (xc * lax.rsqrt(var + LN_EPS) * (1.0 + m[1:2]) + m[0:1]).astype(h_ref.dtype)


def _ln_mod(xx, mod, tps, nb):
    tt, d = xx.shape
    return pl.pallas_call(
        _ln_mod_kernel,
        grid=(tt // ROW_TILE,),
        in_specs=[pl.BlockSpec((ROW_TILE, d), lambda i: (i, 0)),
                  pl.BlockSpec((1, 3, d), lambda i: (_mod_row(i, tps, nb), 0, 0))],
        out_specs=pl.BlockSpec((ROW_TILE, d), lambda i: (i, 0)),
        out_shape=jax.ShapeDtypeStruct((tt, d), BF16),
        compiler_params=_cparams("parallel"),
    )(xx, mod)


def _repack_w_in(w_in):
    parts, start = [], 0
    for size in IN_SPLITS:
        parts.append(w_in[:, start:start + size])
        start += size
    s_z, s_xbc, s_dt, m_cq, m_ckv, m_kpe, m_z, g_q, g_k, g_v, g_z, hy_xv, hy_z, merge = parts
    gn = SSD_GROUPS * SSD_STATE
    small = jnp.concatenate([m_kpe, s_dt, jnp.zeros((w_in.shape[0], LANE - MLA_ROPE - 2 * SSD_HEADS), w_in.dtype)],
                            axis=1)
    order = [merge, hy_xv, hy_z, s_z, s_xbc[:, :SSD_WIDTH], m_z, g_q, g_z, m_cq,
             s_xbc[:, SSD_WIDTH:SSD_WIDTH + gn], s_xbc[:, SSD_WIDTH + gn:], m_ckv, g_k, g_v, small]
    return jnp.concatenate(order, axis=1).astype(BF16)


def _conv_chunk(src_ref, w, bias, c, n_ctx_chunks, n_chunks):
    tb = n_chunks * CHUNK
    r0 = pl.multiple_of(c * CHUNK, CHUNK)
    cur = src_ref[pl.ds(r0, CHUNK), :].astype(F32)
    rp = pl.multiple_of(jnp.maximum(r0 - 16, 0), 16)
    rn = pl.multiple_of(jnp.minimum(r0 + CHUNK, tb - 16), 16)
    prev_row = src_ref[pl.ds(rp, 16), :].astype(F32)[15:16]
    next_row = src_ref[pl.ds(rn, 16), :].astype(F32)[0:1]
    has_prev = jnp.logical_and(c != 0, c != n_ctx_chunks)
    has_next = jnp.logical_and(c != n_ctx_chunks - 1, c != n_chunks - 1)
    prev_row = jnp.where(has_prev, prev_row, 0.0)
    next_row = jnp.where(has_next, next_row, 0.0)
    row = lax.broadcasted_iota(jnp.int32, cur.shape, 0)
    sp = jnp.where(row == 0, prev_row, pltpu.roll(cur, 1, 0))
    sn = jnp.where(row == CHUNK - 1, next_row, pltpu.roll(cur, CHUNK - 1, 0))
    return sp * w[0:1] + cur * w[1:2] + sn * w[2:3] + bias


def _ssd_kernel(x_ref, b_ref, c_ref, sm_ref, z_ref, cwx_ref, cbx_ref, cwb_ref, cbb_ref, cwc_ref, cbc_ref,
                dtb_ref, alog_ref, dsk_ref, nw_ref, o_ref, ux, ub, uc, yacc, st, *, n_ctx_chunks, n_chunks):
    g = pl.program_id(1)
    hp = ux.shape[1]
    nh = hp // SSD_HEAD_DIM

    def conv_body(c, carry):
        r0 = pl.multiple_of(c * CHUNK, CHUNK)
        ux[pl.ds(r0, CHUNK), :] = _silu(_conv_chunk(x_ref, cwx_ref[...], cbx_ref[...], c, n_ctx_chunks,
                                                    n_chunks)).astype(ux.dtype)
        ub[pl.ds(r0, CHUNK), :] = _silu(_conv_chunk(b_ref, cwb_ref[...], cbb_ref[...], c, n_ctx_chunks,
                                                    n_chunks)).astype(ub.dtype)
        uc[pl.ds(r0, CHUNK), :] = _silu(_conv_chunk(c_ref, cwc_ref[...], cbc_ref[...], c, n_ctx_chunks,
                                                    n_chunks)).astype(uc.dtype)
        return carry

    lax.fori_loop(0, n_chunks, conv_body, 0)

    ri = lax.broadcasted_iota(jnp.int32, (CHUNK, CHUNK), 0)
    ci = lax.broadcasted_iota(jnp.int32, (CHUNK, CHUNK), 1)
    lane_hp = lax.broadcasted_iota(jnp.int32, (CHUNK, hp), 1)
    lane128 = lax.broadcasted_iota(jnp.int32, (CHUNK, LANE), 1)
    a_all = -jnp.exp(alog_ref[0])
    dtb = dtb_ref[0]

    def run_direction(dirn):
        tri = (ci <= ri) if dirn == 0 else (ci >= ri)
        tri_bf = jnp.where(tri, 1.0, 0.0).astype(BF16)
        expand = jnp.where(lax.broadcasted_iota(jnp.int32, (LANE, hp), 0)
                           == dirn * 16 + (lax.broadcasted_iota(jnp.int32, (LANE, hp), 1) >> 6),
                           1.0, 0.0).astype(BF16)
        last = CHUNK - 1 if dirn == 0 else 0
        st[...] = jnp.zeros_like(st)

        def body(i, carry):
            if dirn == 0:
                c = i
            else:
                c = jnp.where(i < n_ctx_chunks, n_ctx_chunks - 1 - i, n_chunks - 1 + n_ctx_chunks - i)
            r0 = pl.multiple_of(c * CHUNK, CHUNK)
            xs = ux[pl.ds(r0, CHUNK), :].astype(F32)
            bc = ub[pl.ds(r0, CHUNK), :]
            cc = uc[pl.ds(r0, CHUNK), :]
            sm = sm_ref[pl.ds(r0, CHUNK), :].astype(F32)
            sm = jnp.where(g == 0, pltpu.roll(sm, LANE - SMALL_DT, 1), pltpu.roll(sm, LANE - SMALL_DT - nh, 1))
            dt = _softplus(sm + dtb)
            la = dt * a_all
            cum = _dot_exact_rhs_lhs(tri_bf, la)
            cx = _dot_exact_rhs(cum, expand, 2)
            dtx = _dot_exact_rhs(dt, expand, 2)
            cx_last = cx[last:last + 1]
            ecum = jnp.exp(cx)
            dec_end = jnp.exp(cx_last - cx)
            xd = xs * dtx
            xd_bf = xd.astype(BF16)
            gmat = _dot_nt(cc, bc)
            cum_t = cum.T
            cols = []
            for j in range(nh // 2):
                sc = []
                for h in (2 * j, 2 * j + 1):
                    k = dirn * 16 + h
                    diff = cum[:, k:k + 1] - cum_t[k:k + 1, :]
                    lmat = jnp.exp(jnp.where(tri, diff, NEG_INF))
                    sc.append((gmat * lmat).astype(BF16))
                xcol = xd_bf[:, j * LANE:(j + 1) * LANE]
                lo = lane128 < SSD_HEAD_DIM
                rhs = jnp.concatenate([jnp.where(lo, xcol, jnp.zeros_like(xcol)),
                                       jnp.where(lo, jnp.zeros_like(xcol), xcol)], axis=0)
                cols.append(_dot(jnp.concatenate(sc, axis=1), rhs))
            y = jnp.concatenate(cols, axis=1)
            s_prev = st[...]
            y = y + _dot(cc, s_prev.astype(BF16)) * ecum
            st[...] = jnp.exp(cx_last) * s_prev + _dot_tn(bc, (xd * dec_end).astype(BF16))
            if dirn == 0:
                yacc[pl.ds(r0, CHUNK), :] = y + xs * dsk_ref[...]
            else:
                tot = (yacc[pl.ds(r0, CHUNK), :] + y) * _silu(z_ref[pl.ds(r0, CHUNK), :].astype(F32))
                ms = jnp.mean(tot * tot, axis=-1, keepdims=True)
                o_ref[pl.ds(r0, CHUNK), :] = (tot * lax.rsqrt(ms + LN_EPS) * nw_ref[...]).astype(o_ref.dtype)
            return carry

        lax.fori_loop(0, n_chunks, body, 0)

    run_direction(0)
    run_direction(1)


def _dot_exact_rhs_lhs(tri_bf, v):
    v1, v2, v3 = _split3(v)
    return _dot(tri_bf, v1) + (_dot(tri_bf, v2) + _dot(tri_bf, v3))


def _ssd(p3, conv_w, conv_b, dt_bias, a_log, d_skip, norm_w, n_ctx):
    nb, tb, _ = p3.shape
    hp = SSD_WIDTH // SSD_GROUPS
    n_chunks = tb // CHUNK
    gn = SSD_GROUPS * SSD_STATE
    cwx, cwb, cwc = conv_w[:, :SSD_WIDTH], conv_w[:, SSD_WIDTH:SSD_WIDTH + gn], conv_w[:, SSD_WIDTH + gn:]
    cb = conv_b[None, :]
    cbx, cbb, cbc = cb[:, :SSD_WIDTH], cb[:, SSD_WIDTH:SSD_WIDTH + gn], cb[:, SSD_WIDTH + gn:]
    hg = SSD_HEADS // SSD_GROUPS

    def per_group(v):
        v = v.reshape(2, SSD_GROUPS, hg).transpose(1, 0, 2)
        v = jnp.pad(v, ((0, 0), (0, 0), (0, 16 - hg))).reshape(SSD_GROUPS, 1, 32)
        return jnp.pad(v, ((0, 0), (0, 0), (0, LANE - 32)))

    dsk = jnp.repeat(d_skip, SSD_HEAD_DIM)[None, :]
    kern = functools.partial(_ssd_kernel, n_ctx_chunks=n_ctx // CHUNK, n_chunks=n_chunks)
    col = lambda off, w: (lambda b, g: (b, 0, off // w + g))
    par = lambda b, g: (0, g)
    return pl.pallas_call(
        kern,
        grid=(nb, SSD_GROUPS),
        in_specs=[pl.BlockSpec((None, tb, hp), col(P_SSD_X, hp)),
                  pl.BlockSpec((None, tb, SSD_STATE), col(P_SSD_B, SSD_STATE)),
                  pl.BlockSpec((None, tb, SSD_STATE), col(P_SSD_C, SSD_STATE)),
                  pl.BlockSpec((None, tb, LANE), lambda b, g: (b, 0, P_SMALL // LANE)),
                  pl.BlockSpec((None, tb, hp), col(P_SSD_Z, hp)),
                  pl.BlockSpec((3, hp), par), pl.BlockSpec((1, hp), par),
                  pl.BlockSpec((3, SSD_STATE), par), pl.BlockSpec((1, SSD_STATE), par),
                  pl.BlockSpec((3, SSD_STATE), par), pl.BlockSpec((1, SSD_STATE), par),
                  pl.BlockSpec((1, 1, LANE), lambda b, g: (g, 0, 0)),
                  pl.BlockSpec((1, 1, LANE), lambda b, g: (g, 0, 0)),
                  pl.BlockSpec((1, hp), par), pl.BlockSpec((1, hp), par)],
        out_specs=pl.BlockSpec((None, tb, hp), lambda b, g: (b, 0, g)),
        out_shape=jax.ShapeDtypeStruct((nb, tb, SSD_WIDTH), BF16),
        scratch_shapes=[pltpu.VMEM((tb, hp), BF16), pltpu.VMEM((tb, SSD_STATE), BF16),
                        pltpu.VMEM((tb, SSD_STATE), BF16), pltpu.VMEM((tb, hp), F32),
                        pltpu.VMEM((SSD_STATE, hp), F32)],
        compiler_params=_cparams("parallel", "parallel"),
    )(p3, p3, p3, p3, p3, cwx, cbx, cwb, cbb, cwc, cbc, per_group(dt_bias), per_group(a_log), dsk,
      norm_w[None, :])


def _rope_tables(n_ctx, n_lat):
    rows = n_lat // GRID_W
    row = jnp.repeat(jnp.arange(rows, dtype=F32), GRID_W)
    colp = jnp.tile(jnp.arange(GRID_W, dtype=F32), rows)
    n_freq = ROPE_DIM // 4
    inv_freq = ROPE_BASE ** (-jnp.arange(n_freq, dtype=F32) / n_freq)
    ang = jnp.concatenate([row[:, None] * inv_freq[None, :], colp[:, None] * inv_freq[None, :]], axis=-1)
    cos, sin = jnp.cos(ang), jnp.sin(ang)
    cos2 = jnp.tile(cos, (1, 4))
    sin2 = jnp.tile(jnp.concatenate([-sin, sin], axis=-1), (1, 2))
    cos2 = jnp.concatenate([jnp.ones((n_ctx, LANE), F32), cos2], axis=0)
    sin2 = jnp.concatenate([jnp.zeros((n_ctx, LANE), F32), sin2], axis=0)
    return cos2, sin2


def _attn_prep_kernel(cq_ref, ckv_ref, sm_ref, gk_ref, gv_ref, cos_ref, sin_ref, qn_ref, wq_ref, kn_ref,
                      wk_ref, wv_ref, q_ref, k_ref, v_ref, kg_ref, vg_ref):
    cos, sin = cos_ref[...], sin_ref[...]
    lane = lax.broadcasted_iota(jnp.int32, cos.shape, 1)
    scale = (MLA_NOPE + MLA_ROPE) ** -0.5 * math.log2(math.e)
    cq = cq_ref[...].astype(F32)
    cqn = cq * lax.rsqrt(jnp.mean(cq * cq, axis=-1, keepdims=True) + LN_EPS) * qn_ref[...]
    q = _dot(cqn.astype(BF16), wq_ref[...])
    ckv = ckv_ref[...].astype(F32)
    ckvn = (ckv * lax.rsqrt(jnp.mean(ckv * ckv, axis=-1, keepdims=True) + LN_EPS) * kn_ref[...]).astype(BF16)
    kn = _dot(ckvn, wk_ref[...])
    v_ref[...] = _dot(ckvn, wv_ref[...]).astype(v_ref.dtype)
    kpe = jnp.where(lane < MLA_ROPE, _rope(sm_ref[...].astype(F32), cos, sin), 0.0).astype(k_ref.dtype)
    for h in range(MLA_HEADS):
        o = 2 * h * LANE
        q_ref[:, o:o + LANE] = (q[:, o:o + LANE] * scale).astype(q_ref.dtype)
        q_ref[:, o + LANE:o + 2 * LANE] = (_rope(q[:, o + LANE:o + 2 * LANE], cos, sin) * scale).astype(q_ref.dtype)
        k_ref[:, o:o + LANE] = kn[:, h * LANE:(h + 1) * LANE].astype(k_ref.dtype)
        k_ref[:, o + LANE:o + 2 * LANE] = kpe
    kg_ref[...] = _rope(gk_ref[...].astype(F32), cos, sin).astype(kg_ref.dtype)
    vg_ref[...] = gv_ref[...].astype(F32).T.astype(vg_ref.dtype)


def _attn_prep(p, cos2, sin2, q_norm, w_uq, kv_norm, w_ukv, tps):
    tt = p.shape[0]
    hq = MLA_NOPE + MLA_ROPE
    wq = w_uq.reshape(MLA_Q_LORA, MLA_HEADS, hq)
    wq = jnp.pad(wq, ((0, 0), (0, 0), (0, 2 * LANE - hq))).reshape(MLA_Q_LORA, MLA_HEADS * 2 * LANE).astype(BF16)
    wkv = w_ukv.reshape(MLA_KV_LORA, MLA_HEADS, MLA_NOPE + MLA_V)
    wk = wkv[:, :, :MLA_NOPE].reshape(MLA_KV_LORA, MLA_HEADS * MLA_NOPE).astype(BF16)
    wv = wkv[:, :, MLA_NOPE:].reshape(MLA_KV_LORA, MLA_HEADS * MLA_V).astype(BF16)
    full = lambda shape: pl.BlockSpec(shape, lambda i: (0, 0))
    pcol = lambda off, w: pl.BlockSpec((ROW_TILE, w), lambda i: (i, off // w))
    rows = lambda w: pl.BlockSpec((ROW_TILE, w), lambda i: (i, 0))
    tab = pl.BlockSpec((ROW_TILE, LANE), lambda i: (i % tps, 0))
    return pl.pallas_call(
        _attn_prep_kernel,
        grid=(tt // ROW_TILE,),
        in_specs=[pcol(P_MLA_CQ, MLA_Q_LORA), pcol(P_MLA_CKV, MLA_KV_LORA), pcol(P_SMALL, LANE),
                  pcol(P_GQA_K, LANE), pcol(P_GQA_V, LANE), tab, tab,
                  full((1, MLA_Q_LORA)), full(wq.shape), full((1, MLA_KV_LORA)), full(wk.shape), full(wv.shape)],
        out_specs=[rows(MLA_HEADS * 2 * LANE), rows(MLA_HEADS * 2 * LANE), rows(MLA_WIDTH),
                   rows(LANE), pl.BlockSpec((LANE, ROW_TILE), lambda i: (0, i))],
        out_shape=[jax.ShapeDtypeStruct((tt, MLA_HEADS * 2 * LANE), BF16),
                   jax.ShapeDtypeStruct((tt, MLA_HEADS * 2 * LANE), BF16),
                   jax.ShapeDtypeStruct((tt, MLA_WIDTH), BF16),
                   jax.ShapeDtypeStruct((tt, LANE), BF16),
                   jax.ShapeDtypeStruct((LANE, tt), BF16)],
        compiler_params=_cparams("parallel"),
        name="attn_prep",
    )(p, p, p, p, p, cos2, sin2, q_norm[None, :], wq, kv_norm[None, :], wk, wv)


MLA_Q_CHUNK = 256


def _mla_kernel(q_ref, k_ref, v_ref, z_ref, o_ref, *, n_ctx):
    tb = k_ref.shape[0]

    def attend(r0, rows, n_keys):
        s = _dot_nt(q_ref[r0:r0 + rows, :], k_ref[0:n_keys, :])
        m = jnp.max(s, axis=-1, keepdims=True)
        p = jnp.exp2(s - m)
        l = jnp.sum(p, axis=-1, keepdims=True)
        o = _dot(p.astype(BF16), v_ref[0:n_keys, :]) / l
        o_ref[r0:r0 + rows, :] = (o * _silu(z_ref[r0:r0 + rows, :].astype(F32))).astype(o_ref.dtype)

    attend(0, n_ctx, n_ctx)
    for r0 in range(n_ctx, tb, MLA_Q_CHUNK):
        attend(r0, MLA_Q_CHUNK, tb)


def _mla(qm3, km3, vm3, p3, n_ctx):
    nb, tb, _ = p3.shape
    assert (tb - n_ctx) % MLA_Q_CHUNK == 0
    blk = lambda w, off: pl.BlockSpec((None, tb, w), lambda b, h: (b, 0, off // w + h))
    return pl.pallas_call(
        functools.partial(_mla_kernel, n_ctx=n_ctx),
        grid=(nb, MLA_HEADS),
        in_specs=[blk(2 * LANE, 0), blk(2 * LANE, 0), blk(LANE, 0), blk(LANE, P_MLA_Z)],
        out_specs=blk(LANE, 0),
        out_shape=jax.ShapeDtypeStruct((nb, tb, MLA_WIDTH), BF16),
        compiler_params=_cparams("parallel", "parallel"),
        name="mla_attention",
    )(qm3, km3, vm3, p3)


GQA_HEAD_BATCH = 8

def _gqa_kernel(sink_ref, q_ref, kp_ref, ko_ref, kn_ref, kc_ref, vp_ref, vo_ref, vn_ref, vc_ref, cos_ref,
                sin_ref, z_ref, o_ref, *, n_ctx_blocks):
    blk = pl.program_id(1)
    n_blk = pl.num_programs(1)
    is_lat = blk >= n_ctx_blocks
    has_prev = blk > n_ctx_blocks
    has_next = jnp.logical_and(is_lat, blk < n_blk - 1)
    cos, sin = cos_ref[...], sin_ref[...]
    log2e = math.log2(math.e)
    scale = GQA_HEAD_DIM ** -0.5 * log2e
    w = q_ref.shape[0]
    hd = GQA_HEAD_DIM
    hpk = GQA_HEADS // GQA_KV_HEADS
    q = jnp.concatenate([_rope(q_ref[:, c * LANE:(c + 1) * LANE].astype(F32), cos, sin) * scale
                         for c in range(GQA_WIDTH // LANE)], axis=1)
    qt = q.T.astype(BF16)
    kall = jnp.concatenate([kp_ref[...], ko_ref[...], kn_ref[...], kc_ref[...]], axis=0)
    vt = jnp.concatenate([vp_ref[...], vo_ref[...], vn_ref[...], vc_ref[...]], axis=1)
    kj = lax.broadcasted_iota(jnp.int32, (w, w), 0)
    qi = lax.broadcasted_iota(jnp.int32, (w, w), 1)
    m_prev = jnp.logical_and(kj >= qi, has_prev)
    m_next = jnp.logical_and(kj <= qi, has_next)
    zeros = jnp.zeros((hd, w), BF16)
    nbat = GQA_HEAD_BATCH
    tile = lambda a: jnp.concatenate([a] * nbat, axis=1)
    m_prev, m_next = tile(m_prev), tile(m_next)
    outs = []
    for h0 in range(0, GQA_HEADS, nbat):
        kh = h0 // hpk
        heads = range(h0, h0 + nbat)
        rhs = jnp.concatenate(
            [jnp.concatenate([qt[h * hd:(h + 1) * hd], zeros] if kh == 0 else [zeros, qt[h * hd:(h + 1) * hd]],
                             axis=0) for h in heads], axis=1)
        st = _dot(kall, rhs)
        s_p = jnp.where(m_prev, st[0:w], NEG_INF)
        s_o = jnp.where(is_lat, st[w:2 * w], NEG_INF)
        s_n = jnp.where(m_next, st[2 * w:3 * w], NEG_INF)
        s = jnp.concatenate([s_p, s_o, s_n, st[3 * w:]], axis=0)
        sink = jnp.concatenate([jnp.full((1, w), sink_ref[h] * log2e, F32) for h in heads], axis=1)
        m = jnp.maximum(jnp.max(s, axis=0, keepdims=True), sink)
        pr = jnp.exp2(s - m)
        l = jnp.sum(pr, axis=0, keepdims=True) + jnp.exp2(sink - m)
        ot = _dot(vt[kh * hd:(kh + 1) * hd], pr.astype(BF16)) / l
        outs.extend(ot[:, i * w:(i + 1) * w] for i in range(nbat))
    y = jnp.concatenate(outs, axis=0).T
    o_ref[...] = (y * _silu(z_ref[...].astype(F32))).astype(o_ref.dtype)


def _gqa(p, kg, vgt, cos2, sin2, sink, nb, tb, n_ctx):
    w = WINDOW
    bps = tb // w
    row = lambda b, i: b * bps + i
    prev = lambda b, i: b * bps + jnp.maximum(i - 1, 0)
    nxt = lambda b, i: b * bps + jnp.minimum(i + 1, bps - 1)
    kspec = lambda f: pl.BlockSpec((w, LANE), lambda b, i: (f(b, i), 0))
    vspec = lambda f: pl.BlockSpec((LANE, w), lambda b, i: (0, f(b, i)))
    kctx = pl.BlockSpec((n_ctx, LANE), lambda b, i: (b * (tb // n_ctx), 0))
    vctx = pl.BlockSpec((LANE, n_ctx), lambda b, i: (0, b * (tb // n_ctx)))
    tab = pl.BlockSpec((w, LANE), lambda b, i: (i, 0))
    return pl.pallas_call(
        functools.partial(_gqa_kernel, n_ctx_blocks=n_ctx // w),
        grid=(nb, bps),
        in_specs=[pl.BlockSpec(memory_space=pltpu.SMEM),
                  pl.BlockSpec((w, GQA_WIDTH), lambda b, i: (row(b, i), P_GQA_Q // GQA_WIDTH)),
                  kspec(prev), kspec(row), kspec(nxt), kctx, vspec(prev), vspec(row), vspec(nxt), vctx, tab, tab,
                  pl.BlockSpec((w, GQA_WIDTH), lambda b, i: (row(b, i), P_GQA_Z // GQA_WIDTH))],
        out_specs=pl.BlockSpec((w, GQA_WIDTH), lambda b, i: (row(b, i), 0)),
        out_shape=jax.ShapeDtypeStruct((nb * tb, GQA_WIDTH), BF16),
        compiler_params=_cparams("parallel", "arbitrary"),
        name="gqa_window_attention",
    )(sink, p, kg, kg, kg, kg, vgt, vgt, vgt, vgt, cos2, sin2, p)


def _dft_tables(n):
    m = jnp.arange(n, dtype=jnp.int32)[None, :]

    def small(kvals):
        ang = ((kvals[:, None] * m) % (2 * n)).astype(F32) * (math.pi / n)
        return jnp.cos(ang), jnp.sin(ang)

    ca, sa = small(64 * jnp.arange(n // 64, dtype=jnp.int32))
    cb, sb = small(jnp.arange(64, dtype=jnp.int32))
    cos_t = ca[:, None, :] * cb[None, :, :] - sa[:, None, :] * sb[None, :, :]
    sin_t = sa[:, None, :] * cb[None, :, :] + ca[:, None, :] * sb[None, :, :]
    return cos_t.reshape(n, n).astype(BF16), sin_t.reshape(n, n).astype(BF16)


def _hyena_feats(n):
    t = jnp.linspace(0.0, 1.0, n, dtype=F32)[:, None]
    w_ang = (2.0 * math.pi / n) * jnp.arange(n, dtype=F32)[:, None]
    bands = jnp.linspace(1e-4, HY_BANDS - 1, HY_BANDS, dtype=F32)[None, :]
    feats = jnp.concatenate([t, jnp.cos(bands * w_ang), -jnp.sin(bands * w_ang)], axis=-1)
    return jnp.pad(feats, ((0, 0), (0, LANE - HY_POS_EMB)))


def _hyena_filter_kernel(feats_ref, w1_ref, b1_ref, w2_ref, b2_ref, w3f_ref, w3b_ref, freq_ref, dl_ref,
                         cos_ref, sin_ref, kr_ref, ks_ref, kq_ref, hs, hd, *, n):
    kt = pl.program_id(1)
    tk = cos_ref.shape[0]

    @pl.when(kt == 0)
    def _():
        feats = feats_ref[...]
        t = feats[:, 0:1]
        hdn = jnp.sin(freq_ref[0:1, :] * (_dot3(feats, w1_ref[...]) + b1_ref[...]))
        hdn = jnp.sin(freq_ref[1:2, :] * (_dot3(hdn, w2_ref[...]) + b2_ref[...]))
        decay = jnp.exp(-t * dl_ref[...])
        hf = _dot3(hdn, w3f_ref[...]) * decay
        hb = _dot3(hdn, w3b_ref[...]) * decay
        row = lax.broadcasted_iota(jnp.int32, hb.shape, 0)
        hb = jnp.where(row == 0, 0.0, hb)
        hsum = hf + hb
        hs[...] = hsum.astype(hs.dtype)
        hd[...] = (hf - hb).astype(hd.dtype)
        alt = jnp.where((row & 1) == 0, 1.0, -1.0)
        kq_ref[...] = jnp.sum(hsum * alt, axis=0, keepdims=True) * (1.0 / (2 * n))

    kidx = kt * tk + lax.broadcasted_iota(jnp.int32, (tk, 1), 0)
    wk = jnp.where(kidx == 0, 0.5 / n, 1.0 / n)
    kr_ref[...] = _dot(cos_ref[...], hs[...]) * wk
    ks_ref[...] = _dot(sin_ref[...], hd[...]) * wk


def _hyena_filter(n, cos_t, sin_t, w1, b1, w2, b2, w3, freq):
    cbw = 512
    tk = min(256, n)
    deltas = jnp.abs(jnp.linspace(math.log(HY_DECAY_TARGET) / HY_FAST_DECAY,
                                  math.log(HY_DECAY_TARGET) / HY_SLOW_DECAY, HY_WIDTH, dtype=F32))[None, :]
    w1p = jnp.pad(w1, ((0, LANE - HY_POS_EMB), (0, 0)))
    full = lambda a: pl.BlockSpec(a.shape, lambda c, k: (0,) * a.ndim)
    nblk = HY_WIDTH // cbw
    kern = functools.partial(_hyena_filter_kernel, n=n)
    feats = _hyena_feats(n)
    return pl.pallas_call(
        kern,
        grid=(nblk, n // tk),
        in_specs=[full(feats), full(w1p), pl.BlockSpec((1, HY_FILTER_HIDDEN), lambda c, k: (0, 0)),
                  full(w2), pl.BlockSpec((1, HY_FILTER_HIDDEN), lambda c, k: (0, 0)),
                  pl.BlockSpec((HY_FILTER_HIDDEN, cbw), lambda c, k: (0, c)),
                  pl.BlockSpec((HY_FILTER_HIDDEN, cbw), lambda c, k: (0, nblk + c)),
                  full(freq), pl.BlockSpec((1, cbw), lambda c, k: (0, c)),
                  pl.BlockSpec((tk, n), lambda c, k: (k, 0)), pl.BlockSpec((tk, n), lambda c, k: (k, 0))],
        out_specs=[pl.BlockSpec((tk, cbw), lambda c, k: (k, c)), pl.BlockSpec((tk, cbw), lambda c, k: (k, c)),
                   pl.BlockSpec((1, cbw), lambda c, k: (0, c))],
        out_shape=[jax.ShapeDtypeStruct((n, HY_WIDTH), F32), jax.ShapeDtypeStruct((n, HY_WIDTH), F32),
                   jax.ShapeDtypeStruct((1, HY_WIDTH), F32)],
        scratch_shapes=[pltpu.VMEM((n, cbw), BF16), pltpu.VMEM((n, cbw), BF16)],
        compiler_params=_cparams("parallel", "arbitrary"),
    )(feats, w1p, b1[None, :], w2, b2[None, :], w3, w3, freq, deltas, cos_t, sin_t)


def _spectral_product(a, bq, kr, ks):
    return (a * kr - bq * ks).astype(BF16), (a * ks + bq * kr).astype(BF16)


def _hyena_kernel(x0_ref, x1_ref, v_ref, z_ref, cw0_ref, cb0_ref, cw1_ref, cb1_ref, cwv_ref, cbv_ref, d_ref,
                  cr_ref, sr_ref, cc_ref, sc_ref, kr_ref, ks_ref, kq_ref,
                  ccx_ref, scx_ref, krx_ref, ksx_ref, kqx_ref, o_ref, u, g0, yacc,
                  *, n_ctx, n_chunks, with_ctx):
    kt = pl.program_id(2)
    n_kt = pl.num_programs(2)
    n_ctx_chunks = n_ctx // CHUNK
    tb = n_chunks * CHUNK

    @pl.when(kt == 0)
    def _():
        def conv_body(c, carry):
            r0 = pl.multiple_of(c * CHUNK, CHUNK)
            x1c = _conv_chunk(x1_ref, cw1_ref[...], cb1_ref[...], c, n_ctx_chunks, n_chunks)
            vc = _conv_chunk(v_ref, cwv_ref[...], cbv_ref[...], c, n_ctx_chunks, n_chunks)
            u[pl.ds(r0, CHUNK), :] = (x1c * vc).astype(u.dtype)
            x0c = _conv_chunk(x0_ref, cw0_ref[...], cb0_ref[...], c, n_ctx_chunks, n_chunks)
            g0[pl.ds(r0, CHUNK), :] = (x0c * _silu(z_ref[pl.ds(r0, CHUNK), :].astype(F32))).astype(g0.dtype)
            return carry

        lax.fori_loop(0, n_chunks, conv_body, 0)
        yacc[...] = jnp.zeros_like(yacc)

    ul = u[n_ctx:tb, :]
    yr, ys = _spectral_product(_dot(cr_ref[...], ul), _dot(sr_ref[...], ul), kr_ref[...], ks_ref[...])
    yacc[...] += _dot(cc_ref[...], yr) + _dot(sc_ref[...], ys)

    @pl.when(kt == n_kt - 1)
    def _():
        def finish(uv, y, kq, r_lo, r_hi):
            uf = uv.astype(F32)
            alt = jnp.where((lax.broadcasted_iota(jnp.int32, uf.shape, 0) & 1) == 0, 1.0, -1.0)
            unyq = jnp.sum(uf * alt, axis=0, keepdims=True)
            y = y + alt * (unyq * kq) + uf * d_ref[...]
            o_ref[r_lo:r_hi, :] = (y * g0[r_lo:r_hi, :].astype(F32)).astype(o_ref.dtype)

        finish(ul, yacc[...], kq_ref[...], n_ctx, tb)
        if with_ctx:
            uc = u[0:n_ctx, :]
            yr, ys = _spectral_product(_dot(ccx_ref[...], uc), _dot(scx_ref[...], uc), krx_ref[...], ksx_ref[...])
            finish(uc, _dot(ccx_ref[...], yr) + _dot(scx_ref[...], ys), kqx_ref[...], 0, n_ctx)
        else:
            o_ref[0:n_ctx, :] = jnp.zeros((n_ctx, o_ref.shape[1]), o_ref.dtype)


def _hyena(p3, conv_w, conv_b, d_skip, tabs, filt, tabs_c, filt_c, n_ctx, with_ctx):
    nb, tb, _ = p3.shape
    n_lat = tb - n_ctx
    cbw = 512
    tk = 256
    cos_t, sin_t = tabs
    kr, ks, kq = filt
    cos_c, sin_c = tabs_c
    krc, ksc, kqc = filt_c
    cb = conv_b[None, :]
    one = pl.Buffered(1)
    pcol = lambda off: pl.BlockSpec((None, tb, cbw), lambda b, c, k: (b, 0, off // cbw + c), pipeline_mode=one)
    par = lambda rows, sec: pl.BlockSpec((rows, cbw), lambda b, c, k: (0, sec * (HY_WIDTH // cbw) + c))
    kern = functools.partial(_hyena_kernel, n_ctx=n_ctx, n_chunks=tb // CHUNK, with_ctx=with_ctx)
    small = lambda a: pl.BlockSpec(a.shape, lambda b, c, k: (0, 0))
    chan = lambda rows: pl.BlockSpec((rows, cbw), lambda b, c, k: (0, c))
    return pl.pallas_call(
        kern,
        grid=(nb, HY_WIDTH // cbw, n_lat // tk),
        in_specs=[pcol(P_HY_X0), pcol(P_HY_X1), pcol(P_HY_V), pcol(P_HY_Z),
                  par(3, 0), par(1, 0), par(3, 1), par(1, 1), par(3, 2), par(1, 2), chan(1),
                  pl.BlockSpec((tk, n_lat), lambda b, c, k: (k, 0)), pl.BlockSpec((tk, n_lat), lambda b, c, k: (k, 0)),
                  pl.BlockSpec((n_lat, tk), lambda b, c, k: (0, k)), pl.BlockSpec((n_lat, tk), lambda b, c, k: (0, k)),
                  pl.BlockSpec((tk, cbw), lambda b, c, k: (k, c)), pl.BlockSpec((tk, cbw), lambda b, c, k: (k, c)),
                  chan(1), small(cos_c), small(sin_c), chan(n_ctx), chan(n_ctx), chan(1)],
        out_specs=pl.BlockSpec((None, tb, cbw), lambda b, c, k: (b, 0, c)),
        out_shape=jax.ShapeDtypeStruct((nb, tb, HY_WIDTH), BF16),
        scratch_shapes=[pltpu.VMEM((tb, cbw), BF16), pltpu.VMEM((tb, cbw), BF16), pltpu.VMEM((n_lat, cbw), F32)],
        compiler_params=_cparams("parallel", "parallel", "arbitrary"),
    )(p3, p3, p3, p3, conv_w, cb, conv_w, cb, conv_w, cb, d_skip[None, :], cos_t, sin_t, cos_t, sin_t,
      kr, ks, kq, cos_c, sin_c, krc, ksc, kqc)


def _mix_kernel(g0_ref, g1_ref, g2_ref, g3_ref, lg_ref, w_ref, o_ref, acc):
    k = pl.program_id(1)
    for kk, g_ref in enumerate((g0_ref, g1_ref, g2_ref, g3_ref)):
        @pl.when(k == kk)
        def _(g_ref=g_ref, kk=kk):
            contrib = jax.nn.sigmoid(lg_ref[...].astype(F32)) * _dot(g_ref[...], w_ref[0])
            if kk == 0:
                acc[...] = contrib
            else:
                acc[...] += contrib

    @pl.when(k == N_BRANCH - 1)
    def _():
        o_ref[...] = acc[...].astype(o_ref.dtype)


def _mix(gated, p, w_branch_bf):
    tt = p.shape[0]
    tm = 512
    assert tt % tm == 0
    g_spec = pl.BlockSpec((tm, BRANCH_WIDTH), lambda i, k: (i, 0))
    return pl.pallas_call(
        _mix_kernel,
        grid=(tt // tm, N_BRANCH),
        in_specs=[g_spec, g_spec, g_spec, g_spec,
                  pl.BlockSpec((tm, D_MODEL), lambda i, k: (i, P_MERGE // D_MODEL + k)),
                  pl.BlockSpec((1, BRANCH_WIDTH, D_MODEL), lambda i, k: (k, 0, 0))],
        out_specs=pl.BlockSpec((tm, D_MODEL), lambda i, k: (i, 0)),
        out_shape=jax.ShapeDtypeStruct((tt, D_MODEL), BF16),
        scratch_shapes=[pltpu.VMEM((tm, D_MODEL), F32)],
        compiler_params=_cparams("parallel", "arbitrary"),
    )(*gated, p, w_branch_bf)


def _out_kernel(a_ref, w_ref, x_ref, mod_ref, g_ref, b_ref, *rest, with_next):
    mixed = _dot(a_ref[...], w_ref[...])
    r = DEEPNORM_ALPHA * x_ref[...] + mod_ref[0][2:3] * mixed
    mu = jnp.mean(r, axis=-1, keepdims=True)
    rc = r - mu
    var = jnp.mean(rc * rc, axis=-1, keepdims=True)
    xn = rc * lax.rsqrt(var + LN_EPS) * g_ref[...] + b_ref[...]
    if with_next:
        modn_ref, o_ref, h_ref = rest
        o_ref[...] = xn
        mu = jnp.mean(xn, axis=-1, keepdims=True)
        xc = xn - mu
        var = jnp.mean(xc * xc, axis=-1, keepdims=True)
        mn = modn_ref[0]
        h_ref[...] = (xc * lax.rsqrt(var + LN_EPS) * (1.0 + mn[1:2]) + mn[0:1]).astype(h_ref.dtype)
    else:
        (o_ref,) = rest
        o_ref[...] = xn


def _out_next(mixed, w_out_bf, xx, mod, ln_g, ln_b, mod_next, tps, nb):
    tt, d = xx.shape
    tile = pl.BlockSpec((ROW_TILE, d), lambda i: (i, 0))
    modspec = pl.BlockSpec((1, 3, d), lambda i: (_mod_row(i, tps, nb), 0, 0))
    vec = pl.BlockSpec((1, d), lambda i: (0, 0))
    return pl.pallas_call(
        functools.partial(_out_kernel, with_next=True),
        grid=(tt // ROW_TILE,),
        in_specs=[tile, pl.BlockSpec((d, d), lambda i: (0, 0)), tile, modspec, vec, vec, modspec],
        out_specs=[tile, tile],
        out_shape=[jax.ShapeDtypeStruct((tt, d), F32), jax.ShapeDtypeStruct((tt, d), BF16)],
        compiler_params=_cparams("parallel"),
    )(mixed, w_out_bf, xx, mod, ln_g[None, :], ln_b[None, :], mod_next)


def _out_last(mixed, w_out_bf, xx, mod, ln_g, ln_b, tps, nb, n_ctx):
    tt, d = xx.shape
    off = n_ctx // ROW_TILE
    tile = pl.BlockSpec((ROW_TILE, d), lambda b, j: (b * tps + off + j, 0))
    vec = pl.BlockSpec((1, d), lambda b, j: (0, 0))
    return pl.pallas_call(
        functools.partial(_out_kernel, with_next=False),
        grid=(nb, tps - off),
        in_specs=[tile, pl.BlockSpec((d, d), lambda b, j: (0, 0)), tile,
                  pl.BlockSpec((1, 3, d), lambda b, j: (b, 0, 0)), vec, vec],
        out_specs=pl.BlockSpec((None, ROW_TILE, d), lambda b, j: (b, j, 0)),
        out_shape=jax.ShapeDtypeStruct((nb, (tps - off) * ROW_TILE, d), F32),
        compiler_params=_cparams("parallel", "parallel"),
    )(mixed, w_out_bf, xx, mod, ln_g[None, :], ln_b[None, :])


def kernel(x, c, ctx, c_ctx, w_ada, b_ada, w_in, ssd_conv_w, ssd_conv_b, ssd_dt_bias, ssd_a_log, ssd_d,
           ssd_norm_w, mla_q_norm, mla_w_uq, mla_kv_norm, mla_w_ukv, gqa_sink, hy_conv_w, hy_conv_b,
           hy_w1, hy_b1, hy_w2, hy_b2, hy_w3, hy_freq, hy_d, w_branch, w_out, ln_g, ln_b):
    nb, n_lat, d = x.shape
    n_ctx = ctx.shape[1]
    tb = n_ctx + n_lat
    tt = nb * tb
    tps = tb // ROW_TILE
    assert d == D_MODEL and n_ctx == ROW_TILE and n_lat % 512 == 0 and n_lat % GRID_W == 0

    xx = jnp.concatenate([ctx, x], axis=1).reshape(tt, d)
    cos2, sin2 = _rope_tables(n_ctx, n_lat)
    tabs, tabs_c = _dft_tables(n_lat), _dft_tables(n_ctx)
    cvec = jax.nn.silu(jnp.concatenate([c, c_ctx[None]], axis=0))
    mods = [(_matmul(cvec, w_ada[i], F32, 8, 1024) + b_ada[i]).reshape(nb + 1, 3, d) for i in range(DEPTH)]

    h = _ln_mod(xx, mods[0], tps, nb)
    out = None
    for i in range(DEPTH):
        last = i == DEPTH - 1
        p = _matmul(h, _repack_w_in(w_in[i]), BF16, tb, 512)
        p3 = p.reshape(nb, tb, P_WIDTH)
        g_ssd = _ssd(p3, ssd_conv_w[i], ssd_conv_b[i], ssd_dt_bias[i], ssd_a_log[i], ssd_d[i], ssd_norm_w[i],
                     n_ctx).reshape(tt, SSD_WIDTH)
        qm, km, vm, kg, vg = _attn_prep(p, cos2, sin2, mla_q_norm[i], mla_w_uq[i], mla_kv_norm[i], mla_w_ukv[i],
                                        tps)
        r3 = lambda a: a.reshape(nb, tb, a.shape[-1])
        g_mla = _mla(r3(qm), r3(km), r3(vm), p3, n_ctx).reshape(tt, MLA_WIDTH)
        g_gqa = _gqa(p, kg, vg, cos2, sin2, gqa_sink[i], nb, tb, n_ctx)
        fargs = (hy_w1[i], hy_b1[i], hy_w2[i], hy_b2[i], hy_w3[i], hy_freq[i])
        filt = _hyena_filter(n_lat, *tabs, *fargs)
        filt_c = _hyena_filter(n_ctx, *tabs_c, *fargs)
        g_hy = _hyena(p3, hy_conv_w[i], hy_conv_b[i], hy_d[i], tabs, filt, tabs_c, filt_c, n_ctx,
                      not last).reshape(tt, HY_WIDTH)
        mixed = _mix((g_ssd, g_mla, g_gqa, g_hy), p, w_branch[i].astype(BF16))
        w_out_bf = w_out[i].astype(BF16)
        if last:
            out = _out_last(mixed, w_out_bf, xx, mods[i], ln_g[i], ln_b[i], tps, nb, n_ctx)
        else:
            xx, h = _out_next(mixed, w_out_bf, xx, mods[i], ln_g[i], ln_b[i], mods[i + 1], tps, nb)
    return out
```

```python
import functools
import math

import jax
import jax.numpy as jnp
from jax import lax
from jax.experimental import pallas as pl
from jax.experimental.pallas import tpu as pltpu

D_MODEL = 2048
DEPTH = 2
GRID_W = 64
N_BRANCH = 4
BRANCH_WIDTH = D_MODEL // 2
ROPE_DIM = 64
ROPE_BASE = 10000.0
SSD_WIDTH = BRANCH_WIDTH
SSD_HEAD_DIM = 64
SSD_HEADS = SSD_WIDTH // SSD_HEAD_DIM
SSD_GROUPS = 2
SSD_STATE = 128
SSD_CHUNK = 128
SSD_CONV_CH = SSD_WIDTH + 2 * SSD_GROUPS * SSD_STATE
MLA_HEADS = 8
MLA_NOPE = 128
MLA_ROPE = ROPE_DIM
MLA_V = 128
MLA_Q_LORA = 512
MLA_KV_LORA = 256
MLA_WIDTH = MLA_HEADS * MLA_V
GQA_HEADS = 16
GQA_KV_HEADS = 2
GQA_HEAD_DIM = ROPE_DIM
GQA_WIDTH = GQA_HEADS * GQA_HEAD_DIM
GQA_KV_WIDTH = GQA_KV_HEADS * GQA_HEAD_DIM
WINDOW = 128
HY_WIDTH = BRANCH_WIDTH
HY_POS_EMB = 33
HY_BANDS = (HY_POS_EMB - 1) // 2
HY_FILTER_HIDDEN = 64
HY_FAST_DECAY = 0.3
HY_SLOW_DECAY = 1.5
HY_DECAY_TARGET = 0.01
LN_EPS = 1e-6
NEG_INF = -1e30
DEEPNORM_ALPHA = (2 * DEPTH) ** 0.25

IN_SPLITS = (
    SSD_WIDTH, SSD_CONV_CH, 2 * SSD_HEADS, MLA_Q_LORA, MLA_KV_LORA, MLA_ROPE, MLA_WIDTH, GQA_WIDTH,
    GQA_KV_WIDTH, GQA_KV_WIDTH, GQA_WIDTH, 3 * HY_WIDTH, HY_WIDTH, N_BRANCH * D_MODEL,
)

LANE = 128
ROW_TILE = 256
CHUNK = SSD_CHUNK
VMEM_LIMIT_BYTES = 56 * 1024 * 1024

P_MERGE = 0
P_HY_X0 = 8192
P_HY_X1 = 9216
P_HY_V = 10240
P_HY_Z = 11264
P_SSD_Z = 12288
P_SSD_X = 13312
P_MLA_Z = 14336
P_GQA_Q = 15360
P_GQA_Z = 16384
P_MLA_CQ = 17408
P_SSD_B = 17920
P_SSD_C = 18176
P_MLA_CKV = 18432
P_GQA_K = 18688
P_GQA_V = 18816
P_SMALL = 18944
P_WIDTH = 19072
SMALL_DT = 64

F32 = jnp.float32
BF16 = jnp.bfloat16


def _cparams(*sem):
    return pltpu.CompilerParams(dimension_semantics=sem, vmem_limit_bytes=VMEM_LIMIT_BYTES)


def _split3(v):
    h1 = v.astype(BF16)
    r1 = v - h1.astype(F32)
    h2 = r1.astype(BF16)
    h3 = (r1 - h2.astype(F32)).astype(BF16)
    return h1, h2, h3


def _dot(a, b):
    return jnp.dot(a, b, preferred_element_type=F32)


def _dot_nt(a, b):
    return lax.dot_general(a, b, (((1,), (1,)), ((), ())), preferred_element_type=F32)


def _dot_tn(a, b):
    return lax.dot_general(a, b, (((0,), (0,)), ((), ())), preferred_element_type=F32)


def _dot_exact_rhs(a, b_bf16, pieces):
    parts = _split3(a)[:pieces]
    out = _dot(parts[0], b_bf16)
    for p in parts[1:]:
        out = out + _dot(p, b_bf16)
    return out


def _dot3(a, b):
    a1, a2, _ = _split3(a)
    b1, b2, _ = _split3(b)
    return _dot(a1, b1) + (_dot(a1, b2) + _dot(a2, b1))


def _silu(v):
    return v * jax.nn.sigmoid(v)


def _softplus(v):
    return jnp.maximum(v, 0.0) + jnp.log(1.0 + jnp.exp(-jnp.abs(v)))


def _rotswap(t):
    lane = lax.broadcasted_iota(jnp.int32, t.shape, 1)
    return jnp.where((lane & 63) < 32, pltpu.roll(t, 96, 1), pltpu.roll(t, 32, 1))


def _rope(t, cos, sin_signed):
    return t * cos + _rotswap(t) * sin_signed


def _mm_kernel(a_ref, b_ref, o_ref):
    o_ref[...] = _dot(a_ref[...].astype(BF16), b_ref[...].astype(BF16)).astype(o_ref.dtype)


def _matmul(a, b, layer, out_dtype, tm, tn, name):
    m, k = a.shape
    n = b.shape[2]
    tm = min(tm, m)
    tn = min(tn, n)
    return pl.pallas_call(
        _mm_kernel,
        grid=(pl.cdiv(m, tm), pl.cdiv(n, tn)),
        in_specs=[pl.BlockSpec((tm, k), lambda i, j: (i, 0)),
                  pl.BlockSpec((None, k, tn), lambda i, j: (layer, 0, j))],
        out_specs=pl.BlockSpec((tm, tn), lambda i, j: (i, j)),
        out_shape=jax.ShapeDtypeStruct((m, n), out_dtype),
        compiler_params=_cparams("parallel", "arbitrary"),
        name=name,
    )(a, b)


def _mod_row(i, tiles_per_sample, n_samples):
    return jnp.where(i % tiles_per_sample == 0, n_samples, i // tiles_per_sample)


def _ln_mod_kernel(x_ref, mod_ref, h_ref):
    x = x_ref[...]
    mu = jnp.mean(x, axis=-1, keepdims=True)
    xc = x - mu
    var = jnp.mean(xc * xc, axis=-1, keepdims=True)
    m = mod_ref[0]
    h_ref[...] = (xc * lax.rsqrt(var + LN_EPS) * (1.0 + m[1:2]) + m[0:1]).astype(h_ref.dtype)


def _ln_mod(xx, mod, tps, nb):
    tt, d = xx.shape
    return pl.pallas_call(
        _ln_mod_kernel,
        grid=(tt // ROW_TILE,),
        in_specs=[pl.BlockSpec((ROW_TILE, d), lambda i: (i, 0)),
                  pl.BlockSpec((1, 3, d), lambda i: (_mod_row(i, tps, nb), 0, 0))],
        out_specs=pl.BlockSpec((ROW_TILE, d), lambda i: (i, 0)),
        out_shape=jax.ShapeDtypeStruct((tt, d), BF16),
        compiler_params=_cparams("parallel"),
        name="ln_modulate",
    )(xx, mod)


REPACK_TILE = 512
REPACK_WINDOW = REPACK_TILE + LANE


def _repack_plan():
    off, start = [], 0
    for size in IN_SPLITS:
        off.append(start)
        start += size
    (o_sz, o_xbc, o_dt, o_cq, o_ckv, o_kpe, o_mz, o_gq, o_gk, o_gv, o_gz, o_hy, o_hz, o_mg) = off
    wide = [(P_MERGE, o_mg, N_BRANCH * D_MODEL), (P_HY_X0, o_hy, 3 * HY_WIDTH), (P_HY_Z, o_hz, HY_WIDTH),
            (P_SSD_Z, o_sz, SSD_WIDTH), (P_SSD_X, o_xbc, SSD_WIDTH), (P_MLA_Z, o_mz, MLA_WIDTH),
            (P_GQA_Q, o_gq, GQA_WIDTH), (P_GQA_Z, o_gz, GQA_WIDTH), (P_MLA_CQ, o_cq, MLA_Q_LORA)]
    n_tiles = -(-P_WIDTH // REPACK_TILE)
    src = [None] * n_tiles
    for p_off, s_off, width in wide:
        for t in range(width // REPACK_TILE):
            src[p_off // REPACK_TILE + t] = s_off + t * REPACK_TILE
    last_merge = (P_MERGE + N_BRANCH * D_MODEL) // REPACK_TILE - 1
    tail_tiles = [last_merge] + [j for j in range(n_tiles) if src[j] is None]
    tail_src = [(src[last_merge], REPACK_TILE), (o_xbc + SSD_WIDTH, 2 * SSD_GROUPS * SSD_STATE),
                (o_ckv, MLA_KV_LORA), (o_gk, 2 * GQA_KV_WIDTH), (o_kpe, MLA_ROPE), (o_dt, 2 * SSD_HEADS)]
    src[last_merge] = None
    return src, tail_tiles, tail_src


def _repack_kernel(start_ref, shift_ref, tail_ref_idx, win_ref, tail_ref, o_ref):
    j = pl.program_id(1)
    from_tail = tail_ref_idx[j] >= 0

    @pl.when(jnp.logical_not(from_tail))
    def _():
        shifted = pltpu.roll(win_ref[0], REPACK_WINDOW - shift_ref[j], 1)
        o_ref[...] = shifted[:, :REPACK_TILE].astype(o_ref.dtype)

    @pl.when(from_tail)
    def _():
        o_ref[...] = tail_ref[...]


def _repack_w_in(w_in):
    nl, d, _ = w_in.shape
    src, tail_tiles, tail_src = _repack_plan()
    tail = jnp.concatenate([w_in[:, :, s:s + w] for s, w in tail_src], axis=2).astype(BF16)
    tail = jnp.pad(tail, ((0, 0), (0, 0), (0, len(tail_tiles) * REPACK_TILE - tail.shape[2])))
    starts = [0 if s is None else s // LANE for s in src]
    shifts = [0 if s is None else s % LANE for s in src]
    tidx = [tail_tiles.index(j) if s is None else -1 for j, s in enumerate(src)]
    as_i32 = lambda v: jnp.asarray(v, jnp.int32)
    gs = pltpu.PrefetchScalarGridSpec(
        num_scalar_prefetch=3, grid=(nl, len(src)),
        in_specs=[pl.BlockSpec((pl.Element(1), pl.Element(d), pl.Element(REPACK_WINDOW)),
                               lambda l, j, st, sh, ti: (l, 0, st[j] * LANE)),
                  pl.BlockSpec((None, d, REPACK_TILE), lambda l, j, st, sh, ti: (l, 0, jnp.maximum(ti[j], 0)))],
        out_specs=pl.BlockSpec((None, d, REPACK_TILE), lambda l, j, st, sh, ti: (l, 0, j)))
    return pl.pallas_call(
        _repack_kernel, grid_spec=gs,
        out_shape=jax.ShapeDtypeStruct((nl, d, P_WIDTH), BF16),
        compiler_params=_cparams("parallel", "arbitrary"),
        name="repack_w_in",
    )(as_i32(starts), as_i32(shifts), as_i32(tidx), w_in, tail)


def _conv_chunk(src_ref, w, bias, c, n_ctx_chunks, n_chunks):
    tb = n_chunks * CHUNK
    r0 = pl.multiple_of(c * CHUNK, CHUNK)
    cur = src_ref[pl.ds(r0, CHUNK), :].astype(F32)
    rp = pl.multiple_of(jnp.maximum(r0 - 16, 0), 16)
    rn = pl.multiple_of(jnp.minimum(r0 + CHUNK, tb - 16), 16)
    prev_row = src_ref[pl.ds(rp, 16), :].astype(F32)[15:16]
    next_row = src_ref[pl.ds(rn, 16), :].astype(F32)[0:1]
    has_prev = jnp.logical_and(c != 0, c != n_ctx_chunks)
    has_next = jnp.logical_and(c != n_ctx_chunks - 1, c != n_chunks - 1)
    prev_row = jnp.where(has_prev, prev_row, 0.0)
    next_row = jnp.where(has_next, next_row, 0.0)
    row = lax.broadcasted_iota(jnp.int32, cur.shape, 0)
    sp = jnp.where(row == 0, prev_row, pltpu.roll(cur, 1, 0))
    sn = jnp.where(row == CHUNK - 1, next_row, pltpu.roll(cur, CHUNK - 1, 0))
    return sp * w[0:1] + cur * w[1:2] + sn * w[2:3] + bias


def _ssd_kernel(x_ref, b_ref, c_ref, sm_ref, z_ref, cwx_ref, cbx_ref, cwb_ref, cbb_ref, cwc_ref, cbc_ref,
                dtb_ref, alog_ref, dsk_ref, nw_ref, o_ref, ux, ub, uc, yacc, ybwd, st, st_b,
                *, n_ctx_chunks, n_chunks):
    g = pl.program_id(1)
    hp = ux.shape[1]
    nh = hp // SSD_HEAD_DIM

    def conv_body(c, carry):
        r0 = pl.multiple_of(c * CHUNK, CHUNK)
        ux[pl.ds(r0, CHUNK), :] = _silu(_conv_chunk(x_ref, cwx_ref[...], cbx_ref[...], c, n_ctx_chunks,
                                                    n_chunks)).astype(ux.dtype)
        ub[pl.ds(r0, CHUNK), :] = _silu(_conv_chunk(b_ref, cwb_ref[...], cbb_ref[...], c, n_ctx_chunks,
                                                    n_chunks)).astype(ub.dtype)
        uc[pl.ds(r0, CHUNK), :] = _silu(_conv_chunk(c_ref, cwc_ref[...], cbc_ref[...], c, n_ctx_chunks,
                                                    n_chunks)).astype(uc.dtype)
        return carry

    lax.fori_loop(0, n_chunks, conv_body, 0)

    ri = lax.broadcasted_iota(jnp.int32, (CHUNK, CHUNK), 0)
    ci = lax.broadcasted_iota(jnp.int32, (CHUNK, CHUNK), 1)
    lane_hp = lax.broadcasted_iota(jnp.int32, (CHUNK, hp), 1)
    lane128 = lax.broadcasted_iota(jnp.int32, (CHUNK, LANE), 1)
    a_all = -jnp.exp(alog_ref[0])
    dtb = dtb_ref[0]

    def direction_consts(dirn):
        tri = (ci <= ri) if dirn == 0 else (ci >= ri)
        tri_bf = jnp.where(tri, 1.0, 0.0).astype(BF16)
        expand = jnp.where(lax.broadcasted_iota(jnp.int32, (LANE, hp), 0)
                           == dirn * 16 + (lax.broadcasted_iota(jnp.int32, (LANE, hp), 1) >> 6),
                           1.0, 0.0).astype(BF16)
        return tri, tri_bf, expand

    consts = (direction_consts(0), direction_consts(1))

    def chunk_step(dirn, c, st):
        tri, tri_bf, expand = consts[dirn]
        last = CHUNK - 1 if dirn == 0 else 0
        if True:
            r0 = pl.multiple_of(c * CHUNK, CHUNK)
            xs = ux[pl.ds(r0, CHUNK), :].astype(F32)
            bc = ub[pl.ds(r0, CHUNK), :]
            cc = uc[pl.ds(r0, CHUNK), :]
            sm = sm_ref[pl.ds(r0, CHUNK), :].astype(F32)
            sm = jnp.where(g == 0, pltpu.roll(sm, LANE - SMALL_DT, 1), pltpu.roll(sm, LANE - SMALL_DT - nh, 1))
            dt = _softplus(sm + dtb)
            la = dt * a_all
            cum = _dot_exact_rhs_lhs(tri_bf, la)
            cx = _dot_exact_rhs(cum, expand, 2)
            dtx = _dot_exact_rhs(dt, expand, 2)
            cx_last = cx[last:last + 1]
            ecum = jnp.exp(cx)
            dec_end = jnp.exp(cx_last - cx)
            xd = xs * dtx
            xd_bf = xd.astype(BF16)
            gmat = _dot_nt(cc, bc)
            cum_t = cum.T
            cols = []
            for j in range(nh // 2):
                sc = []
                for h in (2 * j, 2 * j + 1):
                    k = dirn * 16 + h
                    diff = cum[:, k:k + 1] - cum_t[k:k + 1, :]
                    lmat = jnp.exp(jnp.where(tri, diff, NEG_INF))
                    sc.append((gmat * lmat).astype(BF16))
                xcol = xd_bf[:, j * LANE:(j + 1) * LANE]
                lo = lane128 < SSD_HEAD_DIM
                rhs = jnp.concatenate([jnp.where(lo, xcol, jnp.zeros_like(xcol)),
                                       jnp.where(lo, jnp.zeros_like(xcol), xcol)], axis=0)
                cols.append(_dot(jnp.concatenate(sc, axis=1), rhs))
            y = jnp.concatenate(cols, axis=1)
            s_prev = st[...]
            y = y + _dot(cc, s_prev.astype(BF16)) * ecum
            st[...] = jnp.exp(cx_last) * s_prev + _dot_tn(bc, (xd * dec_end).astype(BF16))
            return y, xs, r0

    st[...] = jnp.zeros_like(st)
    st_b[...] = jnp.zeros_like(st_b)

    def scan_body(i, carry):
        y_f, xs_f, r0_f = chunk_step(0, i, st)
        yacc[pl.ds(r0_f, CHUNK), :] = y_f + xs_f * dsk_ref[...]
        c_b = jnp.where(i < n_ctx_chunks, n_ctx_chunks - 1 - i, n_chunks - 1 + n_ctx_chunks - i)
        y_b, _, r0_b = chunk_step(1, c_b, st_b)
        ybwd[pl.ds(r0_b, CHUNK), :] = y_b
        return carry

    lax.fori_loop(0, n_chunks, scan_body, 0)

    def gate_body(c, carry):
        r0 = pl.multiple_of(c * CHUNK, CHUNK)
        tot = (yacc[pl.ds(r0, CHUNK), :] + ybwd[pl.ds(r0, CHUNK), :]) * _silu(z_ref[pl.ds(r0, CHUNK), :].astype(F32))
        ms = jnp.mean(tot * tot, axis=-1, keepdims=True)
        o_ref[pl.ds(r0, CHUNK), :] = (tot * lax.rsqrt(ms + LN_EPS) * nw_ref[...]).astype(o_ref.dtype)
        return carry

    lax.fori_loop(0, n_chunks, gate_body, 0)


def _dot_exact_rhs_lhs(tri_bf, v):
    v1, v2, v3 = _split3(v)
    return _dot(tri_bf, v1) + (_dot(tri_bf, v2) + _dot(tri_bf, v3))


def _ssd(p3, conv_w, conv_b, dt_bias, a_log, d_skip, norm_w, n_ctx):
    nb, tb, _ = p3.shape
    hp = SSD_WIDTH // SSD_GROUPS
    n_chunks = tb // CHUNK
    gn = SSD_GROUPS * SSD_STATE
    cwx, cwb, cwc = conv_w[:, :SSD_WIDTH], conv_w[:, SSD_WIDTH:SSD_WIDTH + gn], conv_w[:, SSD_WIDTH + gn:]
    cb = conv_b[None, :]
    cbx, cbb, cbc = cb[:, :SSD_WIDTH], cb[:, SSD_WIDTH:SSD_WIDTH + gn], cb[:, SSD_WIDTH + gn:]
    hg = SSD_HEADS // SSD_GROUPS

    def per_group(v):
        v = v.reshape(2, SSD_GROUPS, hg).transpose(1, 0, 2)
        v = jnp.pad(v, ((0, 0), (0, 0), (0, 16 - hg))).reshape(SSD_GROUPS, 1, 32)
        return jnp.pad(v, ((0, 0), (0, 0), (0, LANE - 32)))

    dsk = jnp.repeat(d_skip, SSD_HEAD_DIM)[None, :]
    kern = functools.partial(_ssd_kernel, n_ctx_chunks=n_ctx // CHUNK, n_chunks=n_chunks)
    col = lambda off, w: (lambda b, g: (b, 0, off // w + g))
    par = lambda b, g: (0, g)
    return pl.pallas_call(
        kern,
        grid=(nb, SSD_GROUPS),
        in_specs=[pl.BlockSpec((None, tb, hp), col(P_SSD_X, hp)),
                  pl.BlockSpec((None, tb, SSD_STATE), col(P_SSD_B, SSD_STATE)),
                  pl.BlockSpec((None, tb, SSD_STATE), col(P_SSD_C, SSD_STATE)),
                  pl.BlockSpec((None, tb, LANE), lambda b, g: (b, 0, P_SMALL // LANE)),
                  pl.BlockSpec((None, tb, hp), col(P_SSD_Z, hp)),
                  pl.BlockSpec((3, hp), par), pl.BlockSpec((1, hp), par),
                  pl.BlockSpec((3, SSD_STATE), par), pl.BlockSpec((1, SSD_STATE), par),
                  pl.BlockSpec((3, SSD_STATE), par), pl.BlockSpec((1, SSD_STATE), par),
                  pl.BlockSpec((1, 1, LANE), lambda b, g: (g, 0, 0)),
                  pl.BlockSpec((1, 1, LANE), lambda b, g: (g, 0, 0)),
                  pl.BlockSpec((1, hp), par), pl.BlockSpec((1, hp), par)],
        out_specs=pl.BlockSpec((None, tb, hp), lambda b, g: (b, 0, g)),
        out_shape=jax.ShapeDtypeStruct((nb, tb, SSD_WIDTH), BF16),
        scratch_shapes=[pltpu.VMEM((tb, hp), BF16), pltpu.VMEM((tb, SSD_STATE), BF16),
                        pltpu.VMEM((tb, SSD_STATE), BF16), pltpu.VMEM((tb, hp), F32),
                        pltpu.VMEM((tb, hp), F32), pltpu.VMEM((SSD_STATE, hp), F32),
                        pltpu.VMEM((SSD_STATE, hp), F32)],
        compiler_params=_cparams("parallel", "parallel"),
        name="ssd_bidir",
    )(p3, p3, p3, p3, p3, cwx, cbx, cwb, cbb, cwc, cbc, per_group(dt_bias), per_group(a_log), dsk,
      norm_w[None, :])


def _rope_tables(n_ctx, n_lat):
    rows = n_lat // GRID_W
    row = jnp.repeat(jnp.arange(rows, dtype=F32), GRID_W)
    colp = jnp.tile(jnp.arange(GRID_W, dtype=F32), rows)
    n_freq = ROPE_DIM // 4
    inv_freq = ROPE_BASE ** (-jnp.arange(n_freq, dtype=F32) / n_freq)
    ang = jnp.concatenate([row[:, None] * inv_freq[None, :], colp[:, None] * inv_freq[None, :]], axis=-1)
    cos, sin = jnp.cos(ang), jnp.sin(ang)
    cos2 = jnp.tile(cos, (1, 4))
    sin2 = jnp.tile(jnp.concatenate([-sin, sin], axis=-1), (1, 2))
    cos2 = jnp.concatenate([jnp.ones((n_ctx, LANE), F32), cos2], axis=0)
    sin2 = jnp.concatenate([jnp.zeros((n_ctx, LANE), F32), sin2], axis=0)
    return cos2, sin2


def _attn_prep_kernel(cq_ref, ckv_ref, sm_ref, gk_ref, gv_ref, cos_ref, sin_ref, qn_ref, wq_ref, kn_ref,
                      wk_ref, wv_ref, q_ref, k_ref, v_ref, kg_ref, vg_ref):
    cos, sin = cos_ref[...], sin_ref[...]
    lane = lax.broadcasted_iota(jnp.int32, cos.shape, 1)
    scale = (MLA_NOPE + MLA_ROPE) ** -0.5 * math.log2(math.e)
    cq = cq_ref[...].astype(F32)
    cqn = cq * lax.rsqrt(jnp.mean(cq * cq, axis=-1, keepdims=True) + LN_EPS) * qn_ref[...]
    q = _dot(cqn.astype(BF16), wq_ref[...])
    ckv = ckv_ref[...].astype(F32)
    ckvn = (ckv * lax.rsqrt(jnp.mean(ckv * ckv, axis=-1, keepdims=True) + LN_EPS) * kn_ref[...]).astype(BF16)
    kn = _dot(ckvn, wk_ref[...])
    v_ref[...] = _dot(ckvn, wv_ref[...]).astype(v_ref.dtype)
    kpe = jnp.where(lane < MLA_ROPE, _rope(sm_ref[...].astype(F32), cos, sin), 0.0).astype(k_ref.dtype)
    for h in range(MLA_HEADS):
        o = 2 * h * LANE
        q_ref[:, o:o + LANE] = (q[:, o:o + LANE] * scale).astype(q_ref.dtype)
        q_ref[:, o + LANE:o + 2 * LANE] = (_rope(q[:, o + LANE:o + 2 * LANE], cos, sin) * scale).astype(q_ref.dtype)
        k_ref[:, o:o + LANE] = kn[:, h * LANE:(h + 1) * LANE].astype(k_ref.dtype)
        k_ref[:, o + LANE:o + 2 * LANE] = kpe
    kg_ref[...] = _rope(gk_ref[...].astype(F32), cos, sin).astype(kg_ref.dtype)
    vg_ref[...] = gv_ref[...].astype(F32).T.astype(vg_ref.dtype)


def _attn_prep(p, cos2, sin2, q_norm, w_uq, kv_norm, w_ukv, tps):
    tt = p.shape[0]
    hq = MLA_NOPE + MLA_ROPE
    wq = w_uq.reshape(MLA_Q_LORA, MLA_HEADS, hq)
    wq = jnp.pad(wq, ((0, 0), (0, 0), (0, 2 * LANE - hq))).reshape(MLA_Q_LORA, MLA_HEADS * 2 * LANE).astype(BF16)
    wkv = w_ukv.reshape(MLA_KV_LORA, MLA_HEADS, MLA_NOPE + MLA_V)
    wk = wkv[:, :, :MLA_NOPE].reshape(MLA_KV_LORA, MLA_HEADS * MLA_NOPE).astype(BF16)
    wv = wkv[:, :, MLA_NOPE:].reshape(MLA_KV_LORA, MLA_HEADS * MLA_V).astype(BF16)
    full = lambda shape: pl.BlockSpec(shape, lambda i: (0, 0))
    pcol = lambda off, w: pl.BlockSpec((ROW_TILE, w), lambda i: (i, off // w))
    rows = lambda w: pl.BlockSpec((ROW_TILE, w), lambda i: (i, 0))
    tab = pl.BlockSpec((ROW_TILE, LANE), lambda i: (i % tps, 0))
    return pl.pallas_call(
        _attn_prep_kernel,
        grid=(tt // ROW_TILE,),
        in_specs=[pcol(P_MLA_CQ, MLA_Q_LORA), pcol(P_MLA_CKV, MLA_KV_LORA), pcol(P_SMALL, LANE),
                  pcol(P_GQA_K, LANE), pcol(P_GQA_V, LANE), tab, tab,
                  full((1, MLA_Q_LORA)), full(wq.shape), full((1, MLA_KV_LORA)), full(wk.shape), full(wv.shape)],
        out_specs=[rows(MLA_HEADS * 2 * LANE), rows(MLA_HEADS * 2 * LANE), rows(MLA_WIDTH),
                   rows(LANE), pl.BlockSpec((LANE, ROW_TILE), lambda i: (0, i))],
        out_shape=[jax.ShapeDtypeStruct((tt, MLA_HEADS * 2 * LANE), BF16),
                   jax.ShapeDtypeStruct((tt, MLA_HEADS * 2 * LANE), BF16),
                   jax.ShapeDtypeStruct((tt, MLA_WIDTH), BF16),
                   jax.ShapeDtypeStruct((tt, LANE), BF16),
                   jax.ShapeDtypeStruct((LANE, tt), BF16)],
        compiler_params=_cparams("parallel"),
        name="attn_prep",
    )(p, p, p, p, p, cos2, sin2, q_norm[None, :], wq, kv_norm[None, :], wk, wv)


MLA_Q_CHUNK = 256


def _mla_kernel(q_ref, k_ref, v_ref, z_ref, o_ref, *, n_ctx):
    tb = k_ref.shape[0]

    def attend(r0, rows, n_keys):
        s = _dot_nt(q_ref[r0:r0 + rows, :], k_ref[0:n_keys, :])
        m = jnp.max(s, axis=-1, keepdims=True)
        p = jnp.exp2(s - m)
        l = jnp.sum(p, axis=-1, keepdims=True)
        o = _dot(p.astype(BF16), v_ref[0:n_keys, :]) / l
        o_ref[r0:r0 + rows, :] = (o * _silu(z_ref[r0:r0 + rows, :].astype(F32))).astype(o_ref.dtype)

    attend(0, n_ctx, n_ctx)
    for r0 in range(n_ctx, tb, MLA_Q_CHUNK):
        attend(r0, MLA_Q_CHUNK, tb)


def _mla(qm3, km3, vm3, p3, n_ctx):
    nb, tb, _ = p3.shape
    assert (tb - n_ctx) % MLA_Q_CHUNK == 0
    blk = lambda w, off: pl.BlockSpec((None, tb, w), lambda b, h: (b, 0, off // w + h))
    return pl.pallas_call(
        functools.partial(_mla_kernel, n_ctx=n_ctx),
        grid=(nb, MLA_HEADS),
        in_specs=[blk(2 * LANE, 0), blk(2 * LANE, 0), blk(LANE, 0), blk(LANE, P_MLA_Z)],
        out_specs=blk(LANE, 0),
        out_shape=jax.ShapeDtypeStruct((nb, tb, MLA_WIDTH), BF16),
        compiler_params=_cparams("parallel", "parallel"),
        name="mla_attention",
    )(qm3, km3, vm3, p3)


GQA_HEAD_BATCH = 8

def _gqa_kernel(sink_ref, q_ref, kp_ref, ko_ref, kn_ref, kc_ref, vp_ref, vo_ref, vn_ref, vc_ref, cos_ref,
                sin_ref, z_ref, o_ref, *, n_ctx_blocks):
    blk = pl.program_id(1)
    n_blk = pl.num_programs(1)
    is_lat = blk >= n_ctx_blocks
    has_prev = blk > n_ctx_blocks
    has_next = jnp.logical_and(is_lat, blk < n_blk - 1)
    cos, sin = cos_ref[...], sin_ref[...]
    log2e = math.log2(math.e)
    scale = GQA_HEAD_DIM ** -0.5 * log2e
    w = q_ref.shape[0]
    hd = GQA_HEAD_DIM
    hpk = GQA_HEADS // GQA_KV_HEADS
    q = jnp.concatenate([_rope(q_ref[:, c * LANE:(c + 1) * LANE].astype(F32), cos, sin) * scale
                         for c in range(GQA_WIDTH // LANE)], axis=1)
    qt = q.T.astype(BF16)
    kall = jnp.concatenate([kp_ref[...], ko_ref[...], kn_ref[...], kc_ref[...]], axis=0)
    vt = jnp.concatenate([vp_ref[...], vo_ref[...], vn_ref[...], vc_ref[...]], axis=1)
    kj = lax.broadcasted_iota(jnp.int32, (w, w), 0)
    qi = lax.broadcasted_iota(jnp.int32, (w, w), 1)
    m_prev = jnp.logical_and(kj >= qi, has_prev)
    m_next = jnp.logical_and(kj <= qi, has_next)
    zeros = jnp.zeros((hd, w), BF16)
    nbat = GQA_HEAD_BATCH
    tile = lambda a: jnp.concatenate([a] * nbat, axis=1)
    m_prev, m_next = tile(m_prev), tile(m_next)
    outs = []
    for h0 in range(0, GQA_HEADS, nbat):
        kh = h0 // hpk
        heads = range(h0, h0 + nbat)
        rhs = jnp.concatenate(
            [jnp.concatenate([qt[h * hd:(h + 1) * hd], zeros] if kh == 0 else [zeros, qt[h * hd:(h + 1) * hd]],
                             axis=0) for h in heads], axis=1)
        st = _dot(kall, rhs)
        s_p = jnp.where(m_prev, st[0:w], NEG_INF)
        s_o = jnp.where(is_lat, st[w:2 * w], NEG_INF)
        s_n = jnp.where(m_next, st[2 * w:3 * w], NEG_INF)
        s = jnp.concatenate([s_p, s_o, s_n, st[3 * w:]], axis=0)
        sink = jnp.concatenate([jnp.full((1, w), sink_ref[h] * log2e, F32) for h in heads], axis=1)
        m = jnp.maximum(jnp.max(s, axis=0, keepdims=True), sink)
        pr = jnp.exp2(s - m)
        l = jnp.sum(pr, axis=0, keepdims=True) + jnp.exp2(sink - m)
        ot = _dot(vt[kh * hd:(kh + 1) * hd], pr.astype(BF16)) / l
        outs.extend(ot[:, i * w:(i + 1) * w] for i in range(nbat))
    y = jnp.concatenate(outs, axis=0).T
    o_ref[...] = (y * _silu(z_ref[...].astype(F32))).astype(o_ref.dtype)


def _gqa(p, kg, vgt, cos2, sin2, sink, nb, tb, n_ctx):
    w = WINDOW
    bps = tb // w
    row = lambda b, i: b * bps + i
    prev = lambda b, i: b * bps + jnp.maximum(i - 1, 0)
    nxt = lambda b, i: b * bps + jnp.minimum(i + 1, bps - 1)
    kspec = lambda f: pl.BlockSpec((w, LANE), lambda b, i: (f(b, i), 0))
    vspec = lambda f: pl.BlockSpec((LANE, w), lambda b, i: (0, f(b, i)))
    kctx = pl.BlockSpec((n_ctx, LANE), lambda b, i: (b * (tb // n_ctx), 0))
    vctx = pl.BlockSpec((LANE, n_ctx), lambda b, i: (0, b * (tb // n_ctx)))
    tab = pl.BlockSpec((w, LANE), lambda b, i: (i, 0))
    return pl.pallas_call(
        functools.partial(_gqa_kernel, n_ctx_blocks=n_ctx // w),
        grid=(nb, bps),
        in_specs=[pl.BlockSpec(memory_space=pltpu.SMEM),
                  pl.BlockSpec((w, GQA_WIDTH), lambda b, i: (row(b, i), P_GQA_Q // GQA_WIDTH)),
                  kspec(prev), kspec(row), kspec(nxt), kctx, vspec(prev), vspec(row), vspec(nxt), vctx, tab, tab,
                  pl.BlockSpec((w, GQA_WIDTH), lambda b, i: (row(b, i), P_GQA_Z // GQA_WIDTH))],
        out_specs=pl.BlockSpec((w, GQA_WIDTH), lambda b, i: (row(b, i), 0)),
        out_shape=jax.ShapeDtypeStruct((nb * tb, GQA_WIDTH), BF16),
        compiler_params=_cparams("parallel", "arbitrary"),
        name="gqa_window_attention",
    )(sink, p, kg, kg, kg, kg, vgt, vgt, vgt, vgt, cos2, sin2, p)


def _dft_tables(n):
    m = jnp.arange(n, dtype=jnp.int32)[None, :]

    def small(kvals):
        ang = ((kvals[:, None] * m) % (2 * n)).astype(F32) * (math.pi / n)
        return jnp.cos(ang), jnp.sin(ang)

    ca, sa = small(64 * jnp.arange(n // 64, dtype=jnp.int32))
    cb, sb = small(jnp.arange(64, dtype=jnp.int32))
    cos_t = ca[:, None, :] * cb[None, :, :] - sa[:, None, :] * sb[None, :, :]
    sin_t = sa[:, None, :] * cb[None, :, :] + ca[:, None, :] * sb[None, :, :]
    return cos_t.reshape(n, n).astype(BF16), sin_t.reshape(n, n).astype(BF16)


def _hyena_feats(n):
    t = jnp.linspace(0.0, 1.0, n, dtype=F32)[:, None]
    w_ang = (2.0 * math.pi / n) * jnp.arange(n, dtype=F32)[:, None]
    bands = jnp.linspace(1e-4, HY_BANDS - 1, HY_BANDS, dtype=F32)[None, :]
    feats = jnp.concatenate([t, jnp.cos(bands * w_ang), -jnp.sin(bands * w_ang)], axis=-1)
    return jnp.pad(feats, ((0, 0), (0, LANE - HY_POS_EMB)))


def _hyena_filter_kernel(feats_ref, w1_ref, b1_ref, w2_ref, b2_ref, w3f_ref, w3b_ref, freq_ref, dl_ref,
                         cos_ref, sin_ref, kr_ref, ks_ref, kq_ref, hs, hd, *, n):
    kt = pl.program_id(1)
    tk = cos_ref.shape[0]

    @pl.when(kt == 0)
    def _():
        feats = feats_ref[...]
        t = feats[:, 0:1]
        hdn = jnp.sin(freq_ref[0:1, :] * (_dot3(feats, w1_ref[...]) + b1_ref[...]))
        hdn = jnp.sin(freq_ref[1:2, :] * (_dot3(hdn, w2_ref[...]) + b2_ref[...]))
        decay = jnp.exp(-t * dl_ref[...])
        hf = _dot3(hdn, w3f_ref[...]) * decay
        hb = _dot3(hdn, w3b_ref[...]) * decay
        row = lax.broadcasted_iota(jnp.int32, hb.shape, 0)
        hb = jnp.where(row == 0, 0.0, hb)
        hsum = hf + hb
        hs[...] = hsum.astype(hs.dtype)
        hd[...] = (hf - hb).astype(hd.dtype)
        alt = jnp.where((row & 1) == 0, 1.0, -1.0)
        kq_ref[...] = jnp.sum(hsum * alt, axis=0, keepdims=True) * (1.0 / (2 * n))

    kidx = kt * tk + lax.broadcasted_iota(jnp.int32, (tk, 1), 0)
    wk = jnp.where(kidx == 0, 0.5 / n, 1.0 / n)
    kr_ref[...] = _dot(cos_ref[...], hs[...]) * wk
    ks_ref[...] = _dot(sin_ref[...], hd[...]) * wk


def _hyena_filter(n, cos_t, sin_t, w1, b1, w2, b2, w3, freq):
    cbw = 512
    tk = min(256, n)
    deltas = jnp.abs(jnp.linspace(math.log(HY_DECAY_TARGET) / HY_FAST_DECAY,
                                  math.log(HY_DECAY_TARGET) / HY_SLOW_DECAY, HY_WIDTH, dtype=F32))[None, :]
    w1p = jnp.pad(w1, ((0, LANE - HY_POS_EMB), (0, 0)))
    full = lambda a: pl.BlockSpec(a.shape, lambda c, k: (0,) * a.ndim)
    nblk = HY_WIDTH // cbw
    kern = functools.partial(_hyena_filter_kernel, n=n)
    feats = _hyena_feats(n)
    return pl.pallas_call(
        kern,
        grid=(nblk, n // tk),
        in_specs=[full(feats), full(w1p), pl.BlockSpec((1, HY_FILTER_HIDDEN), lambda c, k: (0, 0)),
                  full(w2), pl.BlockSpec((1, HY_FILTER_HIDDEN), lambda c, k: (0, 0)),
                  pl.BlockSpec((HY_FILTER_HIDDEN, cbw), lambda c, k: (0, c)),
                  pl.BlockSpec((HY_FILTER_HIDDEN, cbw), lambda c, k: (0, nblk + c)),
                  full(freq), pl.BlockSpec((1, cbw), lambda c, k: (0, c)),
                  pl.BlockSpec((tk, n), lambda c, k: (k, 0)), pl.BlockSpec((tk, n), lambda c, k: (k, 0))],
        out_specs=[pl.BlockSpec((tk, cbw), lambda c, k: (k, c)), pl.BlockSpec((tk, cbw), lambda c, k: (k, c)),
                   pl.BlockSpec((1, cbw), lambda c, k: (0, c))],
        out_shape=[jax.ShapeDtypeStruct((n, HY_WIDTH), F32), jax.ShapeDtypeStruct((n, HY_WIDTH), F32),
                   jax.ShapeDtypeStruct((1, HY_WIDTH), F32)],
        scratch_shapes=[pltpu.VMEM((n, cbw), BF16), pltpu.VMEM((n, cbw), BF16)],
        compiler_params=_cparams("parallel", "arbitrary"),
        name="hyena_filter",
    )(feats, w1p, b1[None, :], w2, b2[None, :], w3, w3, freq, deltas, cos_t, sin_t)


def _spectral_product(a, bq, kr, ks):
    return (a * kr - bq * ks).astype(BF16), (a * ks + bq * kr).astype(BF16)


HY_TILE = 512


def _hyena_kernel(x0_ref, x1_ref, v_ref, z_ref, cw0_ref, cb0_ref, cw1_ref, cb1_ref, cwv_ref, cbv_ref, d_ref,
                  cr_ref, sr_ref, kr_ref, ks_ref, kq_ref, ccx_ref, scx_ref, krx_ref, ksx_ref, kqx_ref,
                  o_ref, u, yr_s, ys_s, unyq, *, n_ctx, n_chunks, with_ctx):
    kt = pl.program_id(2)
    nk = pl.num_programs(2) // 2
    n_ctx_chunks = n_ctx // CHUNK
    tb = n_chunks * CHUNK
    tk = cr_ref.shape[0]

    def alt_sign(shape):
        return jnp.where((lax.broadcasted_iota(jnp.int32, shape, 0) & 1) == 0, 1.0, -1.0)

    def gate_chunk(c):
        x0c = _conv_chunk(x0_ref, cw0_ref[...], cb0_ref[...], c, n_ctx_chunks, n_chunks)
        r0 = pl.multiple_of(c * CHUNK, CHUNK)
        return x0c * _silu(z_ref[pl.ds(r0, CHUNK), :].astype(F32))

    @pl.when(kt == 0)
    def _():
        def conv_body(c, carry):
            r0 = pl.multiple_of(c * CHUNK, CHUNK)
            x1c = _conv_chunk(x1_ref, cw1_ref[...], cb1_ref[...], c, n_ctx_chunks, n_chunks)
            vc = _conv_chunk(v_ref, cwv_ref[...], cbv_ref[...], c, n_ctx_chunks, n_chunks)
            u[pl.ds(r0, CHUNK), :] = (x1c * vc).astype(u.dtype)
            return carry

        lax.fori_loop(0, n_chunks, conv_body, 0)
        ulf = u[n_ctx:tb, :].astype(F32)
        unyq[...] = jnp.sum(ulf * alt_sign(ulf.shape), axis=0, keepdims=True) * kq_ref[...]

    @pl.when(kt < nk)
    def _():
        ul = u[n_ctx:tb, :]
        yr, ys = _spectral_product(_dot(cr_ref[...], ul), _dot(sr_ref[...], ul), kr_ref[...], ks_ref[...])
        f0 = pl.multiple_of(kt * tk, tk)
        yr_s[pl.ds(f0, tk), :] = yr
        ys_s[pl.ds(f0, tk), :] = ys

    @pl.when(kt >= nk)
    def _():
        t = kt - nk
        y = _dot(cr_ref[...], yr_s[...]) + _dot(sr_ref[...], ys_s[...])
        for j in range(tk // CHUNK):
            c = n_ctx_chunks + t * (tk // CHUNK) + j
            r0 = pl.multiple_of(c * CHUNK, CHUNK)
            uf = u[pl.ds(r0, CHUNK), :].astype(F32)
            yj = y[j * CHUNK:(j + 1) * CHUNK] + alt_sign(uf.shape) * unyq[...] + uf * d_ref[...]
            o_ref[pl.ds(r0, CHUNK), :] = (yj * gate_chunk(c)).astype(o_ref.dtype)

    @pl.when(kt == 2 * nk - 1)
    def _():
        if with_ctx:
            uc = u[0:n_ctx, :]
            ucf = uc.astype(F32)
            alt = alt_sign(ucf.shape)
            yr, ys = _spectral_product(_dot(ccx_ref[...], uc), _dot(scx_ref[...], uc), krx_ref[...], ksx_ref[...])
            y = _dot(ccx_ref[...], yr) + _dot(scx_ref[...], ys)
            y = y + alt * (jnp.sum(ucf * alt, axis=0, keepdims=True) * kqx_ref[...]) + ucf * d_ref[...]
            for c in range(n_ctx_chunks):
                o_ref[c * CHUNK:(c + 1) * CHUNK, :] = (y[c * CHUNK:(c + 1) * CHUNK] * gate_chunk(c)).astype(o_ref.dtype)
        else:
            o_ref[0:n_ctx, :] = jnp.zeros((n_ctx, o_ref.shape[1]), o_ref.dtype)


def _hyena(p3, conv_w, conv_b, d_skip, tabs, filt, tabs_c, filt_c, n_ctx, with_ctx):
    nb, tb, _ = p3.shape
    n_lat = tb - n_ctx
    cbw = 512
    tk = HY_TILE
    nk = n_lat // tk
    cos_t, sin_t = tabs
    kr, ks, kq = filt
    cos_c, sin_c = tabs_c
    krc, ksc, kqc = filt_c
    cb = conv_b[None, :]
    one = pl.Buffered(1)
    pcol = lambda off: pl.BlockSpec((None, tb, cbw), lambda b, c, k: (b, 0, off // cbw + c), pipeline_mode=one)
    par = lambda rows, sec: pl.BlockSpec((rows, cbw), lambda b, c, k: (0, sec * (HY_WIDTH // cbw) + c))
    kern = functools.partial(_hyena_kernel, n_ctx=n_ctx, n_chunks=tb // CHUNK, with_ctx=with_ctx)
    small = lambda a: pl.BlockSpec(a.shape, lambda b, c, k: (0, 0))
    chan = lambda rows: pl.BlockSpec((rows, cbw), lambda b, c, k: (0, c))
    table = pl.BlockSpec((tk, n_lat), lambda b, c, k: (k % nk, 0))
    spec = pl.BlockSpec((tk, cbw), lambda b, c, k: (jnp.minimum(k, nk - 1), c))
    return pl.pallas_call(
        kern,
        grid=(nb, HY_WIDTH // cbw, 2 * nk),
        in_specs=[pcol(P_HY_X0), pcol(P_HY_X1), pcol(P_HY_V), pcol(P_HY_Z),
                  par(3, 0), par(1, 0), par(3, 1), par(1, 1), par(3, 2), par(1, 2), chan(1),
                  table, table, spec, spec, chan(1), small(cos_c), small(sin_c), chan(n_ctx), chan(n_ctx), chan(1)],
        out_specs=pl.BlockSpec((None, tb, cbw), lambda b, c, k: (b, 0, c)),
        out_shape=jax.ShapeDtypeStruct((nb, tb, HY_WIDTH), BF16),
        scratch_shapes=[pltpu.VMEM((tb, cbw), BF16), pltpu.VMEM((n_lat, cbw), BF16), pltpu.VMEM((n_lat, cbw), BF16),
                        pltpu.VMEM((1, cbw), F32)],
        compiler_params=_cparams("parallel", "parallel", "arbitrary"),
        name="hyena_dft_conv",
    )(p3, p3, p3, p3, conv_w, cb, conv_w, cb, conv_w, cb, d_skip[None, :], cos_t, sin_t,
      kr, ks, kq, cos_c, sin_c, krc, ksc, kqc)


def _mix_kernel(g0_ref, g1_ref, g2_ref, g3_ref, lg_ref, w_ref, o_ref, acc):
    k = pl.program_id(1)
    for kk, g_ref in enumerate((g0_ref, g1_ref, g2_ref, g3_ref)):
        @pl.when(k == kk)
        def _(g_ref=g_ref, kk=kk):
            gate = 0.5 * jnp.tanh(0.5 * lg_ref[...].astype(F32)) + 0.5
            contrib = gate * _dot(g_ref[...], w_ref[0])
            if kk == 0:
                acc[...] = contrib
            else:
                acc[...] += contrib

    @pl.when(k == N_BRANCH - 1)
    def _():
        o_ref[...] = acc[...].astype(o_ref.dtype)


def _mix(gated, p, w_branch_bf):
    tt = p.shape[0]
    tm = 512
    assert tt % tm == 0
    g_spec = pl.BlockSpec((tm, BRANCH_WIDTH), lambda i, k: (i, 0))
    return pl.pallas_call(
        _mix_kernel,
        grid=(tt // tm, N_BRANCH),
        in_specs=[g_spec, g_spec, g_spec, g_spec,
                  pl.BlockSpec((tm, D_MODEL), lambda i, k: (i, P_MERGE // D_MODEL + k)),
                  pl.BlockSpec((1, BRANCH_WIDTH, D_MODEL), lambda i, k: (k, 0, 0))],
        out_specs=pl.BlockSpec((tm, D_MODEL), lambda i, k: (i, 0)),
        out_shape=jax.ShapeDtypeStruct((tt, D_MODEL), BF16),
        scratch_shapes=[pltpu.VMEM((tm, D_MODEL), F32)],
        compiler_params=_cparams("parallel", "arbitrary"),
        name="branch_mix",
    )(*gated, p, w_branch_bf)


def _out_kernel(a_ref, w_ref, x_ref, mod_ref, g_ref, b_ref, *rest, with_next):
    mixed = _dot(a_ref[...], w_ref[...])
    r = DEEPNORM_ALPHA * x_ref[...] + mod_ref[0][2:3] * mixed
    mu = jnp.mean(r, axis=-1, keepdims=True)
    rc = r - mu
    var = jnp.mean(rc * rc, axis=-1, keepdims=True)
    xn = rc * lax.rsqrt(var + LN_EPS) * g_ref[...] + b_ref[...]
    if with_next:
        modn_ref, o_ref, h_ref = rest
        o_ref[...] = xn
        mu = jnp.mean(xn, axis=-1, keepdims=True)
        xc = xn - mu
        var = jnp.mean(xc * xc, axis=-1, keepdims=True)
        mn = modn_ref[0]
        h_ref[...] = (xc * lax.rsqrt(var + LN_EPS) * (1.0 + mn[1:2]) + mn[0:1]).astype(h_ref.dtype)
    else:
        (o_ref,) = rest
        o_ref[...] = xn


def _out_next(mixed, w_out_bf, xx, mod, ln_g, ln_b, mod_next, tps, nb):
    tt, d = xx.shape
    tile = pl.BlockSpec((ROW_TILE, d), lambda i: (i, 0))
    modspec = pl.BlockSpec((1, 3, d), lambda i: (_mod_row(i, tps, nb), 0, 0))
    vec = pl.BlockSpec((1, d), lambda i: (0, 0))
    return pl.pallas_call(
        functools.partial(_out_kernel, with_next=True),
        grid=(tt // ROW_TILE,),
        in_specs=[tile, pl.BlockSpec((d, d), lambda i: (0, 0)), tile, modspec, vec, vec, modspec],
        out_specs=[tile, tile],
        out_shape=[jax.ShapeDtypeStruct((tt, d), F32), jax.ShapeDtypeStruct((tt, d), BF16)],
        compiler_params=_cparams("parallel"),
        name="out_proj_norm_next",
    )(mixed, w_out_bf, xx, mod, ln_g[None, :], ln_b[None, :], mod_next)


def _out_last(mixed, w_out_bf, xx, mod, ln_g, ln_b, tps, nb, n_ctx):
    tt, d = xx.shape
    off = n_ctx // ROW_TILE
    tile = pl.BlockSpec((ROW_TILE, d), lambda b, j: (b * tps + off + j, 0))
    vec = pl.BlockSpec((1, d), lambda b, j: (0, 0))
    return pl.pallas_call(
        functools.partial(_out_kernel, with_next=False),
        grid=(nb, tps - off),
        in_specs=[tile, pl.BlockSpec((d, d), lambda b, j: (0, 0)), tile,
                  pl.BlockSpec((1, 3, d), lambda b, j: (b, 0, 0)), vec, vec],
        out_specs=pl.BlockSpec((None, ROW_TILE, d), lambda b, j: (b, j, 0)),
        out_shape=jax.ShapeDtypeStruct((nb, (tps - off) * ROW_TILE, d), F32),
        compiler_params=_cparams("parallel", "parallel"),
        name="out_proj_norm_last",
    )(mixed, w_out_bf, xx, mod, ln_g[None, :], ln_b[None, :])


def kernel(x, c, ctx, c_ctx, w_ada, b_ada, w_in, ssd_conv_w, ssd_conv_b, ssd_dt_bias, ssd_a_log, ssd_d,
           ssd_norm_w, mla_q_norm, mla_w_uq, mla_kv_norm, mla_w_ukv, gqa_sink, hy_conv_w, hy_conv_b,
           hy_w1, hy_b1, hy_w2, hy_b2, hy_w3, hy_freq, hy_d, w_branch, w_out, ln_g, ln_b):
    nb, n_lat, d = x.shape
    n_ctx = ctx.shape[1]
    tb = n_ctx + n_lat
    tt = nb * tb
    tps = tb // ROW_TILE
    assert d == D_MODEL and n_ctx == ROW_TILE and n_lat % 512 == 0 and n_lat % GRID_W == 0

    xx = jnp.concatenate([ctx, x], axis=1).reshape(tt, d)
    cos2, sin2 = _rope_tables(n_ctx, n_lat)
    tabs, tabs_c = _dft_tables(n_lat), _dft_tables(n_ctx)
    cvec = jax.nn.silu(jnp.concatenate([c, c_ctx[None]], axis=0))
    mods = [(_matmul(cvec, w_ada, i, F32, 8, 512, "adaln_mod") + b_ada[i]).reshape(nb + 1, 3, d)
            for i in range(DEPTH)]
    w_packed = _repack_w_in(w_in)

    h = _ln_mod(xx, mods[0], tps, nb)
    out = None
    for i in range(DEPTH):
        last = i == DEPTH - 1
        p = _matmul(h, w_packed, i, BF16, tb, 512, "in_proj")
        p3 = p.reshape(nb, tb, P_WIDTH)
        g_ssd = _ssd(p3, ssd_conv_w[i], ssd_conv_b[i], ssd_dt_bias[i], ssd_a_log[i], ssd_d[i], ssd_norm_w[i],
                     n_ctx).reshape(tt, SSD_WIDTH)
        qm, km, vm, kg, vg = _attn_prep(p, cos2, sin2, mla_q_norm[i], mla_w_uq[i], mla_kv_norm[i], mla_w_ukv[i],
                                        tps)
        r3 = lambda a: a.reshape(nb, tb, a.shape[-1])
        g_mla = _mla(r3(qm), r3(km), r3(vm), p3, n_ctx).reshape(tt, MLA_WIDTH)
        g_gqa = _gqa(p, kg, vg, cos2, sin2, gqa_sink[i], nb, tb, n_ctx)
        fargs = (hy_w1[i], hy_b1[i], hy_w2[i], hy_b2[i], hy_w3[i], hy_freq[i])
        filt = _hyena_filter(n_lat, *tabs, *fargs)
        filt_c = _hyena_filter(n_ctx, *tabs_c, *fargs)
        g_hy = _hyena(p3, hy_conv_w[i], hy_conv_b[i], hy_d[i], tabs, filt, tabs_c, filt_c, n_ctx,
                      not last).reshape(tt, HY_WIDTH)
        mixed = _mix((g_ssd, g_mla, g_gqa, g_hy), p, w_branch[i].astype(BF16))
        w_out_bf = w_out[i].astype(BF16)
        if last:
            out = _out_last(mixed, w_out_bf, xx, mods[i], ln_g[i], ln_b[i], tps, nb, n_ctx)
        else:
            xx, h = _out_next(mixed, w_out_bf, xx, mods[i], ln_g[i], ln_b[i], mods[i + 1], tps, nb)
    return out
```

```python
import functools
import math

import jax
import jax.numpy as jnp
from jax import lax
from jax.experimental import pallas as pl
from jax.experimental.pallas import tpu as pltpu

D_MODEL = 2048
DEPTH = 2
GRID_W = 64
N_BRANCH = 4
BRANCH_WIDTH = D_MODEL // 2
ROPE_DIM = 64
ROPE_BASE = 10000.0
SSD_WIDTH = BRANCH_WIDTH
SSD_HEAD_DIM = 64
SSD_HEADS = SSD_WIDTH // SSD_HEAD_DIM
SSD_GROUPS = 2
SSD_STATE = 128
SSD_CHUNK = 128
SSD_CONV_CH = SSD_WIDTH + 2 * SSD_GROUPS * SSD_STATE
MLA_HEADS = 8
MLA_NOPE = 128
MLA_ROPE = ROPE_DIM
MLA_V = 128
MLA_Q_LORA = 512
MLA_KV_LORA = 256
MLA_WIDTH = MLA_HEADS * MLA_V
GQA_HEADS = 16
GQA_KV_HEADS = 2
GQA_HEAD_DIM = ROPE_DIM
GQA_WIDTH = GQA_HEADS * GQA_HEAD_DIM
GQA_KV_WIDTH = GQA_KV_HEADS * GQA_HEAD_DIM
WINDOW = 128
HY_WIDTH = BRANCH_WIDTH
HY_POS_EMB = 33
HY_BANDS = (HY_POS_EMB - 1) // 2
HY_FILTER_HIDDEN = 64
HY_FAST_DECAY = 0.3
HY_SLOW_DECAY = 1.5
HY_DECAY_TARGET = 0.01
LN_EPS = 1e-6
NEG_INF = -1e30
DEEPNORM_ALPHA = (2 * DEPTH) ** 0.25

IN_SPLITS = (
    SSD_WIDTH, SSD_CONV_CH, 2 * SSD_HEADS, MLA_Q_LORA, MLA_KV_LORA, MLA_ROPE, MLA_WIDTH, GQA_WIDTH,
    GQA_KV_WIDTH, GQA_KV_WIDTH, GQA_WIDTH, 3 * HY_WIDTH, HY_WIDTH, N_BRANCH * D_MODEL,
)

LANE = 128
ROW_TILE = 256
CHUNK = SSD_CHUNK
VMEM_LIMIT_BYTES = 56 * 1024 * 1024

P_MERGE = 0
P_HY_X0 = 8192
P_HY_X1 = 9216
P_HY_V = 10240
P_HY_Z = 11264
P_SSD_Z = 12288
P_SSD_X = 13312
P_MLA_Z = 14336
P_GQA_Q = 15360
P_GQA_Z = 16384
P_MLA_CQ = 17408
P_SSD_B = 17920
P_SSD_C = 18176
P_MLA_CKV = 18432
P_GQA_K = 18688
P_GQA_V = 18816
P_SMALL = 18944
P_WIDTH = 19072
SMALL_DT = 64

F32 = jnp.float32
BF16 = jnp.bfloat16


def _cparams(*sem):
    return pltpu.CompilerParams(dimension_semantics=sem, vmem_limit_bytes=VMEM_LIMIT_BYTES)


def _split3(v):
    h1 = v.astype(BF16)
    r1 = v - h1.astype(F32)
    h2 = r1.astype(BF16)
    h3 = (r1 - h2.astype(F32)).astype(BF16)
    return h1, h2, h3


def _dot(a, b):
    return jnp.dot(a, b, preferred_element_type=F32)


def _dot_nt(a, b):
    return lax.dot_general(a, b, (((1,), (1,)), ((), ())), preferred_element_type=F32)


def _dot_tn(a, b):
    return lax.dot_general(a, b, (((0,), (0,)), ((), ())), preferred_element_type=F32)


def _dot_exact_rhs(a, b_bf16, pieces):
    parts = _split3(a)[:pieces]
    out = _dot(parts[0], b_bf16)
    for p in parts[1:]:
        out = out + _dot(p, b_bf16)
    return out


def _dot3(a, b):
    a1, a2, _ = _split3(a)
    b1, b2, _ = _split3(b)
    return _dot(a1, b1) + (_dot(a1, b2) + _dot(a2, b1))


def _silu(v):
    return v * jax.nn.sigmoid(v)


def _softplus(v):
    return jnp.maximum(v, 0.0) + jnp.log(1.0 + jnp.exp(-jnp.abs(v)))


def _rotswap(t):
    lane = lax.broadcasted_iota(jnp.int32, t.shape, 1)
    return jnp.where((lane & 63) < 32, pltpu.roll(t, 96, 1), pltpu.roll(t, 32, 1))


def _rope(t, cos, sin_signed):
    return t * cos + _rotswap(t) * sin_signed


def _mm_kernel(a_ref, b_ref, o_ref):
    o_ref[...] = _dot(a_ref[...].astype(BF16), b_ref[...].astype(BF16)).astype(o_ref.dtype)


def _matmul(a, b, layer, out_dtype, tm, tn, name):
    m, k = a.shape
    n = b.shape[2]
    tm = min(tm, m)
    tn = min(tn, n)
    return pl.pallas_call(
        _mm_kernel,
        grid=(pl.cdiv(m, tm), pl.cdiv(n, tn)),
        in_specs=[pl.BlockSpec((tm, k), lambda i, j: (i, 0)),
                  pl.BlockSpec((None, k, tn), lambda i, j: (layer, 0, j))],
        out_specs=pl.BlockSpec((tm, tn), lambda i, j: (i, j)),
        out_shape=jax.ShapeDtypeStruct((m, n), out_dtype),
        compiler_params=_cparams("parallel", "arbitrary"),
        name=name,
    )(a, b)


def _mod_row(i, tiles_per_sample, n_samples):
    return jnp.where(i % tiles_per_sample == 0, n_samples, i // tiles_per_sample)


def _ln_mod_kernel(x_ref, mod_ref, h_ref):
    x = x_ref[...]
    mu = jnp.mean(x, axis=-1, keepdims=True)
    xc = x - mu
    var = jnp.mean(xc * xc, axis=-1, keepdims=True)
    m = mod_ref[0]
    h_ref[...] = (xc * lax.rsqrt(var + LN_EPS) * (1.0 + m[1:2]) + m[0:1]).astype(h_ref.dtype)


def _ln_mod(xx, mod, tps, nb):
    tt, d = xx.shape
    return pl.pallas_call(
        _ln_mod_kernel,
        grid=(tt // ROW_TILE,),
        in_specs=[pl.BlockSpec((ROW_TILE, d), lambda i: (i, 0)),
                  pl.BlockSpec((1, 3, d), lambda i: (_mod_row(i, tps, nb), 0, 0))],
        out_specs=pl.BlockSpec((ROW_TILE, d), lambda i: (i, 0)),
        out_shape=jax.ShapeDtypeStruct((tt, d), BF16),
        compiler_params=_cparams("parallel"),
        name="ln_modulate",
    )(xx, mod)


IN_TILE = 1024
IN_ROWS = 1152
SUBLANE = 8


def _in_proj_plan():
    off, start = [], 0
    for size in IN_SPLITS:
        off.append(start)
        start += size
    (o_sz, o_xbc, o_dt, o_cq, o_ckv, o_kpe, o_mz, o_gq, o_gk, o_gv, o_gz, o_hy, o_hz, o_mg) = off
    wide = [(P_MERGE, o_mg, N_BRANCH * D_MODEL), (P_HY_X0, o_hy, 3 * HY_WIDTH), (P_HY_Z, o_hz, HY_WIDTH),
            (P_SSD_Z, o_sz, SSD_WIDTH), (P_SSD_X, o_xbc, SSD_WIDTH), (P_MLA_Z, o_mz, MLA_WIDTH),
            (P_GQA_Q, o_gq, GQA_WIDTH), (P_GQA_Z, o_gz, GQA_WIDTH)]
    n_tiles = -(-P_WIDTH // IN_TILE)
    src = [None] * n_tiles
    for p_off, s_off, width in wide:
        for t in range(width // IN_TILE):
            src[p_off // IN_TILE + t] = s_off + t * IN_TILE
    tail_tiles = [j for j in range(n_tiles) if src[j] is None]
    assert tail_tiles[0] * IN_TILE == P_MLA_CQ
    tail_src = [(o_cq, MLA_Q_LORA), (o_xbc + SSD_WIDTH, 2 * SSD_GROUPS * SSD_STATE), (o_ckv, MLA_KV_LORA),
                (o_gk, 2 * GQA_KV_WIDTH), (o_kpe, MLA_ROPE), (o_dt, 2 * SSD_HEADS)]
    return src, tail_tiles, tail_src


def _in_proj_kernel(start_ref, tail_idx_ref, a_ref, win_ref, tail_ref, o_ref, wb):
    @pl.when(pl.program_id(1) == 0)
    def _():
        from_tail = tail_idx_ref[pl.program_id(0)] >= 0

        @pl.when(jnp.logical_not(from_tail))
        def _():
            wb[...] = win_ref[0].astype(wb.dtype)

        @pl.when(from_tail)
        def _():
            wb[...] = tail_ref[...]

    o_ref[...] = _dot_nt(a_ref[...], wb[...]).astype(o_ref.dtype)


def _in_proj(h, w_in, layer):
    tt, d = h.shape
    w_t = jnp.swapaxes(w_in, 1, 2)
    src, tail_tiles, tail_src = _in_proj_plan()
    tail = jnp.concatenate([w_t[layer, s:s + w, :] for s, w in tail_src], axis=0).astype(BF16)
    tail = jnp.pad(tail, ((0, len(tail_tiles) * IN_TILE - tail.shape[0]), (0, 0)))
    assert all(s is None or s % SUBLANE == 0 for s in src)
    rows = math.gcd(tt, IN_ROWS)
    starts = [0 if s is None else s // SUBLANE for s in src]
    tidx = [tail_tiles.index(j) if s is None else -1 for j, s in enumerate(src)]
    as_i32 = lambda v: jnp.asarray(v, jnp.int32)
    gs = pltpu.PrefetchScalarGridSpec(
        num_scalar_prefetch=2, grid=(len(src), tt // rows),
        in_specs=[pl.BlockSpec((rows, d), lambda j, i, st, ti: (i, 0)),
                  pl.BlockSpec((pl.Element(1), pl.Element(IN_TILE), pl.Element(d)),
                               lambda j, i, st, ti: (layer, st[j] * SUBLANE, 0)),
                  pl.BlockSpec((IN_TILE, d), lambda j, i, st, ti: (jnp.maximum(ti[j], 0), 0))],
        out_specs=pl.BlockSpec((rows, IN_TILE), lambda j, i, st, ti: (i, j)),
        scratch_shapes=[pltpu.VMEM((IN_TILE, d), BF16)])
    return pl.pallas_call(
        _in_proj_kernel, grid_spec=gs,
        out_shape=jax.ShapeDtypeStruct((tt, P_WIDTH), BF16),
        compiler_params=_cparams("parallel", "arbitrary"),
        name="in_proj",
    )(as_i32(starts), as_i32(tidx), h, w_t, tail)


def _conv_chunk(src_ref, w, bias, c, n_ctx_chunks, n_chunks):
    tb = n_chunks * CHUNK
    r0 = pl.multiple_of(c * CHUNK, CHUNK)
    cur = src_ref[pl.ds(r0, CHUNK), :].astype(F32)
    rp = pl.multiple_of(jnp.maximum(r0 - 16, 0), 16)
    rn = pl.multiple_of(jnp.minimum(r0 + CHUNK, tb - 16), 16)
    prev_row = src_ref[pl.ds(rp, 16), :].astype(F32)[15:16]
    next_row = src_ref[pl.ds(rn, 16), :].astype(F32)[0:1]
    has_prev = jnp.logical_and(c != 0, c != n_ctx_chunks)
    has_next = jnp.logical_and(c != n_ctx_chunks - 1, c != n_chunks - 1)
    prev_row = jnp.where(has_prev, prev_row, 0.0)
    next_row = jnp.where(has_next, next_row, 0.0)
    row = lax.broadcasted_iota(jnp.int32, cur.shape, 0)
    sp = jnp.where(row == 0, prev_row, pltpu.roll(cur, 1, 0))
    sn = jnp.where(row == CHUNK - 1, next_row, pltpu.roll(cur, CHUNK - 1, 0))
    return sp * w[0:1] + cur * w[1:2] + sn * w[2:3] + bias


def _ssd_kernel(x_ref, b_ref, c_ref, sm_ref, z_ref, cwx_ref, cbx_ref, cwb_ref, cbb_ref, cwc_ref, cbc_ref,
                dtb_ref, alog_ref, dsk_ref, nw_ref, o_ref, ux, ub, uc, yacc, ybwd, st, st_b,
                *, n_ctx_chunks, n_chunks):
    g = pl.program_id(1)
    hp = ux.shape[1]
    nh = hp // SSD_HEAD_DIM

    def conv_body(c, carry):
        r0 = pl.multiple_of(c * CHUNK, CHUNK)
        ux[pl.ds(r0, CHUNK), :] = _silu(_conv_chunk(x_ref, cwx_ref[...], cbx_ref[...], c, n_ctx_chunks,
                                                    n_chunks)).astype(ux.dtype)
        ub[pl.ds(r0, CHUNK), :] = _silu(_conv_chunk(b_ref, cwb_ref[...], cbb_ref[...], c, n_ctx_chunks,
                                                    n_chunks)).astype(ub.dtype)
        uc[pl.ds(r0, CHUNK), :] = _silu(_conv_chunk(c_ref, cwc_ref[...], cbc_ref[...], c, n_ctx_chunks,
                                                    n_chunks)).astype(uc.dtype)
        return carry

    lax.fori_loop(0, n_chunks, conv_body, 0)

    ri = lax.broadcasted_iota(jnp.int32, (CHUNK, CHUNK), 0)
    ci = lax.broadcasted_iota(jnp.int32, (CHUNK, CHUNK), 1)
    lane_hp = lax.broadcasted_iota(jnp.int32, (CHUNK, hp), 1)
    lane128 = lax.broadcasted_iota(jnp.int32, (CHUNK, LANE), 1)
    a_all = -jnp.exp(alog_ref[0])
    dtb = dtb_ref[0]

    def direction_consts(dirn):
        tri = (ci <= ri) if dirn == 0 else (ci >= ri)
        tri_bf = jnp.where(tri, 1.0, 0.0).astype(BF16)
        expand = jnp.where(lax.broadcasted_iota(jnp.int32, (LANE, hp), 0)
                           == dirn * 16 + (lax.broadcasted_iota(jnp.int32, (LANE, hp), 1) >> 6),
                           1.0, 0.0).astype(BF16)
        return tri, tri_bf, expand

    consts = (direction_consts(0), direction_consts(1))

    def chunk_step(dirn, c, st):
        tri, tri_bf, expand = consts[dirn]
        last = CHUNK - 1 if dirn == 0 else 0
        if True:
            r0 = pl.multiple_of(c * CHUNK, CHUNK)
            xs = ux[pl.ds(r0, CHUNK), :].astype(F32)
            bc = ub[pl.ds(r0, CHUNK), :]
            cc = uc[pl.ds(r0, CHUNK), :]
            sm = sm_ref[pl.ds(r0, CHUNK), :].astype(F32)
            sm = jnp.where(g == 0, pltpu.roll(sm, LANE - SMALL_DT, 1), pltpu.roll(sm, LANE - SMALL_DT - nh, 1))
            dt = _softplus(sm + dtb)
            la = dt * a_all
            cum = _dot_exact_rhs_lhs(tri_bf, la)
            cx = _dot_exact_rhs(cum, expand, 2)
            dtx = _dot_exact_rhs(dt, expand, 2)
            cx_last = cx[last:last + 1]
            ecum = jnp.exp(cx)
            dec_end = jnp.exp(cx_last - cx)
            xd = xs * dtx
            xd_bf = xd.astype(BF16)
            gmat = _dot_nt(cc, bc)
            cum_t = cum.T
            cols = []
            for j in range(nh // 2):
                sc = []
                for h in (2 * j, 2 * j + 1):
                    k = dirn * 16 + h
                    diff = cum[:, k:k + 1] - cum_t[k:k + 1, :]
                    lmat = jnp.exp(jnp.where(tri, diff, NEG_INF))
                    sc.append((gmat * lmat).astype(BF16))
                xcol = xd_bf[:, j * LANE:(j + 1) * LANE]
                lo = lane128 < SSD_HEAD_DIM
                rhs = jnp.concatenate([jnp.where(lo, xcol, jnp.zeros_like(xcol)),
                                       jnp.where(lo, jnp.zeros_like(xcol), xcol)], axis=0)
                cols.append(_dot(jnp.concatenate(sc, axis=1), rhs))
            y = jnp.concatenate(cols, axis=1)
            s_prev = st[...]
            y = y + _dot(cc, s_prev.astype(BF16)) * ecum
            st[...] = jnp.exp(cx_last) * s_prev + _dot_tn(bc, (xd * dec_end).astype(BF16))
            return y, xs, r0

    st[...] = jnp.zeros_like(st)
    st_b[...] = jnp.zeros_like(st_b)

    def scan_body(i, carry):
        y_f, xs_f, r0_f = chunk_step(0, i, st)
        yacc[pl.ds(r0_f, CHUNK), :] = y_f + xs_f * dsk_ref[...]
        c_b = jnp.where(i < n_ctx_chunks, n_ctx_chunks - 1 - i, n_chunks - 1 + n_ctx_chunks - i)
        y_b, _, r0_b = chunk_step(1, c_b, st_b)
        ybwd[pl.ds(r0_b, CHUNK), :] = y_b
        return carry

    lax.fori_loop(0, n_chunks, scan_body, 0)

    def gate_body(c, carry):
        r0 = pl.multiple_of(c * CHUNK, CHUNK)
        tot = (yacc[pl.ds(r0, CHUNK), :] + ybwd[pl.ds(r0, CHUNK), :]) * _silu(z_ref[pl.ds(r0, CHUNK), :].astype(F32))
        ms = jnp.mean(tot * tot, axis=-1, keepdims=True)
        o_ref[pl.ds(r0, CHUNK), :] = (tot * lax.rsqrt(ms + LN_EPS) * nw_ref[...]).astype(o_ref.dtype)
        return carry

    lax.fori_loop(0, n_chunks, gate_body, 0)


def _dot_exact_rhs_lhs(tri_bf, v):
    v1, v2, v3 = _split3(v)
    return _dot(tri_bf, v1) + (_dot(tri_bf, v2) + _dot(tri_bf, v3))


def _ssd(p3, conv_w, conv_b, dt_bias, a_log, d_skip, norm_w, n_ctx):
    nb, tb, _ = p3.shape
    hp = SSD_WIDTH // SSD_GROUPS
    n_chunks = tb // CHUNK
    gn = SSD_GROUPS * SSD_STATE
    cwx, cwb, cwc = conv_w[:, :SSD_WIDTH], conv_w[:, SSD_WIDTH:SSD_WIDTH + gn], conv_w[:, SSD_WIDTH + gn:]
    cb = conv_b[None, :]
    cbx, cbb, cbc = cb[:, :SSD_WIDTH], cb[:, SSD_WIDTH:SSD_WIDTH + gn], cb[:, SSD_WIDTH + gn:]
    hg = SSD_HEADS // SSD_GROUPS

    def per_group(v):
        v = v.reshape(2, SSD_GROUPS, hg).transpose(1, 0, 2)
        v = jnp.pad(v, ((0, 0), (0, 0), (0, 16 - hg))).reshape(SSD_GROUPS, 1, 32)
        return jnp.pad(v, ((0, 0), (0, 0), (0, LANE - 32)))

    dsk = jnp.repeat(d_skip, SSD_HEAD_DIM)[None, :]
    kern = functools.partial(_ssd_kernel, n_ctx_chunks=n_ctx // CHUNK, n_chunks=n_chunks)
    col = lambda off, w: (lambda b, g: (b, 0, off // w + g))
    par = lambda b, g: (0, g)
    return pl.pallas_call(
        kern,
        grid=(nb, SSD_GROUPS),
        in_specs=[pl.BlockSpec((None, tb, hp), col(P_SSD_X, hp)),
                  pl.BlockSpec((None, tb, SSD_STATE), col(P_SSD_B, SSD_STATE)),
                  pl.BlockSpec((None, tb, SSD_STATE), col(P_SSD_C, SSD_STATE)),
                  pl.BlockSpec((None, tb, LANE), lambda b, g: (b, 0, P_SMALL // LANE)),
                  pl.BlockSpec((None, tb, hp), col(P_SSD_Z, hp)),
                  pl.BlockSpec((3, hp), par), pl.BlockSpec((1, hp), par),
                  pl.BlockSpec((3, SSD_STATE), par), pl.BlockSpec((1, SSD_STATE), par),
                  pl.BlockSpec((3, SSD_STATE), par), pl.BlockSpec((1, SSD_STATE), par),
                  pl.BlockSpec((1, 1, LANE), lambda b, g: (g, 0, 0)),
                  pl.BlockSpec((1, 1, LANE), lambda b, g: (g, 0, 0)),
                  pl.BlockSpec((1, hp), par), pl.BlockSpec((1, hp), par)],
        out_specs=pl.BlockSpec((None, tb, hp), lambda b, g: (b, 0, g)),
        out_shape=jax.ShapeDtypeStruct((nb, tb, SSD_WIDTH), BF16),
        scratch_shapes=[pltpu.VMEM((tb, hp), BF16), pltpu.VMEM((tb, SSD_STATE), BF16),
                        pltpu.VMEM((tb, SSD_STATE), BF16), pltpu.VMEM((tb, hp), F32),
                        pltpu.VMEM((tb, hp), F32), pltpu.VMEM((SSD_STATE, hp), F32),
                        pltpu.VMEM((SSD_STATE, hp), F32)],
        compiler_params=_cparams("parallel", "parallel"),
        name="ssd_bidir",
    )(p3, p3, p3, p3, p3, cwx, cbx, cwb, cbb, cwc, cbc, per_group(dt_bias), per_group(a_log), dsk,
      norm_w[None, :])


def _rope_tables(n_ctx, n_lat):
    rows = n_lat // GRID_W
    row = jnp.repeat(jnp.arange(rows, dtype=F32), GRID_W)
    colp = jnp.tile(jnp.arange(GRID_W, dtype=F32), rows)
    n_freq = ROPE_DIM // 4
    inv_freq = ROPE_BASE ** (-jnp.arange(n_freq, dtype=F32) / n_freq)
    ang = jnp.concatenate([row[:, None] * inv_freq[None, :], colp[:, None] * inv_freq[None, :]], axis=-1)
    cos, sin = jnp.cos(ang), jnp.sin(ang)
    cos2 = jnp.tile(cos, (1, 4))
    sin2 = jnp.tile(jnp.concatenate([-sin, sin], axis=-1), (1, 2))
    cos2 = jnp.concatenate([jnp.ones((n_ctx, LANE), F32), cos2], axis=0)
    sin2 = jnp.concatenate([jnp.zeros((n_ctx, LANE), F32), sin2], axis=0)
    return cos2, sin2


def _attn_prep_kernel(cq_ref, ckv_ref, sm_ref, gk_ref, gv_ref, cos_ref, sin_ref, qn_ref, wq_ref, kn_ref,
                      wk_ref, wv_ref, q_ref, k_ref, v_ref, kg_ref, vg_ref):
    cos, sin = cos_ref[...], sin_ref[...]
    lane = lax.broadcasted_iota(jnp.int32, cos.shape, 1)
    scale = (MLA_NOPE + MLA_ROPE) ** -0.5 * math.log2(math.e)
    cq = cq_ref[...].astype(F32)
    cqn = cq * lax.rsqrt(jnp.mean(cq * cq, axis=-1, keepdims=True) + LN_EPS) * qn_ref[...]
    q = _dot(cqn.astype(BF16), wq_ref[...])
    ckv = ckv_ref[...].astype(F32)
    ckvn = (ckv * lax.rsqrt(jnp.mean(ckv * ckv, axis=-1, keepdims=True) + LN_EPS) * kn_ref[...]).astype(BF16)
    kn = _dot(ckvn, wk_ref[...])
    v_ref[...] = _dot(ckvn, wv_ref[...]).astype(v_ref.dtype)
    kpe = jnp.where(lane < MLA_ROPE, _rope(sm_ref[...].astype(F32), cos, sin), 0.0).astype(k_ref.dtype)
    for h in range(MLA_HEADS):
        o = 2 * h * LANE
        q_ref[:, o:o + LANE] = (q[:, o:o + LANE] * scale).astype(q_ref.dtype)
        q_ref[:, o + LANE:o + 2 * LANE] = (_rope(q[:, o + LANE:o + 2 * LANE], cos, sin) * scale).astype(q_ref.dtype)
        k_ref[:, o:o + LANE] = kn[:, h * LANE:(h + 1) * LANE].astype(k_ref.dtype)
        k_ref[:, o + LANE:o + 2 * LANE] = kpe
    kg_ref[...] = _rope(gk_ref[...].astype(F32), cos, sin).astype(kg_ref.dtype)
    vg_ref[...] = gv_ref[...].astype(F32).T.astype(vg_ref.dtype)


def _attn_prep(p, cos2, sin2, q_norm, w_uq, kv_norm, w_ukv, tps):
    tt = p.shape[0]
    hq = MLA_NOPE + MLA_ROPE
    wq = w_uq.reshape(MLA_Q_LORA, MLA_HEADS, hq)
    wq = jnp.pad(wq, ((0, 0), (0, 0), (0, 2 * LANE - hq))).reshape(MLA_Q_LORA, MLA_HEADS * 2 * LANE).astype(BF16)
    wkv = w_ukv.reshape(MLA_KV_LORA, MLA_HEADS, MLA_NOPE + MLA_V)
    wk = wkv[:, :, :MLA_NOPE].reshape(MLA_KV_LORA, MLA_HEADS * MLA_NOPE).astype(BF16)
    wv = wkv[:, :, MLA_NOPE:].reshape(MLA_KV_LORA, MLA_HEADS * MLA_V).astype(BF16)
    full = lambda shape: pl.BlockSpec(shape, lambda i: (0, 0))
    pcol = lambda off, w: pl.BlockSpec((ROW_TILE, w), lambda i: (i, off // w))
    rows = lambda w: pl.BlockSpec((ROW_TILE, w), lambda i: (i, 0))
    tab = pl.BlockSpec((ROW_TILE, LANE), lambda i: (i % tps, 0))
    return pl.pallas_call(
        _attn_prep_kernel,
        grid=(tt // ROW_TILE,),
        in_specs=[pcol(P_MLA_CQ, MLA_Q_LORA), pcol(P_MLA_CKV, MLA_KV_LORA), pcol(P_SMALL, LANE),
                  pcol(P_GQA_K, LANE), pcol(P_GQA_V, LANE), tab, tab,
                  full((1, MLA_Q_LORA)), full(wq.shape), full((1, MLA_KV_LORA)), full(wk.shape), full(wv.shape)],
        out_specs=[rows(MLA_HEADS * 2 * LANE), rows(MLA_HEADS * 2 * LANE), rows(MLA_WIDTH),
                   rows(LANE), pl.BlockSpec((LANE, ROW_TILE), lambda i: (0, i))],
        out_shape=[jax.ShapeDtypeStruct((tt, MLA_HEADS * 2 * LANE), BF16),
                   jax.ShapeDtypeStruct((tt, MLA_HEADS * 2 * LANE), BF16),
                   jax.ShapeDtypeStruct((tt, MLA_WIDTH), BF16),
                   jax.ShapeDtypeStruct((tt, LANE), BF16),
                   jax.ShapeDtypeStruct((LANE, tt), BF16)],
        compiler_params=_cparams("parallel"),
        name="attn_prep",
    )(p, p, p, p, p, cos2, sin2, q_norm[None, :], wq, kv_norm[None, :], wk, wv)


MLA_Q_CHUNK = 256


def _mla_kernel(q_ref, k_ref, v_ref, z_ref, o_ref, *, n_ctx):
    tb = k_ref.shape[0]

    def attend(r0, rows, n_keys):
        s = _dot_nt(q_ref[r0:r0 + rows, :], k_ref[0:n_keys, :])
        m = jnp.max(s, axis=-1, keepdims=True)
        p = jnp.exp2(s - m)
        l = jnp.sum(p, axis=-1, keepdims=True)
        o = _dot(p.astype(BF16), v_ref[0:n_keys, :]) / l
        o_ref[r0:r0 + rows, :] = (o * _silu(z_ref[r0:r0 + rows, :].astype(F32))).astype(o_ref.dtype)

    attend(0, n_ctx, n_ctx)
    for r0 in range(n_ctx, tb, MLA_Q_CHUNK):
        attend(r0, MLA_Q_CHUNK, tb)


def _mla(qm3, km3, vm3, p3, n_ctx):
    nb, tb, _ = p3.shape
    assert (tb - n_ctx) % MLA_Q_CHUNK == 0
    blk = lambda w, off: pl.BlockSpec((None, tb, w), lambda b, h: (b, 0, off // w + h))
    return pl.pallas_call(
        functools.partial(_mla_kernel, n_ctx=n_ctx),
        grid=(nb, MLA_HEADS),
        in_specs=[blk(2 * LANE, 0), blk(2 * LANE, 0), blk(LANE, 0), blk(LANE, P_MLA_Z)],
        out_specs=blk(LANE, 0),
        out_shape=jax.ShapeDtypeStruct((nb, tb, MLA_WIDTH), BF16),
        compiler_params=_cparams("parallel", "parallel"),
        name="mla_attention",
    )(qm3, km3, vm3, p3)


GQA_HEAD_BATCH = 8

def _gqa_kernel(sink_ref, q_ref, kp_ref, ko_ref, kn_ref, kc_ref, vp_ref, vo_ref, vn_ref, vc_ref, cos_ref,
                sin_ref, z_ref, o_ref, *, n_ctx_blocks):
    blk = pl.program_id(1)
    n_blk = pl.num_programs(1)
    is_lat = blk >= n_ctx_blocks
    has_prev = blk > n_ctx_blocks
    has_next = jnp.logical_and(is_lat, blk < n_blk - 1)
    cos, sin = cos_ref[...], sin_ref[...]
    log2e = math.log2(math.e)
    scale = GQA_HEAD_DIM ** -0.5 * log2e
    w = q_ref.shape[0]
    hd = GQA_HEAD_DIM
    hpk = GQA_HEADS // GQA_KV_HEADS
    q = jnp.concatenate([_rope(q_ref[:, c * LANE:(c + 1) * LANE].astype(F32), cos, sin) * scale
                         for c in range(GQA_WIDTH // LANE)], axis=1)
    qt = q.T.astype(BF16)
    kall = jnp.concatenate([kp_ref[...], ko_ref[...], kn_ref[...], kc_ref[...]], axis=0)
    vt = jnp.concatenate([vp_ref[...], vo_ref[...], vn_ref[...], vc_ref[...]], axis=1)
    kj = lax.broadcasted_iota(jnp.int32, (w, w), 0)
    qi = lax.broadcasted_iota(jnp.int32, (w, w), 1)
    m_prev = jnp.logical_and(kj >= qi, has_prev)
    m_next = jnp.logical_and(kj <= qi, has_next)
    zeros = jnp.zeros((hd, w), BF16)
    nbat = GQA_HEAD_BATCH
    tile = lambda a: jnp.concatenate([a] * nbat, axis=1)
    m_prev, m_next = tile(m_prev), tile(m_next)
    outs = []
    for h0 in range(0, GQA_HEADS, nbat):
        kh = h0 // hpk
        heads = range(h0, h0 + nbat)
        rhs = jnp.concatenate(
            [jnp.concatenate([qt[h * hd:(h + 1) * hd], zeros] if kh == 0 else [zeros, qt[h * hd:(h + 1) * hd]],
                             axis=0) for h in heads], axis=1)
        st = _dot(kall, rhs)
        s_p = jnp.where(m_prev, st[0:w], NEG_INF)
        s_o = jnp.where(is_lat, st[w:2 * w], NEG_INF)
        s_n = jnp.where(m_next, st[2 * w:3 * w], NEG_INF)
        s = jnp.concatenate([s_p, s_o, s_n, st[3 * w:]], axis=0)
        sink = jnp.concatenate([jnp.full((1, w), sink_ref[h] * log2e, F32) for h in heads], axis=1)
        m = jnp.maximum(jnp.max(s, axis=0, keepdims=True), sink)
        pr = jnp.exp2(s - m)
        l = jnp.sum(pr, axis=0, keepdims=True) + jnp.exp2(sink - m)
        ot = _dot(vt[kh * hd:(kh + 1) * hd], pr.astype(BF16)) / l
        outs.extend(ot[:, i * w:(i + 1) * w] for i in range(nbat))
    y = jnp.concatenate(outs, axis=0).T
    o_ref[...] = (y * _silu(z_ref[...].astype(F32))).astype(o_ref.dtype)


def _gqa(p, kg, vgt, cos2, sin2, sink, nb, tb, n_ctx):
    w = WINDOW
    bps = tb // w
    row = lambda b, i: b * bps + i
    prev = lambda b, i: b * bps + jnp.maximum(i - 1, 0)
    nxt = lambda b, i: b * bps + jnp.minimum(i + 1, bps - 1)
    kspec = lambda f: pl.BlockSpec((w, LANE), lambda b, i: (f(b, i), 0))
    vspec = lambda f: pl.BlockSpec((LANE, w), lambda b, i: (0, f(b, i)))
    kctx = pl.BlockSpec((n_ctx, LANE), lambda b, i: (b * (tb // n_ctx), 0))
    vctx = pl.BlockSpec((LANE, n_ctx), lambda b, i: (0, b * (tb // n_ctx)))
    tab = pl.BlockSpec((w, LANE), lambda b, i: (i, 0))
    return pl.pallas_call(
        functools.partial(_gqa_kernel, n_ctx_blocks=n_ctx // w),
        grid=(nb, bps),
        in_specs=[pl.BlockSpec(memory_space=pltpu.SMEM),
                  pl.BlockSpec((w, GQA_WIDTH), lambda b, i: (row(b, i), P_GQA_Q // GQA_WIDTH)),
                  kspec(prev), kspec(row), kspec(nxt), kctx, vspec(prev), vspec(row), vspec(nxt), vctx, tab, tab,
                  pl.BlockSpec((w, GQA_WIDTH), lambda b, i: (row(b, i), P_GQA_Z // GQA_WIDTH))],
        out_specs=pl.BlockSpec((w, GQA_WIDTH), lambda b, i: (row(b, i), 0)),
        out_shape=jax.ShapeDtypeStruct((nb * tb, GQA_WIDTH), BF16),
        compiler_params=_cparams("parallel", "arbitrary"),
        name="gqa_window_attention",
    )(sink, p, kg, kg, kg, kg, vgt, vgt, vgt, vgt, cos2, sin2, p)


def _dft_tables(n):
    m = jnp.arange(n, dtype=jnp.int32)[None, :]

    def small(kvals):
        ang = ((kvals[:, None] * m) % (2 * n)).astype(F32) * (math.pi / n)
        return jnp.cos(ang), jnp.sin(ang)

    ca, sa = small(64 * jnp.arange(n // 64, dtype=jnp.int32))
    cb, sb = small(jnp.arange(64, dtype=jnp.int32))
    cos_t = ca[:, None, :] * cb[None, :, :] - sa[:, None, :] * sb[None, :, :]
    sin_t = sa[:, None, :] * cb[None, :, :] + ca[:, None, :] * sb[None, :, :]
    return cos_t.reshape(n, n).astype(BF16), sin_t.reshape(n, n).astype(BF16)


def _hyena_feats(n):
    t = jnp.linspace(0.0, 1.0, n, dtype=F32)[:, None]
    w_ang = (2.0 * math.pi / n) * jnp.arange(n, dtype=F32)[:, None]
    bands = jnp.linspace(1e-4, HY_BANDS - 1, HY_BANDS, dtype=F32)[None, :]
    feats = jnp.concatenate([t, jnp.cos(bands * w_ang), -jnp.sin(bands * w_ang)], axis=-1)
    return jnp.pad(feats, ((0, 0), (0, LANE - HY_POS_EMB)))


def _hyena_filter_kernel(feats_ref, w1_ref, b1_ref, w2_ref, b2_ref, w3f_ref, w3b_ref, freq_ref, dl_ref,
                         cos_ref, sin_ref, kr_ref, ks_ref, kq_ref, hs, hd, *, n):
    kt = pl.program_id(1)
    tk = cos_ref.shape[0]

    @pl.when(kt == 0)
    def _():
        feats = feats_ref[...]
        t = feats[:, 0:1]
        hdn = jnp.sin(freq_ref[0:1, :] * (_dot3(feats, w1_ref[...]) + b1_ref[...]))
        hdn = jnp.sin(freq_ref[1:2, :] * (_dot3(hdn, w2_ref[...]) + b2_ref[...]))
        decay = jnp.exp(-t * dl_ref[...])
        hf = _dot3(hdn, w3f_ref[...]) * decay
        hb = _dot3(hdn, w3b_ref[...]) * decay
        row = lax.broadcasted_iota(jnp.int32, hb.shape, 0)
        hb = jnp.where(row == 0, 0.0, hb)
        hsum = hf + hb
        hs[...] = hsum.astype(hs.dtype)
        hd[...] = (hf - hb).astype(hd.dtype)
        alt = jnp.where((row & 1) == 0, 1.0, -1.0)
        kq_ref[...] = jnp.sum(hsum * alt, axis=0, keepdims=True) * (1.0 / (2 * n))

    kidx = kt * tk + lax.broadcasted_iota(jnp.int32, (tk, 1), 0)
    wk = jnp.where(kidx == 0, 0.5 / n, 1.0 / n)
    kr_ref[...] = _dot(cos_ref[...], hs[...]) * wk
    ks_ref[...] = _dot(sin_ref[...], hd[...]) * wk


def _hyena_filter(n, cos_t, sin_t, w1, b1, w2, b2, w3, freq):
    cbw = 512
    tk = min(256, n)
    deltas = jnp.abs(jnp.linspace(math.log(HY_DECAY_TARGET) / HY_FAST_DECAY,
                                  math.log(HY_DECAY_TARGET) / HY_SLOW_DECAY, HY_WIDTH, dtype=F32))[None, :]
    w1p = jnp.pad(w1, ((0, LANE - HY_POS_EMB), (0, 0)))
    full = lambda a: pl.BlockSpec(a.shape, lambda c, k: (0,) * a.ndim)
    nblk = HY_WIDTH // cbw
    kern = functools.partial(_hyena_filter_kernel, n=n)
    feats = _hyena_feats(n)
    return pl.pallas_call(
        kern,
        grid=(nblk, n // tk),
        in_specs=[full(feats), full(w1p), pl.BlockSpec((1, HY_FILTER_HIDDEN), lambda c, k: (0, 0)),
                  full(w2), pl.BlockSpec((1, HY_FILTER_HIDDEN), lambda c, k: (0, 0)),
                  pl.BlockSpec((HY_FILTER_HIDDEN, cbw), lambda c, k: (0, c)),
                  pl.BlockSpec((HY_FILTER_HIDDEN, cbw), lambda c, k: (0, nblk + c)),
                  full(freq), pl.BlockSpec((1, cbw), lambda c, k: (0, c)),
                  pl.BlockSpec((tk, n), lambda c, k: (k, 0)), pl.BlockSpec((tk, n), lambda c, k: (k, 0))],
        out_specs=[pl.BlockSpec((tk, cbw), lambda c, k: (k, c)), pl.BlockSpec((tk, cbw), lambda c, k: (k, c)),
                   pl.BlockSpec((1, cbw), lambda c, k: (0, c))],
        out_shape=[jax.ShapeDtypeStruct((n, HY_WIDTH), F32), jax.ShapeDtypeStruct((n, HY_WIDTH), F32),
                   jax.ShapeDtypeStruct((1, HY_WIDTH), F32)],
        scratch_shapes=[pltpu.VMEM((n, cbw), BF16), pltpu.VMEM((n, cbw), BF16)],
        compiler_params=_cparams("parallel", "arbitrary"),
        name="hyena_filter",
    )(feats, w1p, b1[None, :], w2, b2[None, :], w3, w3, freq, deltas, cos_t, sin_t)


def _spectral_product(a, bq, kr, ks):
    return (a * kr - bq * ks).astype(BF16), (a * ks + bq * kr).astype(BF16)


HY_TILE = 512


def _hyena_kernel(x0_ref, x1_ref, v_ref, z_ref, cw0_ref, cb0_ref, cw1_ref, cb1_ref, cwv_ref, cbv_ref, d_ref,
                  cr_ref, sr_ref, kr_ref, ks_ref, kq_ref, ccx_ref, scx_ref, krx_ref, ksx_ref, kqx_ref,
                  o_ref, u, yr_s, ys_s, unyq, *, n_ctx, n_chunks, with_ctx):
    kt = pl.program_id(2)
    nk = pl.num_programs(2) // 2
    n_ctx_chunks = n_ctx // CHUNK
    tb = n_chunks * CHUNK
    tk = cr_ref.shape[0]

    def alt_sign(shape):
        return jnp.where((lax.broadcasted_iota(jnp.int32, shape, 0) & 1) == 0, 1.0, -1.0)

    def gate_chunk(c):
        x0c = _conv_chunk(x0_ref, cw0_ref[...], cb0_ref[...], c, n_ctx_chunks, n_chunks)
        r0 = pl.multiple_of(c * CHUNK, CHUNK)
        return x0c * _silu(z_ref[pl.ds(r0, CHUNK), :].astype(F32))

    @pl.when(kt == 0)
    def _():
        def conv_body(c, carry):
            r0 = pl.multiple_of(c * CHUNK, CHUNK)
            x1c = _conv_chunk(x1_ref, cw1_ref[...], cb1_ref[...], c, n_ctx_chunks, n_chunks)
            vc = _conv_chunk(v_ref, cwv_ref[...], cbv_ref[...], c, n_ctx_chunks, n_chunks)
            u[pl.ds(r0, CHUNK), :] = (x1c * vc).astype(u.dtype)
            return carry

        lax.fori_loop(0, n_chunks, conv_body, 0)
        ulf = u[n_ctx:tb, :].astype(F32)
        unyq[...] = jnp.sum(ulf * alt_sign(ulf.shape), axis=0, keepdims=True) * kq_ref[...]

    @pl.when(kt < nk)
    def _():
        ul = u[n_ctx:tb, :]
        yr, ys = _spectral_product(_dot(cr_ref[...], ul), _dot(sr_ref[...], ul), kr_ref[...], ks_ref[...])
        f0 = pl.multiple_of(kt * tk, tk)
        yr_s[pl.ds(f0, tk), :] = yr
        ys_s[pl.ds(f0, tk), :] = ys

    @pl.when(kt >= nk)
    def _():
        t = kt - nk
        y = _dot(cr_ref[...], yr_s[...]) + _dot(sr_ref[...], ys_s[...])
        for j in range(tk // CHUNK):
            c = n_ctx_chunks + t * (tk // CHUNK) + j
            r0 = pl.multiple_of(c * CHUNK, CHUNK)
            uf = u[pl.ds(r0, CHUNK), :].astype(F32)
            yj = y[j * CHUNK:(j + 1) * CHUNK] + alt_sign(uf.shape) * unyq[...] + uf * d_ref[...]
            o_ref[pl.ds(r0, CHUNK), :] = (yj * gate_chunk(c)).astype(o_ref.dtype)

    @pl.when(kt == 2 * nk - 1)
    def _():
        if with_ctx:
            uc = u[0:n_ctx, :]
            ucf = uc.astype(F32)
            alt = alt_sign(ucf.shape)
            yr, ys = _spectral_product(_dot(ccx_ref[...], uc), _dot(scx_ref[...], uc), krx_ref[...], ksx_ref[...])
            y = _dot(ccx_ref[...], yr) + _dot(scx_ref[...], ys)
            y = y + alt * (jnp.sum(ucf * alt, axis=0, keepdims=True) * kqx_ref[...]) + ucf * d_ref[...]
            for c in range(n_ctx_chunks):
                o_ref[c * CHUNK:(c + 1) * CHUNK, :] = (y[c * CHUNK:(c + 1) * CHUNK] * gate_chunk(c)).astype(o_ref.dtype)
        else:
            o_ref[0:n_ctx, :] = jnp.zeros((n_ctx, o_ref.shape[1]), o_ref.dtype)


def _hyena(p3, conv_w, conv_b, d_skip, tabs, filt, tabs_c, filt_c, n_ctx, with_ctx):
    nb, tb, _ = p3.shape
    n_lat = tb - n_ctx
    cbw = 512
    tk = HY_TILE
    nk = n_lat // tk
    cos_t, sin_t = tabs
    kr, ks, kq = filt
    cos_c, sin_c = tabs_c
    krc, ksc, kqc = filt_c
    cb = conv_b[None, :]
    one = pl.Buffered(1)
    pcol = lambda off: pl.BlockSpec((None, tb, cbw), lambda b, c, k: (b, 0, off // cbw + c), pipeline_mode=one)
    par = lambda rows, sec: pl.BlockSpec((rows, cbw), lambda b, c, k: (0, sec * (HY_WIDTH // cbw) + c))
    kern = functools.partial(_hyena_kernel, n_ctx=n_ctx, n_chunks=tb // CHUNK, with_ctx=with_ctx)
    small = lambda a: pl.BlockSpec(a.shape, lambda b, c, k: (0, 0))
    chan = lambda rows: pl.BlockSpec((rows, cbw), lambda b, c, k: (0, c))
    table = pl.BlockSpec((tk, n_lat), lambda b, c, k: (k % nk, 0))
    spec = pl.BlockSpec((tk, cbw), lambda b, c, k: (jnp.minimum(k, nk - 1), c))
    return pl.pallas_call(
        kern,
        grid=(nb, HY_WIDTH // cbw, 2 * nk),
        in_specs=[pcol(P_HY_X0), pcol(P_HY_X1), pcol(P_HY_V), pcol(P_HY_Z),
                  par(3, 0), par(1, 0), par(3, 1), par(1, 1), par(3, 2), par(1, 2), chan(1),
                  table, table, spec, spec, chan(1), small(cos_c), small(sin_c), chan(n_ctx), chan(n_ctx), chan(1)],
        out_specs=pl.BlockSpec((None, tb, cbw), lambda b, c, k: (b, 0, c)),
        out_shape=jax.ShapeDtypeStruct((nb, tb, HY_WIDTH), BF16),
        scratch_shapes=[pltpu.VMEM((tb, cbw), BF16), pltpu.VMEM((n_lat, cbw), BF16), pltpu.VMEM((n_lat, cbw), BF16),
                        pltpu.VMEM((1, cbw), F32)],
        compiler_params=_cparams("parallel", "parallel", "arbitrary"),
        name="hyena_dft_conv",
    )(p3, p3, p3, p3, conv_w, cb, conv_w, cb, conv_w, cb, d_skip[None, :], cos_t, sin_t,
      kr, ks, kq, cos_c, sin_c, krc, ksc, kqc)


def _mix_kernel(g0_ref, g1_ref, g2_ref, g3_ref, lg_ref, w_ref, o_ref, acc):
    k = pl.program_id(1)
    for kk, g_ref in enumerate((g0_ref, g1_ref, g2_ref, g3_ref)):
        @pl.when(k == kk)
        def _(g_ref=g_ref, kk=kk):
            gate = 0.5 * jnp.tanh(0.5 * lg_ref[...].astype(F32)) + 0.5
            contrib = gate * _dot(g_ref[...], w_ref[0])
            if kk == 0:
                acc[...] = contrib
            else:
                acc[...] += contrib

    @pl.when(k == N_BRANCH - 1)
    def _():
        o_ref[...] = acc[...].astype(o_ref.dtype)


def _mix(gated, p, w_branch_bf):
    tt = p.shape[0]
    tm = 512
    assert tt % tm == 0
    g_spec = pl.BlockSpec((tm, BRANCH_WIDTH), lambda i, k: (i, 0))
    return pl.pallas_call(
        _mix_kernel,
        grid=(tt // tm, N_BRANCH),
        in_specs=[g_spec, g_spec, g_spec, g_spec,
                  pl.BlockSpec((tm, D_MODEL), lambda i, k: (i, P_MERGE // D_MODEL + k)),
                  pl.BlockSpec((1, BRANCH_WIDTH, D_MODEL), lambda i, k: (k, 0, 0))],
        out_specs=pl.BlockSpec((tm, D_MODEL), lambda i, k: (i, 0)),
        out_shape=jax.ShapeDtypeStruct((tt, D_MODEL), BF16),
        scratch_shapes=[pltpu.VMEM((tm, D_MODEL), F32)],
        compiler_params=_cparams("parallel", "arbitrary"),
        name="branch_mix",
    )(*gated, p, w_branch_bf)


def _out_kernel(a_ref, w_ref, x_ref, mod_ref, g_ref, b_ref, *rest, with_next):
    mixed = _dot(a_ref[...], w_ref[...])
    r = DEEPNORM_ALPHA * x_ref[...] + mod_ref[0][2:3] * mixed
    mu = jnp.mean(r, axis=-1, keepdims=True)
    rc = r - mu
    var = jnp.mean(rc * rc, axis=-1, keepdims=True)
    xn = rc * lax.rsqrt(var + LN_EPS) * g_ref[...] + b_ref[...]
    if with_next:
        modn_ref, o_ref, h_ref = rest
        o_ref[...] = xn
        mu = jnp.mean(xn, axis=-1, keepdims=True)
        xc = xn - mu
        var = jnp.mean(xc * xc, axis=-1, keepdims=True)
        mn = modn_ref[0]
        h_ref[...] = (xc * lax.rsqrt(var + LN_EPS) * (1.0 + mn[1:2]) + mn[0:1]).astype(h_ref.dtype)
    else:
        (o_ref,) = rest
        o_ref[...] = xn


def _out_next(mixed, w_out_bf, xx, mod, ln_g, ln_b, mod_next, tps, nb):
    tt, d = xx.shape
    tile = pl.BlockSpec((ROW_TILE, d), lambda i: (i, 0))
    modspec = pl.BlockSpec((1, 3, d), lambda i: (_mod_row(i, tps, nb), 0, 0))
    vec = pl.BlockSpec((1, d), lambda i: (0, 0))
    return pl.pallas_call(
        functools.partial(_out_kernel, with_next=True),
        grid=(tt // ROW_TILE,),
        in_specs=[tile, pl.BlockSpec((d, d), lambda i: (0, 0)), tile, modspec, vec, vec, modspec],
        out_specs=[tile, tile],
        out_shape=[jax.ShapeDtypeStruct((tt, d), F32), jax.ShapeDtypeStruct((tt, d), BF16)],
        compiler_params=_cparams("parallel"),
        name="out_proj_norm_next",
    )(mixed, w_out_bf, xx, mod, ln_g[None, :], ln_b[None, :], mod_next)


def _out_last(mixed, w_out_bf, xx, mod, ln_g, ln_b, tps, nb, n_ctx):
    tt, d = xx.shape
    off = n_ctx // ROW_TILE
    tile = pl.BlockSpec((ROW_TILE, d), lambda b, j: (b * tps + off + j, 0))
    vec = pl.BlockSpec((1, d), lambda b, j: (0, 0))
    return pl.pallas_call(
        functools.partial(_out_kernel, with_next=False),
        grid=(nb, tps - off),
        in_specs=[tile, pl.BlockSpec((d, d), lambda b, j: (0, 0)), tile,
                  pl.BlockSpec((1, 3, d), lambda b, j: (b, 0, 0)), vec, vec],
        out_specs=pl.BlockSpec((None, ROW_TILE, d), lambda b, j: (b, j, 0)),
        out_shape=jax.ShapeDtypeStruct((nb, (tps - off) * ROW_TILE, d), F32),
        compiler_params=_cparams("parallel", "parallel"),
        name="out_proj_norm_last",
    )(mixed, w_out_bf, xx, mod, ln_g[None, :], ln_b[None, :])


def kernel(x, c, ctx, c_ctx, w_ada, b_ada, w_in, ssd_conv_w, ssd_conv_b, ssd_dt_bias, ssd_a_log, ssd_d,
           ssd_norm_w, mla_q_norm, mla_w_uq, mla_kv_norm, mla_w_ukv, gqa_sink, hy_conv_w, hy_conv_b,
           hy_w1, hy_b1, hy_w2, hy_b2, hy_w3, hy_freq, hy_d, w_branch, w_out, ln_g, ln_b):
    nb, n_lat, d = x.shape
    n_ctx = ctx.shape[1]
    tb = n_ctx + n_lat
    tt = nb * tb
    tps = tb // ROW_TILE
    assert d == D_MODEL and n_ctx == ROW_TILE and n_lat % 512 == 0 and n_lat % GRID_W == 0

    xx = jnp.concatenate([ctx, x], axis=1).reshape(tt, d)
    cos2, sin2 = _rope_tables(n_ctx, n_lat)
    tabs, tabs_c = _dft_tables(n_lat), _dft_tables(n_ctx)
    cvec = jax.nn.silu(jnp.concatenate([c, c_ctx[None]], axis=0))
    mods = [(_matmul(cvec, w_ada, i, F32, 8, 512, "adaln_mod") + b_ada[i]).reshape(nb + 1, 3, d)
            for i in range(DEPTH)]

    h = _ln_mod(xx, mods[0], tps, nb)
    out = None
    for i in range(DEPTH):
        last = i == DEPTH - 1
        p = _in_proj(h, w_in, i)
        p3 = p.reshape(nb, tb, P_WIDTH)
        g_ssd = _ssd(p3, ssd_conv_w[i], ssd_conv_b[i], ssd_dt_bias[i], ssd_a_log[i], ssd_d[i], ssd_norm_w[i],
                     n_ctx).reshape(tt, SSD_WIDTH)
        qm, km, vm, kg, vg = _attn_prep(p, cos2, sin2, mla_q_norm[i], mla_w_uq[i], mla_kv_norm[i], mla_w_ukv[i],
                                        tps)
        r3 = lambda a: a.reshape(nb, tb, a.shape[-1])
        g_mla = _mla(r3(qm), r3(km), r3(vm), p3, n_ctx).reshape(tt, MLA_WIDTH)
        g_gqa = _gqa(p, kg, vg, cos2, sin2, gqa_sink[i], nb, tb, n_ctx)
        fargs = (hy_w1[i], hy_b1[i], hy_w2[i], hy_b2[i], hy_w3[i], hy_freq[i])
        filt = _hyena_filter(n_lat, *tabs, *fargs)
        filt_c = _hyena_filter(n_ctx, *tabs_c, *fargs)
        g_hy = _hyena(p3, hy_conv_w[i], hy_conv_b[i], hy_d[i], tabs, filt, tabs_c, filt_c, n_ctx,
                      not last).reshape(tt, HY_WIDTH)
        mixed = _mix((g_ssd, g_mla, g_gqa, g_hy), p, w_branch[i].astype(BF16))
        w_out_bf = w_out[i].astype(BF16)
        if last:
            out = _out_last(mixed, w_out_bf, xx, mods[i], ln_g[i], ln_b[i], tps, nb, n_ctx)
        else:
            xx, h = _out_next(mixed, w_out_bf, xx, mods[i], ln_g[i], ln_b[i], mods[i + 1], tps, nb)
    return out
```

```python
import functools
import math

import jax
import jax.numpy as jnp
from jax import lax
from jax.experimental import pallas as pl
from jax.experimental.pallas import tpu as pltpu

D_MODEL = 2048
DEPTH = 2
GRID_W = 64
N_BRANCH = 4
BRANCH_WIDTH = D_MODEL // 2
ROPE_DIM = 64
ROPE_BASE = 10000.0
SSD_WIDTH = BRANCH_WIDTH
SSD_HEAD_DIM = 64
SSD_HEADS = SSD_WIDTH // SSD_HEAD_DIM
SSD_GROUPS = 2
SSD_STATE = 128
SSD_CHUNK = 128
SSD_CONV_CH = SSD_WIDTH + 2 * SSD_GROUPS * SSD_STATE
MLA_HEADS = 8
MLA_NOPE = 128
MLA_ROPE = ROPE_DIM
MLA_V = 128
MLA_Q_LORA = 512
MLA_KV_LORA = 256
MLA_WIDTH = MLA_HEADS * MLA_V
GQA_HEADS = 16
GQA_KV_HEADS = 2
GQA_HEAD_DIM = ROPE_DIM
GQA_WIDTH = GQA_HEADS * GQA_HEAD_DIM
GQA_KV_WIDTH = GQA_KV_HEADS * GQA_HEAD_DIM
WINDOW = 128
HY_WIDTH = BRANCH_WIDTH
HY_POS_EMB = 33
HY_BANDS = (HY_POS_EMB - 1) // 2
HY_FILTER_HIDDEN = 64
HY_FAST_DECAY = 0.3
HY_SLOW_DECAY = 1.5
HY_DECAY_TARGET = 0.01
LN_EPS = 1e-6
NEG_INF = -1e30
DEEPNORM_ALPHA = (2 * DEPTH) ** 0.25

IN_SPLITS = (
    SSD_WIDTH, SSD_CONV_CH, 2 * SSD_HEADS, MLA_Q_LORA, MLA_KV_LORA, MLA_ROPE, MLA_WIDTH, GQA_WIDTH,
    GQA_KV_WIDTH, GQA_KV_WIDTH, GQA_WIDTH, 3 * HY_WIDTH, HY_WIDTH, N_BRANCH * D_MODEL,
)

LANE = 128
ROW_TILE = 256
CHUNK = SSD_CHUNK
VMEM_LIMIT_BYTES = 56 * 1024 * 1024

P_MERGE = 0
P_HY_X0 = 8192
P_HY_X1 = 9216
P_HY_V = 10240
P_HY_Z = 11264
P_SSD_Z = 12288
P_SSD_X = 13312
P_MLA_Z = 14336
P_GQA_Q = 15360
P_GQA_Z = 16384
P_MLA_CQ = 17408
P_SSD_B = 17920
P_SSD_C = 18176
P_MLA_CKV = 18432
P_GQA_K = 18688
P_GQA_V = 18816
P_SMALL = 18944
P_WIDTH = 19072
SMALL_DT = 64

F32 = jnp.float32
BF16 = jnp.bfloat16


def _cparams(*sem):
    return pltpu.CompilerParams(dimension_semantics=sem, vmem_limit_bytes=VMEM_LIMIT_BYTES)


def _split3(v):
    h1 = v.astype(BF16)
    r1 = v - h1.astype(F32)
    h2 = r1.astype(BF16)
    h3 = (r1 - h2.astype(F32)).astype(BF16)
    return h1, h2, h3


def _dot(a, b):
    return jnp.dot(a, b, preferred_element_type=F32)


def _dot_nt(a, b):
    return lax.dot_general(a, b, (((1,), (1,)), ((), ())), preferred_element_type=F32)


def _dot_tn(a, b):
    return lax.dot_general(a, b, (((0,), (0,)), ((), ())), preferred_element_type=F32)


def _dot_exact_rhs(a, b_bf16, pieces):
    parts = _split3(a)[:pieces]
    out = _dot(parts[0], b_bf16)
    for p in parts[1:]:
        out = out + _dot(p, b_bf16)
    return out


def _dot3(a, b):
    a1, a2, _ = _split3(a)
    b1, b2, _ = _split3(b)
    return _dot(a1, b1) + (_dot(a1, b2) + _dot(a2, b1))


def _silu(v):
    return v * jax.nn.sigmoid(v)


def _softplus(v):
    return jnp.maximum(v, 0.0) + jnp.log(1.0 + jnp.exp(-jnp.abs(v)))


def _rotswap(t):
    lane = lax.broadcasted_iota(jnp.int32, t.shape, 1)
    return jnp.where((lane & 63) < 32, pltpu.roll(t, 96, 1), pltpu.roll(t, 32, 1))


def _rope(t, cos, sin_signed):
    return t * cos + _rotswap(t) * sin_signed


def _mm_kernel(a_ref, b_ref, o_ref):
    o_ref[...] = _dot(a_ref[...].astype(BF16), b_ref[...].astype(BF16)).astype(o_ref.dtype)


def _matmul(a, b, layer, out_dtype, tm, tn, name):
    m, k = a.shape
    n = b.shape[2]
    tm = min(tm, m)
    tn = min(tn, n)
    return pl.pallas_call(
        _mm_kernel,
        grid=(pl.cdiv(m, tm), pl.cdiv(n, tn)),
        in_specs=[pl.BlockSpec((tm, k), lambda i, j: (i, 0)),
                  pl.BlockSpec((None, k, tn), lambda i, j: (layer, 0, j))],
        out_specs=pl.BlockSpec((tm, tn), lambda i, j: (i, j)),
        out_shape=jax.ShapeDtypeStruct((m, n), out_dtype),
        compiler_params=_cparams("parallel", "arbitrary"),
        name=name,
    )(a, b)


def _mod_row(i, tiles_per_sample, n_samples):
    return jnp.where(i % tiles_per_sample == 0, n_samples, i // tiles_per_sample)


def _ln_mod_kernel(ctx_ref, x_ref, mod_ref, xx_ref, h_ref):
    def emit(x):
        xx_ref[...] = x
        mu = jnp.mean(x, axis=-1, keepdims=True)
        xc = x - mu
        var = jnp.mean(xc * xc, axis=-1, keepdims=True)
        m = mod_ref[0]
        h_ref[...] = (xc * lax.rsqrt(var + LN_EPS) * (1.0 + m[1:2]) + m[0:1]).astype(h_ref.dtype)

    @pl.when(pl.program_id(1) == 0)
    def _():
        emit(ctx_ref[...])

    @pl.when(pl.program_id(1) != 0)
    def _():
        emit(x_ref[...])


def _ln_mod(x, ctx, mod):
    nb, n_lat, d = x.shape
    assert ctx.shape[1] == ROW_TILE
    tps = 1 + n_lat // ROW_TILE
    rows = pl.BlockSpec((ROW_TILE, d), lambda b, i: (b * tps + i, 0))
    return pl.pallas_call(
        _ln_mod_kernel,
        grid=(nb, tps),
        in_specs=[pl.BlockSpec((None, ROW_TILE, d), lambda b, i: (b, 0, 0)),
                  pl.BlockSpec((None, ROW_TILE, d), lambda b, i: (b, jnp.maximum(i - 1, 0), 0)),
                  pl.BlockSpec((1, 3, d), lambda b, i: (jnp.where(i == 0, nb, b), 0, 0))],
        out_specs=[rows, rows],
        out_shape=[jax.ShapeDtypeStruct((nb * tps * ROW_TILE, d), F32),
                   jax.ShapeDtypeStruct((nb * tps * ROW_TILE, d), BF16)],
        compiler_params=_cparams("parallel", "arbitrary"),
        name="ln_modulate",
    )(ctx, x, mod)


IN_TILE = 1024
IN_ROWS = 1152
SUBLANE = 8


def _in_proj_plan():
    off, start = [], 0
    for size in IN_SPLITS:
        off.append(start)
        start += size
    (o_sz, o_xbc, o_dt, o_cq, o_ckv, o_kpe, o_mz, o_gq, o_gk, o_gv, o_gz, o_hy, o_hz, o_mg) = off
    wide = [(P_MERGE, o_mg, N_BRANCH * D_MODEL), (P_HY_X0, o_hy, 3 * HY_WIDTH), (P_HY_Z, o_hz, HY_WIDTH),
            (P_SSD_Z, o_sz, SSD_WIDTH), (P_SSD_X, o_xbc, SSD_WIDTH), (P_MLA_Z, o_mz, MLA_WIDTH),
            (P_GQA_Q, o_gq, GQA_WIDTH), (P_GQA_Z, o_gz, GQA_WIDTH)]
    n_tiles = -(-P_WIDTH // IN_TILE)
    src = [None] * n_tiles
    for p_off, s_off, width in wide:
        for t in range(width // IN_TILE):
            src[p_off // IN_TILE + t] = s_off + t * IN_TILE
    tail_tiles = [j for j in range(n_tiles) if src[j] is None]
    assert tail_tiles[0] * IN_TILE == P_MLA_CQ
    tail_src = [(o_cq, MLA_Q_LORA), (o_xbc + SSD_WIDTH, 2 * SSD_GROUPS * SSD_STATE), (o_ckv, MLA_KV_LORA),
                (o_gk, 2 * GQA_KV_WIDTH), (o_kpe, MLA_ROPE), (o_dt, 2 * SSD_HEADS)]
    return src, tail_tiles, tail_src


def _in_proj_kernel(start_ref, tail_idx_ref, a_ref, win_ref, tail_ref, o_ref, wb):
    @pl.when(pl.program_id(1) == 0)
    def _():
        from_tail = tail_idx_ref[pl.program_id(0)] >= 0

        @pl.when(jnp.logical_not(from_tail))
        def _():
            wb[...] = win_ref[0].astype(wb.dtype)

        @pl.when(from_tail)
        def _():
            wb[...] = tail_ref[...].astype(wb.dtype)

    o_ref[...] = _dot_nt(a_ref[...], wb[...]).astype(o_ref.dtype)


def _in_proj(h, w_in, layer):
    tt, d = h.shape
    w_t = jnp.swapaxes(w_in, 1, 2)
    src, tail_tiles, tail_src = _in_proj_plan()
    tail = jnp.concatenate([w_t[layer, s:s + w, :] for s, w in tail_src], axis=0)
    tail = jnp.pad(tail, ((0, len(tail_tiles) * IN_TILE - tail.shape[0]), (0, 0)))
    assert all(s is None or s % SUBLANE == 0 for s in src)
    rows = math.gcd(tt, IN_ROWS)
    starts = [0 if s is None else s // SUBLANE for s in src]
    tidx = [tail_tiles.index(j) if s is None else -1 for j, s in enumerate(src)]
    as_i32 = lambda v: jnp.asarray(v, jnp.int32)
    gs = pltpu.PrefetchScalarGridSpec(
        num_scalar_prefetch=2, grid=(len(src), tt // rows),
        in_specs=[pl.BlockSpec((rows, d), lambda j, i, st, ti: (i, 0)),
                  pl.BlockSpec((pl.Element(1), pl.Element(IN_TILE), pl.Element(d)),
                               lambda j, i, st, ti: (layer, st[j] * SUBLANE, 0)),
                  pl.BlockSpec((IN_TILE, d), lambda j, i, st, ti: (jnp.maximum(ti[j], 0), 0),
                               pipeline_mode=pl.Buffered(1))],
        out_specs=pl.BlockSpec((rows, IN_TILE), lambda j, i, st, ti: (i, j)),
        scratch_shapes=[pltpu.VMEM((IN_TILE, d), BF16)])
    return pl.pallas_call(
        _in_proj_kernel, grid_spec=gs,
        out_shape=jax.ShapeDtypeStruct((tt, P_WIDTH), BF16),
        compiler_params=_cparams("parallel", "arbitrary"),
        name="in_proj",
    )(as_i32(starts), as_i32(tidx), h, w_t, tail)


def _conv_chunk(src_ref, w, bias, c, n_ctx_chunks, n_chunks):
    tb = n_chunks * CHUNK
    r0 = pl.multiple_of(c * CHUNK, CHUNK)
    cur = src_ref[pl.ds(r0, CHUNK), :].astype(F32)
    rp = pl.multiple_of(jnp.maximum(r0 - 16, 0), 16)
    rn = pl.multiple_of(jnp.minimum(r0 + CHUNK, tb - 16), 16)
    prev_row = src_ref[pl.ds(rp, 16), :].astype(F32)[15:16]
    next_row = src_ref[pl.ds(rn, 16), :].astype(F32)[0:1]
    has_prev = jnp.logical_and(c != 0, c != n_ctx_chunks)
    has_next = jnp.logical_and(c != n_ctx_chunks - 1, c != n_chunks - 1)
    prev_row = jnp.where(has_prev, prev_row, 0.0)
    next_row = jnp.where(has_next, next_row, 0.0)
    row = lax.broadcasted_iota(jnp.int32, cur.shape, 0)
    sp = jnp.where(row == 0, prev_row, pltpu.roll(cur, 1, 0))
    sn = jnp.where(row == CHUNK - 1, next_row, pltpu.roll(cur, CHUNK - 1, 0))
    return sp * w[0:1] + cur * w[1:2] + sn * w[2:3] + bias


def _ssd_kernel(x_ref, b_ref, c_ref, sm_ref, z_ref, cwx_ref, cbx_ref, cwb_ref, cbb_ref, cwc_ref, cbc_ref,
                dtb_ref, alog_ref, dsk_ref, nw_ref, o_ref, ux, ub, uc, yacc, ybwd, st, st_b,
                *, n_ctx_chunks, n_chunks):
    g = pl.program_id(1)
    hp = ux.shape[1]
    nh = hp // SSD_HEAD_DIM

    def conv_body(c, carry):
        r0 = pl.multiple_of(c * CHUNK, CHUNK)
        ux[pl.ds(r0, CHUNK), :] = _silu(_conv_chunk(x_ref, cwx_ref[...], cbx_ref[...], c, n_ctx_chunks,
                                                    n_chunks)).astype(ux.dtype)
        ub[pl.ds(r0, CHUNK), :] = _silu(_conv_chunk(b_ref, cwb_ref[...], cbb_ref[...], c, n_ctx_chunks,
                                                    n_chunks)).astype(ub.dtype)
        uc[pl.ds(r0, CHUNK), :] = _silu(_conv_chunk(c_ref, cwc_ref[...], cbc_ref[...], c, n_ctx_chunks,
                                                    n_chunks)).astype(uc.dtype)
        return carry

    lax.fori_loop(0, n_chunks, conv_body, 0)

    ri = lax.broadcasted_iota(jnp.int32, (CHUNK, CHUNK), 0)
    ci = lax.broadcasted_iota(jnp.int32, (CHUNK, CHUNK), 1)
    lane_hp = lax.broadcasted_iota(jnp.int32, (CHUNK, hp), 1)
    lane128 = lax.broadcasted_iota(jnp.int32, (CHUNK, LANE), 1)
    a_all = -jnp.exp(alog_ref[0]) * math.log2(math.e)
    dtb = dtb_ref[0]

    def direction_consts(dirn):
        tri = (ci <= ri) if dirn == 0 else (ci >= ri)
        tri_bf = jnp.where(tri, 1.0, 0.0).astype(BF16)
        expand = jnp.where(lax.broadcasted_iota(jnp.int32, (LANE, hp), 0)
                           == dirn * 16 + (lax.broadcasted_iota(jnp.int32, (LANE, hp), 1) >> 6),
                           1.0, 0.0).astype(BF16)
        return tri, tri_bf, expand

    consts = (direction_consts(0), direction_consts(1))

    def chunk_step(dirn, c, st):
        tri, tri_bf, expand = consts[dirn]
        last = CHUNK - 1 if dirn == 0 else 0
        if True:
            r0 = pl.multiple_of(c * CHUNK, CHUNK)
            xs = ux[pl.ds(r0, CHUNK), :].astype(F32)
            bc = ub[pl.ds(r0, CHUNK), :]
            cc = uc[pl.ds(r0, CHUNK), :]
            sm = sm_ref[pl.ds(r0, CHUNK), :].astype(F32)
            sm = jnp.where(g == 0, pltpu.roll(sm, LANE - SMALL_DT, 1), pltpu.roll(sm, LANE - SMALL_DT - nh, 1))
            dt = _softplus(sm + dtb)
            la = dt * a_all
            cum = _dot_exact_rhs_lhs(tri_bf, la)
            cx = _dot_exact_rhs(cum, expand, 2)
            dtx = _dot_exact_rhs(dt, expand, 2)
            cx_last = cx[last:last + 1]
            ecum = jnp.exp2(cx)
            dec_end = jnp.exp2(cx_last - cx)
            xd = xs * dtx
            xd_bf = xd.astype(BF16)
            gmat = _dot_nt(cc, bc)
            cum_t = cum.T
            cols = []
            for j in range(nh // 2):
                sc = []
                for h in (2 * j, 2 * j + 1):
                    k = dirn * 16 + h
                    diff = cum[:, k:k + 1] - cum_t[k:k + 1, :]
                    lmat = jnp.exp2(jnp.where(tri, diff, NEG_INF))
                    sc.append((gmat * lmat).astype(BF16))
                xcol = xd_bf[:, j * LANE:(j + 1) * LANE]
                lo = lane128 < SSD_HEAD_DIM
                rhs = jnp.concatenate([jnp.where(lo, xcol, jnp.zeros_like(xcol)),
                                       jnp.where(lo, jnp.zeros_like(xcol), xcol)], axis=0)
                cols.append(_dot(jnp.concatenate(sc, axis=1), rhs))
            y = jnp.concatenate(cols, axis=1)
            s_prev = st[...]
            y = y + _dot(cc, s_prev.astype(BF16)) * ecum
            st[...] = jnp.exp2(cx_last) * s_prev + _dot_tn(bc, (xd * dec_end).astype(BF16))
            return y, xs, r0

    st[...] = jnp.zeros_like(st)
    st_b[...] = jnp.zeros_like(st_b)

    def scan_body(i, carry):
        y_f, xs_f, r0_f = chunk_step(0, i, st)
        yacc[pl.ds(r0_f, CHUNK), :] = y_f + xs_f * dsk_ref[...]
        c_b = jnp.where(i < n_ctx_chunks, n_ctx_chunks - 1 - i, n_chunks - 1 + n_ctx_chunks - i)
        y_b, _, r0_b = chunk_step(1, c_b, st_b)
        ybwd[pl.ds(r0_b, CHUNK), :] = y_b
        return carry

    lax.fori_loop(0, n_chunks, scan_body, 0)

    def gate_body(c, carry):
        r0 = pl.multiple_of(c * CHUNK, CHUNK)
        tot = (yacc[pl.ds(r0, CHUNK), :] + ybwd[pl.ds(r0, CHUNK), :]) * _silu(z_ref[pl.ds(r0, CHUNK), :].astype(F32))
        ms = jnp.mean(tot * tot, axis=-1, keepdims=True)
        o_ref[pl.ds(r0, CHUNK), :] = (tot * lax.rsqrt(ms + LN_EPS) * nw_ref[...]).astype(o_ref.dtype)
        return carry

    lax.fori_loop(0, n_chunks, gate_body, 0)


def _dot_exact_rhs_lhs(tri_bf, v):
    v1, v2, v3 = _split3(v)
    return _dot(tri_bf, v1) + (_dot(tri_bf, v2) + _dot(tri_bf, v3))


def _ssd(p3, conv_w, conv_b, dt_bias, a_log, d_skip, norm_w, n_ctx):
    nb, tb, _ = p3.shape
    hp = SSD_WIDTH // SSD_GROUPS
    n_chunks = tb // CHUNK
    gn = SSD_GROUPS * SSD_STATE
    cwx, cwb, cwc = conv_w[:, :SSD_WIDTH], conv_w[:, SSD_WIDTH:SSD_WIDTH + gn], conv_w[:, SSD_WIDTH + gn:]
    cb = conv_b[None, :]
    cbx, cbb, cbc = cb[:, :SSD_WIDTH], cb[:, SSD_WIDTH:SSD_WIDTH + gn], cb[:, SSD_WIDTH + gn:]
    hg = SSD_HEADS // SSD_GROUPS

    def per_group(v):
        v = v.reshape(2, SSD_GROUPS, hg).transpose(1, 0, 2)
        v = jnp.pad(v, ((0, 0), (0, 0), (0, 16 - hg))).reshape(SSD_GROUPS, 1, 32)
        return jnp.pad(v, ((0, 0), (0, 0), (0, LANE - 32)))

    dsk = jnp.repeat(d_skip, SSD_HEAD_DIM)[None, :]
    kern = functools.partial(_ssd_kernel, n_ctx_chunks=n_ctx // CHUNK, n_chunks=n_chunks)
    col = lambda off, w: (lambda b, g: (b, 0, off // w + g))
    par = lambda b, g: (0, g)
    return pl.pallas_call(
        kern,
        grid=(nb, SSD_GROUPS),
        in_specs=[pl.BlockSpec((None, tb, hp), col(P_SSD_X, hp)),
                  pl.BlockSpec((None, tb, SSD_STATE), col(P_SSD_B, SSD_STATE)),
                  pl.BlockSpec((None, tb, SSD_STATE), col(P_SSD_C, SSD_STATE)),
                  pl.BlockSpec((None, tb, LANE), lambda b, g: (b, 0, P_SMALL // LANE)),
                  pl.BlockSpec((None, tb, hp), col(P_SSD_Z, hp)),
                  pl.BlockSpec((3, hp), par), pl.BlockSpec((1, hp), par),
                  pl.BlockSpec((3, SSD_STATE), par), pl.BlockSpec((1, SSD_STATE), par),
                  pl.BlockSpec((3, SSD_STATE), par), pl.BlockSpec((1, SSD_STATE), par),
                  pl.BlockSpec((1, 1, LANE), lambda b, g: (g, 0, 0)),
                  pl.BlockSpec((1, 1, LANE), lambda b, g: (g, 0, 0)),
                  pl.BlockSpec((1, hp), par), pl.BlockSpec((1, hp), par)],
        out_specs=pl.BlockSpec((None, tb, hp), lambda b, g: (b, 0, g)),
        out_shape=jax.ShapeDtypeStruct((nb, tb, SSD_WIDTH), BF16),
        scratch_shapes=[pltpu.VMEM((tb, hp), BF16), pltpu.VMEM((tb, SSD_STATE), BF16),
                        pltpu.VMEM((tb, SSD_STATE), BF16), pltpu.VMEM((tb, hp), F32),
                        pltpu.VMEM((tb, hp), F32), pltpu.VMEM((SSD_STATE, hp), F32),
                        pltpu.VMEM((SSD_STATE, hp), F32)],
        compiler_params=_cparams("parallel", "parallel"),
        name="ssd_bidir",
    )(p3, p3, p3, p3, p3, cwx, cbx, cwb, cbb, cwc, cbc, per_group(dt_bias), per_group(a_log), dsk,
      norm_w[None, :])


def _rope_tables(n_ctx, n_lat):
    rows = n_lat // GRID_W
    row = jnp.repeat(jnp.arange(rows, dtype=F32), GRID_W)
    colp = jnp.tile(jnp.arange(GRID_W, dtype=F32), rows)
    n_freq = ROPE_DIM // 4
    inv_freq = ROPE_BASE ** (-jnp.arange(n_freq, dtype=F32) / n_freq)
    ang = jnp.concatenate([row[:, None] * inv_freq[None, :], colp[:, None] * inv_freq[None, :]], axis=-1)
    cos, sin = jnp.cos(ang), jnp.sin(ang)
    cos2 = jnp.tile(cos, (1, 4))
    sin2 = jnp.tile(jnp.concatenate([-sin, sin], axis=-1), (1, 2))
    cos2 = jnp.concatenate([jnp.ones((n_ctx, LANE), F32), cos2], axis=0)
    sin2 = jnp.concatenate([jnp.zeros((n_ctx, LANE), F32), sin2], axis=0)
    return cos2, sin2


def _attn_prep_kernel(cq_ref, ckv_ref, sm_ref, gk_ref, gv_ref, cos_ref, sin_ref, qn_ref, wq_ref, kn_ref,
                      wk_ref, wv_ref, q_ref, k_ref, v_ref, kg_ref, vg_ref):
    cos, sin = cos_ref[...], sin_ref[...]
    lane = lax.broadcasted_iota(jnp.int32, cos.shape, 1)
    scale = (MLA_NOPE + MLA_ROPE) ** -0.5 * math.log2(math.e)
    cq = cq_ref[...].astype(F32)
    cqn = cq * lax.rsqrt(jnp.mean(cq * cq, axis=-1, keepdims=True) + LN_EPS) * qn_ref[...]
    q = _dot(cqn.astype(BF16), wq_ref[...])
    ckv = ckv_ref[...].astype(F32)
    ckvn = (ckv * lax.rsqrt(jnp.mean(ckv * ckv, axis=-1, keepdims=True) + LN_EPS) * kn_ref[...]).astype(BF16)
    kn = _dot(ckvn, wk_ref[...])
    v_ref[...] = _dot(ckvn, wv_ref[...]).astype(v_ref.dtype)
    kpe = jnp.where(lane < MLA_ROPE, _rope(sm_ref[...].astype(F32), cos, sin), 0.0).astype(k_ref.dtype)
    for h in range(MLA_HEADS):
        o = 2 * h * LANE
        q_ref[:, o:o + LANE] = (q[:, o:o + LANE] * scale).astype(q_ref.dtype)
        q_ref[:, o + LANE:o + 2 * LANE] = (_rope(q[:, o + LANE:o + 2 * LANE], cos, sin) * scale).astype(q_ref.dtype)
        k_ref[:, o:o + LANE] = kn[:, h * LANE:(h + 1) * LANE].astype(k_ref.dtype)
        k_ref[:, o + LANE:o + 2 * LANE] = kpe
    kg_ref[...] = _rope(gk_ref[...].astype(F32), cos, sin).astype(kg_ref.dtype)
    vg_ref[...] = gv_ref[...].astype(F32).T.astype(vg_ref.dtype)


def _attn_prep(p, cos2, sin2, q_norm, w_uq, kv_norm, w_ukv, tps):
    tt = p.shape[0]
    hq = MLA_NOPE + MLA_ROPE
    wq = w_uq.reshape(MLA_Q_LORA, MLA_HEADS, hq)
    wq = jnp.pad(wq, ((0, 0), (0, 0), (0, 2 * LANE - hq))).reshape(MLA_Q_LORA, MLA_HEADS * 2 * LANE).astype(BF16)
    wkv = w_ukv.reshape(MLA_KV_LORA, MLA_HEADS, MLA_NOPE + MLA_V)
    wk = wkv[:, :, :MLA_NOPE].reshape(MLA_KV_LORA, MLA_HEADS * MLA_NOPE).astype(BF16)
    wv = wkv[:, :, MLA_NOPE:].reshape(MLA_KV_LORA, MLA_HEADS * MLA_V).astype(BF16)
    full = lambda shape: pl.BlockSpec(shape, lambda i: (0, 0))
    pcol = lambda off, w: pl.BlockSpec((ROW_TILE, w), lambda i: (i, off // w))
    rows = lambda w: pl.BlockSpec((ROW_TILE, w), lambda i: (i, 0))
    tab = pl.BlockSpec((ROW_TILE, LANE), lambda i: (i % tps, 0))
    return pl.pallas_call(
        _attn_prep_kernel,
        grid=(tt // ROW_TILE,),
        in_specs=[pcol(P_MLA_CQ, MLA_Q_LORA), pcol(P_MLA_CKV, MLA_KV_LORA), pcol(P_SMALL, LANE),
                  pcol(P_GQA_K, LANE), pcol(P_GQA_V, LANE), tab, tab,
                  full((1, MLA_Q_LORA)), full(wq.shape), full((1, MLA_KV_LORA)), full(wk.shape), full(wv.shape)],
        out_specs=[rows(MLA_HEADS * 2 * LANE), rows(MLA_HEADS * 2 * LANE), rows(MLA_WIDTH),
                   rows(LANE), pl.BlockSpec((LANE, ROW_TILE), lambda i: (0, i))],
        out_shape=[jax.ShapeDtypeStruct((tt, MLA_HEADS * 2 * LANE), BF16),
                   jax.ShapeDtypeStruct((tt, MLA_HEADS * 2 * LANE), BF16),
                   jax.ShapeDtypeStruct((tt, MLA_WIDTH), BF16),
                   jax.ShapeDtypeStruct((tt, LANE), BF16),
                   jax.ShapeDtypeStruct((LANE, tt), BF16)],
        compiler_params=_cparams("parallel"),
        name="attn_prep",
    )(p, p, p, p, p, cos2, sin2, q_norm[None, :], wq, kv_norm[None, :], wk, wv)


MLA_Q_CHUNK = 256


def _mla_kernel(q_ref, k_ref, v_ref, z_ref, o_ref, *, n_ctx):
    tb = k_ref.shape[0]

    def attend(r0, rows, n_keys):
        s = _dot_nt(q_ref[r0:r0 + rows, :], k_ref[0:n_keys, :])
        m = jnp.max(s, axis=-1, keepdims=True)
        p = jnp.exp2(s - m)
        l = jnp.sum(p, axis=-1, keepdims=True)
        o = _dot(p.astype(BF16), v_ref[0:n_keys, :]) / l
        o_ref[r0:r0 + rows, :] = (o * _silu(z_ref[r0:r0 + rows, :].astype(F32))).astype(o_ref.dtype)

    attend(0, n_ctx, n_ctx)
    for r0 in range(n_ctx, tb, MLA_Q_CHUNK):
        attend(r0, MLA_Q_CHUNK, tb)


def _mla(qm3, km3, vm3, p3, n_ctx):
    nb, tb, _ = p3.shape
    assert (tb - n_ctx) % MLA_Q_CHUNK == 0
    blk = lambda w, off: pl.BlockSpec((None, tb, w), lambda b, h: (b, 0, off // w + h))
    return pl.pallas_call(
        functools.partial(_mla_kernel, n_ctx=n_ctx),
        grid=(nb, MLA_HEADS),
        in_specs=[blk(2 * LANE, 0), blk(2 * LANE, 0), blk(LANE, 0), blk(LANE, P_MLA_Z)],
        out_specs=blk(LANE, 0),
        out_shape=jax.ShapeDtypeStruct((nb, tb, MLA_WIDTH), BF16),
        compiler_params=_cparams("parallel", "parallel"),
        name="mla_attention",
    )(qm3, km3, vm3, p3)


GQA_HEAD_BATCH = 8

def _gqa_kernel(sink_ref, q_ref, kp_ref, ko_ref, kn_ref, kc_ref, vp_ref, vo_ref, vn_ref, vc_ref, cos_ref,
                sin_ref, z_ref, o_ref, *, n_ctx_blocks):
    blk = pl.program_id(1)
    n_blk = pl.num_programs(1)
    is_lat = blk >= n_ctx_blocks
    has_prev = blk > n_ctx_blocks
    has_next = jnp.logical_and(is_lat, blk < n_blk - 1)
    cos, sin = cos_ref[...], sin_ref[...]
    log2e = math.log2(math.e)
    scale = GQA_HEAD_DIM ** -0.5 * log2e
    w = q_ref.shape[0]
    hd = GQA_HEAD_DIM
    hpk = GQA_HEADS // GQA_KV_HEADS
    q = jnp.concatenate([_rope(q_ref[:, c * LANE:(c + 1) * LANE].astype(F32), cos, sin) * scale
                         for c in range(GQA_WIDTH // LANE)], axis=1)
    qt = q.T.astype(BF16)
    kall = jnp.concatenate([kp_ref[...], ko_ref[...], kn_ref[...], kc_ref[...]], axis=0)
    vt = jnp.concatenate([vp_ref[...], vo_ref[...], vn_ref[...], vc_ref[...]], axis=1)
    kj = lax.broadcasted_iota(jnp.int32, (w, w), 0)
    qi = lax.broadcasted_iota(jnp.int32, (w, w), 1)
    m_prev = jnp.logical_and(kj >= qi, has_prev)
    m_next = jnp.logical_and(kj <= qi, has_next)
    zeros = jnp.zeros((hd, w), BF16)
    nbat = GQA_HEAD_BATCH
    tile = lambda a: jnp.concatenate([a] * nbat, axis=1)
    m_prev, m_next = tile(m_prev), tile(m_next)
    outs = []
    for h0 in range(0, GQA_HEADS, nbat):
        kh = h0 // hpk
        heads = range(h0, h0 + nbat)
        rhs = jnp.concatenate(
            [jnp.concatenate([qt[h * hd:(h + 1) * hd], zeros] if kh == 0 else [zeros, qt[h * hd:(h + 1) * hd]],
                             axis=0) for h in heads], axis=1)
        st = _dot(kall, rhs)
        s_p = jnp.where(m_prev, st[0:w], NEG_INF)
        s_o = jnp.where(is_lat, st[w:2 * w], NEG_INF)
        s_n = jnp.where(m_next, st[2 * w:3 * w], NEG_INF)
        s = jnp.concatenate([s_p, s_o, s_n, st[3 * w:]], axis=0)
        sink = jnp.concatenate([jnp.full((1, w), sink_ref[h] * log2e, F32) for h in heads], axis=1)
        m = jnp.maximum(jnp.max(s, axis=0, keepdims=True), sink)
        pr = jnp.exp2(s - m)
        l = jnp.sum(pr, axis=0, keepdims=True) + jnp.exp2(sink - m)
        ot = _dot(vt[kh * hd:(kh + 1) * hd], pr.astype(BF16)) / l
        outs.extend(ot[:, i * w:(i + 1) * w] for i in range(nbat))
    y = jnp.concatenate(outs, axis=0).T
    o_ref[...] = (y * _silu(z_ref[...].astype(F32))).astype(o_ref.dtype)


def _gqa(p, kg, vgt, cos2, sin2, sink, nb, tb, n_ctx):
    w = WINDOW
    bps = tb // w
    row = lambda b, i: b * bps + i
    prev = lambda b, i: b * bps + jnp.maximum(i - 1, 0)
    nxt = lambda b, i: b * bps + jnp.minimum(i + 1, bps - 1)
    kspec = lambda f: pl.BlockSpec((w, LANE), lambda b, i: (f(b, i), 0))
    vspec = lambda f: pl.BlockSpec((LANE, w), lambda b, i: (0, f(b, i)))
    kctx = pl.BlockSpec((n_ctx, LANE), lambda b, i: (b * (tb // n_ctx), 0))
    vctx = pl.BlockSpec((LANE, n_ctx), lambda b, i: (0, b * (tb // n_ctx)))
    tab = pl.BlockSpec((w, LANE), lambda b, i: (i, 0))
    return pl.pallas_call(
        functools.partial(_gqa_kernel, n_ctx_blocks=n_ctx // w),
        grid=(nb, bps),
        in_specs=[pl.BlockSpec(memory_space=pltpu.SMEM),
                  pl.BlockSpec((w, GQA_WIDTH), lambda b, i: (row(b, i), P_GQA_Q // GQA_WIDTH)),
                  kspec(prev), kspec(row), kspec(nxt), kctx, vspec(prev), vspec(row), vspec(nxt), vctx, tab, tab,
                  pl.BlockSpec((w, GQA_WIDTH), lambda b, i: (row(b, i), P_GQA_Z // GQA_WIDTH))],
        out_specs=pl.BlockSpec((w, GQA_WIDTH), lambda b, i: (row(b, i), 0)),
        out_shape=jax.ShapeDtypeStruct((nb * tb, GQA_WIDTH), BF16),
        compiler_params=_cparams("parallel", "arbitrary"),
        name="gqa_window_attention",
    )(sink, p, kg, kg, kg, kg, vgt, vgt, vgt, vgt, cos2, sin2, p)


def _dft_tables(n):
    m = jnp.arange(n, dtype=jnp.int32)[None, :]

    def small(kvals):
        ang = ((kvals[:, None] * m) % (2 * n)).astype(F32) * (math.pi / n)
        return jnp.cos(ang), jnp.sin(ang)

    ca, sa = small(64 * jnp.arange(n // 64, dtype=jnp.int32))
    cb, sb = small(jnp.arange(64, dtype=jnp.int32))
    cos_t = ca[:, None, :] * cb[None, :, :] - sa[:, None, :] * sb[None, :, :]
    sin_t = sa[:, None, :] * cb[None, :, :] + ca[:, None, :] * sb[None, :, :]
    return cos_t.reshape(n, n).astype(BF16), sin_t.reshape(n, n).astype(BF16)


def _hyena_feats(n):
    t = jnp.linspace(0.0, 1.0, n, dtype=F32)[:, None]
    w_ang = (2.0 * math.pi / n) * jnp.arange(n, dtype=F32)[:, None]
    bands = jnp.linspace(1e-4, HY_BANDS - 1, HY_BANDS, dtype=F32)[None, :]
    feats = jnp.concatenate([t, jnp.cos(bands * w_ang), -jnp.sin(bands * w_ang)], axis=-1)
    return jnp.pad(feats, ((0, 0), (0, LANE - HY_POS_EMB)))


def _hyena_filter_kernel(feats_ref, w1_ref, b1_ref, w2_ref, b2_ref, w3f_ref, w3b_ref, freq_ref, dl_ref,
                         cos_ref, sin_ref, kr_ref, ks_ref, kq_ref, hs, hd, *, n):
    kt = pl.program_id(1)
    tk = cos_ref.shape[0]

    @pl.when(kt == 0)
    def _():
        feats = feats_ref[...]
        t = feats[:, 0:1]
        hdn = jnp.sin(freq_ref[0:1, :] * (_dot3(feats, w1_ref[...]) + b1_ref[...]))
        hdn = jnp.sin(freq_ref[1:2, :] * (_dot3(hdn, w2_ref[...]) + b2_ref[...]))
        decay = jnp.exp(-t * dl_ref[...])
        hf = _dot3(hdn, w3f_ref[...]) * decay
        hb = _dot3(hdn, w3b_ref[...]) * decay
        row = lax.broadcasted_iota(jnp.int32, hb.shape, 0)
        hb = jnp.where(row == 0, 0.0, hb)
        hsum = hf + hb
        hs[...] = hsum.astype(hs.dtype)
        hd[...] = (hf - hb).astype(hd.dtype)
        alt = jnp.where((row & 1) == 0, 1.0, -1.0)
        kq_ref[...] = jnp.sum(hsum * alt, axis=0, keepdims=True) * (1.0 / (2 * n))

    kidx = kt * tk + lax.broadcasted_iota(jnp.int32, (tk, 1), 0)
    wk = jnp.where(kidx == 0, 0.5 / n, 1.0 / n)
    kr_ref[...] = _dot(cos_ref[...], hs[...]) * wk
    ks_ref[...] = _dot(sin_ref[...], hd[...]) * wk


def _hyena_filter(n, cos_t, sin_t, w1, b1, w2, b2, w3, freq):
    cbw = 512
    tk = min(256, n)
    deltas = jnp.abs(jnp.linspace(math.log(HY_DECAY_TARGET) / HY_FAST_DECAY,
                                  math.log(HY_DECAY_TARGET) / HY_SLOW_DECAY, HY_WIDTH, dtype=F32))[None, :]
    w1p = jnp.pad(w1, ((0, LANE - HY_POS_EMB), (0, 0)))
    full = lambda a: pl.BlockSpec(a.shape, lambda c, k: (0,) * a.ndim)
    nblk = HY_WIDTH // cbw
    kern = functools.partial(_hyena_filter_kernel, n=n)
    feats = _hyena_feats(n)
    return pl.pallas_call(
        kern,
        grid=(nblk, n // tk),
        in_specs=[full(feats), full(w1p), pl.BlockSpec((1, HY_FILTER_HIDDEN), lambda c, k: (0, 0)),
                  full(w2), pl.BlockSpec((1, HY_FILTER_HIDDEN), lambda c, k: (0, 0)),
                  pl.BlockSpec((HY_FILTER_HIDDEN, cbw), lambda c, k: (0, c)),
                  pl.BlockSpec((HY_FILTER_HIDDEN, cbw), lambda c, k: (0, nblk + c)),
                  full(freq), pl.BlockSpec((1, cbw), lambda c, k: (0, c)),
                  pl.BlockSpec((tk, n), lambda c, k: (k, 0)), pl.BlockSpec((tk, n), lambda c, k: (k, 0))],
        out_specs=[pl.BlockSpec((tk, cbw), lambda c, k: (k, c)), pl.BlockSpec((tk, cbw), lambda c, k: (k, c)),
                   pl.BlockSpec((1, cbw), lambda c, k: (0, c))],
        out_shape=[jax.ShapeDtypeStruct((n, HY_WIDTH), F32), jax.ShapeDtypeStruct((n, HY_WIDTH), F32),
                   jax.ShapeDtypeStruct((1, HY_WIDTH), F32)],
        scratch_shapes=[pltpu.VMEM((n, cbw), BF16), pltpu.VMEM((n, cbw), BF16)],
        compiler_params=_cparams("parallel", "arbitrary"),
        name="hyena_filter",
    )(feats, w1p, b1[None, :], w2, b2[None, :], w3, w3, freq, deltas, cos_t, sin_t)


def _spectral_product(a, bq, kr, ks):
    return (a * kr - bq * ks).astype(BF16), (a * ks + bq * kr).astype(BF16)


HY_TILE = 512


def _hyena_kernel(x0_ref, x1_ref, v_ref, z_ref, cw0_ref, cb0_ref, cw1_ref, cb1_ref, cwv_ref, cbv_ref, d_ref,
                  cr_ref, sr_ref, kr_ref, ks_ref, kq_ref, ccx_ref, scx_ref, krx_ref, ksx_ref, kqx_ref,
                  o_ref, u, yr_s, ys_s, unyq, *, n_ctx, n_chunks, with_ctx):
    kt = pl.program_id(2)
    nk = pl.num_programs(2) // 2
    n_ctx_chunks = n_ctx // CHUNK
    tb = n_chunks * CHUNK
    tk = cr_ref.shape[0]

    def alt_sign(shape):
        return jnp.where((lax.broadcasted_iota(jnp.int32, shape, 0) & 1) == 0, 1.0, -1.0)

    def gate_chunk(c):
        x0c = _conv_chunk(x0_ref, cw0_ref[...], cb0_ref[...], c, n_ctx_chunks, n_chunks)
        r0 = pl.multiple_of(c * CHUNK, CHUNK)
        return x0c * _silu(z_ref[pl.ds(r0, CHUNK), :].astype(F32))

    @pl.when(kt == 0)
    def _():
        def conv_body(c, carry):
            r0 = pl.multiple_of(c * CHUNK, CHUNK)
            x1c = _conv_chunk(x1_ref, cw1_ref[...], cb1_ref[...], c, n_ctx_chunks, n_chunks)
            vc = _conv_chunk(v_ref, cwv_ref[...], cbv_ref[...], c, n_ctx_chunks, n_chunks)
            u[pl.ds(r0, CHUNK), :] = (x1c * vc).astype(u.dtype)
            return carry

        lax.fori_loop(0, n_chunks, conv_body, 0)
        ulf = u[n_ctx:tb, :].astype(F32)
        unyq[...] = jnp.sum(ulf * alt_sign(ulf.shape), axis=0, keepdims=True) * kq_ref[...]

    @pl.when(kt < nk)
    def _():
        ul = u[n_ctx:tb, :]
        yr, ys = _spectral_product(_dot(cr_ref[...], ul), _dot(sr_ref[...], ul), kr_ref[...], ks_ref[...])
        f0 = pl.multiple_of(kt * tk, tk)
        yr_s[pl.ds(f0, tk), :] = yr
        ys_s[pl.ds(f0, tk), :] = ys

    @pl.when(kt >= nk)
    def _():
        t = kt - nk
        y = _dot(cr_ref[...], yr_s[...]) + _dot(sr_ref[...], ys_s[...])
        for j in range(tk // CHUNK):
            c = n_ctx_chunks + t * (tk // CHUNK) + j
            r0 = pl.multiple_of(c * CHUNK, CHUNK)
            uf = u[pl.ds(r0, CHUNK), :].astype(F32)
            yj = y[j * CHUNK:(j + 1) * CHUNK] + alt_sign(uf.shape) * unyq[...] + uf * d_ref[...]
            o_ref[pl.ds(r0, CHUNK), :] = (yj * gate_chunk(c)).astype(o_ref.dtype)

    @pl.when(kt == 2 * nk - 1)
    def _():
        if with_ctx:
            uc = u[0:n_ctx, :]
            ucf = uc.astype(F32)
            alt = alt_sign(ucf.shape)
            yr, ys = _spectral_product(_dot(ccx_ref[...], uc), _dot(scx_ref[...], uc), krx_ref[...], ksx_ref[...])
            y = _dot(ccx_ref[...], yr) + _dot(scx_ref[...], ys)
            y = y + alt * (jnp.sum(ucf * alt, axis=0, keepdims=True) * kqx_ref[...]) + ucf * d_ref[...]
            for c in range(n_ctx_chunks):
                o_ref[c * CHUNK:(c + 1) * CHUNK, :] = (y[c * CHUNK:(c + 1) * CHUNK] * gate_chunk(c)).astype(o_ref.dtype)
        else:
            o_ref[0:n_ctx, :] = jnp.zeros((n_ctx, o_ref.shape[1]), o_ref.dtype)


def _hyena(p3, conv_w, conv_b, d_skip, tabs, filt, tabs_c, filt_c, n_ctx, with_ctx):
    nb, tb, _ = p3.shape
    n_lat = tb - n_ctx
    cbw = 512
    tk = HY_TILE
    nk = n_lat // tk
    cos_t, sin_t = tabs
    kr, ks, kq = filt
    cos_c, sin_c = tabs_c
    krc, ksc, kqc = filt_c
    cb = conv_b[None, :]
    one = pl.Buffered(1)
    pcol = lambda off: pl.BlockSpec((None, tb, cbw), lambda b, c, k: (b, 0, off // cbw + c), pipeline_mode=one)
    par = lambda rows, sec: pl.BlockSpec((rows, cbw), lambda b, c, k: (0, sec * (HY_WIDTH // cbw) + c))
    kern = functools.partial(_hyena_kernel, n_ctx=n_ctx, n_chunks=tb // CHUNK, with_ctx=with_ctx)
    small = lambda a: pl.BlockSpec(a.shape, lambda b, c, k: (0, 0))
    chan = lambda rows: pl.BlockSpec((rows, cbw), lambda b, c, k: (0, c))
    table = pl.BlockSpec((tk, n_lat), lambda b, c, k: (k % nk, 0))
    spec = pl.BlockSpec((tk, cbw), lambda b, c, k: (jnp.minimum(k, nk - 1), c))
    return pl.pallas_call(
        kern,
        grid=(nb, HY_WIDTH // cbw, 2 * nk),
        in_specs=[pcol(P_HY_X0), pcol(P_HY_X1), pcol(P_HY_V), pcol(P_HY_Z),
                  par(3, 0), par(1, 0), par(3, 1), par(1, 1), par(3, 2), par(1, 2), chan(1),
                  table, table, spec, spec, chan(1), small(cos_c), small(sin_c), chan(n_ctx), chan(n_ctx), chan(1)],
        out_specs=pl.BlockSpec((None, tb, cbw), lambda b, c, k: (b, 0, c)),
        out_shape=jax.ShapeDtypeStruct((nb, tb, HY_WIDTH), BF16),
        scratch_shapes=[pltpu.VMEM((tb, cbw), BF16), pltpu.VMEM((n_lat, cbw), BF16), pltpu.VMEM((n_lat, cbw), BF16),
                        pltpu.VMEM((1, cbw), F32)],
        compiler_params=_cparams("parallel", "parallel", "arbitrary"),
        name="hyena_dft_conv",
    )(p3, p3, p3, p3, conv_w, cb, conv_w, cb, conv_w, cb, d_skip[None, :], cos_t, sin_t,
      kr, ks, kq, cos_c, sin_c, krc, ksc, kqc)


def _mix_kernel(g0_ref, g1_ref, g2_ref, g3_ref, lg_ref, w_ref, o_ref, acc):
    k = pl.program_id(1)
    for kk, g_ref in enumerate((g0_ref, g1_ref, g2_ref, g3_ref)):
        @pl.when(k == kk)
        def _(g_ref=g_ref, kk=kk):
            gate = 0.5 * jnp.tanh(0.5 * lg_ref[...].astype(F32)) + 0.5
            contrib = gate * _dot(g_ref[...], w_ref[0])
            if kk == 0:
                acc[...] = contrib
            else:
                acc[...] += contrib

    @pl.when(k == N_BRANCH - 1)
    def _():
        o_ref[...] = acc[...].astype(o_ref.dtype)


def _mix(gated, p, w_branch_bf):
    tt = p.shape[0]
    tm = 512
    assert tt % tm == 0
    g_spec = pl.BlockSpec((tm, BRANCH_WIDTH), lambda i, k: (i, 0))
    return pl.pallas_call(
        _mix_kernel,
        grid=(tt // tm, N_BRANCH),
        in_specs=[g_spec, g_spec, g_spec, g_spec,
                  pl.BlockSpec((tm, D_MODEL), lambda i, k: (i, P_MERGE // D_MODEL + k)),
                  pl.BlockSpec((1, BRANCH_WIDTH, D_MODEL), lambda i, k: (k, 0, 0))],
        out_specs=pl.BlockSpec((tm, D_MODEL), lambda i, k: (i, 0)),
        out_shape=jax.ShapeDtypeStruct((tt, D_MODEL), BF16),
        scratch_shapes=[pltpu.VMEM((tm, D_MODEL), F32)],
        compiler_params=_cparams("parallel", "arbitrary"),
        name="branch_mix",
    )(*gated, p, w_branch_bf)


def _out_kernel(a_ref, w_ref, x_ref, mod_ref, g_ref, b_ref, *rest, with_next):
    mixed = _dot(a_ref[...], w_ref[...])
    r = DEEPNORM_ALPHA * x_ref[...] + mod_ref[0][2:3] * mixed
    mu = jnp.mean(r, axis=-1, keepdims=True)
    rc = r - mu
    var = jnp.mean(rc * rc, axis=-1, keepdims=True)
    xn = rc * lax.rsqrt(var + LN_EPS) * g_ref[...] + b_ref[...]
    if with_next:
        modn_ref, o_ref, h_ref = rest
        o_ref[...] = xn
        mu = jnp.mean(xn, axis=-1, keepdims=True)
        xc = xn - mu
        var = jnp.mean(xc * xc, axis=-1, keepdims=True)
        mn = modn_ref[0]
        h_ref[...] = (xc * lax.rsqrt(var + LN_EPS) * (1.0 + mn[1:2]) + mn[0:1]).astype(h_ref.dtype)
    else:
        (o_ref,) = rest
        o_ref[...] = xn


def _out_next(mixed, w_out_bf, xx, mod, ln_g, ln_b, mod_next, tps, nb):
    tt, d = xx.shape
    tile = pl.BlockSpec((ROW_TILE, d), lambda i: (i, 0))
    modspec = pl.BlockSpec((1, 3, d), lambda i: (_mod_row(i, tps, nb), 0, 0))
    vec = pl.BlockSpec((1, d), lambda i: (0, 0))
    return pl.pallas_call(
        functools.partial(_out_kernel, with_next=True),
        grid=(tt // ROW_TILE,),
        in_specs=[tile, pl.BlockSpec((d, d), lambda i: (0, 0)), tile, modspec, vec, vec, modspec],
        out_specs=[tile, tile],
        out_shape=[jax.ShapeDtypeStruct((tt, d), F32), jax.ShapeDtypeStruct((tt, d), BF16)],
        compiler_params=_cparams("parallel"),
        name="out_proj_norm_next",
    )(mixed, w_out_bf, xx, mod, ln_g[None, :], ln_b[None, :], mod_next)


def _out_last(mixed, w_out_bf, xx, mod, ln_g, ln_b, tps, nb, n_ctx):
    tt, d = xx.shape
    off = n_ctx // ROW_TILE
    tile = pl.BlockSpec((ROW_TILE, d), lambda b, j: (b * tps + off + j, 0))
    vec = pl.BlockSpec((1, d), lambda b, j: (0, 0))
    return pl.pallas_call(
        functools.partial(_out_kernel, with_next=False),
        grid=(nb, tps - off),
        in_specs=[tile, pl.BlockSpec((d, d), lambda b, j: (0, 0)), tile,
                  pl.BlockSpec((1, 3, d), lambda b, j: (b, 0, 0)), vec, vec],
        out_specs=pl.BlockSpec((None, ROW_TILE, d), lambda b, j: (b, j, 0)),
        out_shape=jax.ShapeDtypeStruct((nb, (tps - off) * ROW_TILE, d), F32),
        compiler_params=_cparams("parallel", "parallel"),
        name="out_proj_norm_last",
    )(mixed, w_out_bf, xx, mod, ln_g[None, :], ln_b[None, :])


def kernel(x, c, ctx, c_ctx, w_ada, b_ada, w_in, ssd_conv_w, ssd_conv_b, ssd_dt_bias, ssd_a_log, ssd_d,
           ssd_norm_w, mla_q_norm, mla_w_uq, mla_kv_norm, mla_w_ukv, gqa_sink, hy_conv_w, hy_conv_b,
           hy_w1, hy_b1, hy_w2, hy_b2, hy_w3, hy_freq, hy_d, w_branch, w_out, ln_g, ln_b):
    nb, n_lat, d = x.shape
    n_ctx = ctx.shape[1]
    tb = n_ctx + n_lat
    tt = nb * tb
    tps = tb // ROW_TILE
    assert d == D_MODEL and n_ctx == ROW_TILE and n_lat % 512 == 0 and n_lat % GRID_W == 0

    cos2, sin2 = _rope_tables(n_ctx, n_lat)
    tabs, tabs_c = _dft_tables(n_lat), _dft_tables(n_ctx)
    cvec = jax.nn.silu(jnp.concatenate([c, c_ctx[None]], axis=0))
    mods = [(_matmul(cvec, w_ada, i, F32, 8, 512, "adaln_mod") + b_ada[i]).reshape(nb + 1, 3, d)
            for i in range(DEPTH)]

    xx, h = _ln_mod(x, ctx, mods[0])
    out = None
    for i in range(DEPTH):
        last = i == DEPTH - 1
        p = _in_proj(h, w_in, i)
        p3 = p.reshape(nb, tb, P_WIDTH)
        g_ssd = _ssd(p3, ssd_conv_w[i], ssd_conv_b[i], ssd_dt_bias[i], ssd_a_log[i], ssd_d[i], ssd_norm_w[i],
                     n_ctx).reshape(tt, SSD_WIDTH)
        qm, km, vm, kg, vg = _attn_prep(p, cos2, sin2, mla_q_norm[i], mla_w_uq[i], mla_kv_norm[i], mla_w_ukv[i],
                                        tps)
        r3 = lambda a: a.reshape(nb, tb, a.shape[-1])
        g_mla = _mla(r3(qm), r3(km), r3(vm), p3, n_ctx).reshape(tt, MLA_WIDTH)
        g_gqa = _gqa(p, kg, vg, cos2, sin2, gqa_sink[i], nb, tb, n_ctx)
        fargs = (hy_w1[i], hy_b1[i], hy_w2[i], hy_b2[i], hy_w3[i], hy_freq[i])
        filt = _hyena_filter(n_lat, *tabs, *fargs)
        filt_c = _hyena_filter(n_ctx, *tabs_c, *fargs)
        g_hy = _hyena(p3, hy_conv_w[i], hy_conv_b[i], hy_d[i], tabs, filt, tabs_c, filt_c, n_ctx,
                      not last).reshape(tt, HY_WIDTH)
        mixed = _mix((g_ssd, g_mla, g_gqa, g_hy), p, w_branch[i].astype(BF16))
        w_out_bf = w_out[i].astype(BF16)
        if last:
            out = _out_last(mixed, w_out_bf, xx, mods[i], ln_g[i], ln_b[i], tps, nb, n_ctx)
        else:
            xx, h = _out_next(mixed, w_out_bf, xx, mods[i], ln_g[i], ln_b[i], mods[i + 1], tps, nb)
    return out
```

```python
import functools
import math

import jax
import jax.numpy as jnp
from jax import lax
from jax.experimental import pallas as pl
from jax.experimental.pallas import tpu as pltpu

D_MODEL = 2048
DEPTH = 2
GRID_W = 64
N_BRANCH = 4
BRANCH_WIDTH = D_MODEL // 2
ROPE_DIM = 64
ROPE_BASE = 10000.0
SSD_WIDTH = BRANCH_WIDTH
SSD_HEAD_DIM = 64
SSD_HEADS = SSD_WIDTH // SSD_HEAD_DIM
SSD_GROUPS = 2
SSD_STATE = 128
SSD_CHUNK = 128
SSD_CONV_CH = SSD_WIDTH + 2 * SSD_GROUPS * SSD_STATE
MLA_HEADS = 8
MLA_NOPE = 128
MLA_ROPE = ROPE_DIM
MLA_V = 128
MLA_Q_LORA = 512
MLA_KV_LORA = 256
MLA_WIDTH = MLA_HEADS * MLA_V
GQA_HEADS = 16
GQA_KV_HEADS = 2
GQA_HEAD_DIM = ROPE_DIM
GQA_WIDTH = GQA_HEADS * GQA_HEAD_DIM
GQA_KV_WIDTH = GQA_KV_HEADS * GQA_HEAD_DIM
WINDOW = 128
HY_WIDTH = BRANCH_WIDTH
HY_POS_EMB = 33
HY_BANDS = (HY_POS_EMB - 1) // 2
HY_FILTER_HIDDEN = 64
HY_FAST_DECAY = 0.3
HY_SLOW_DECAY = 1.5
HY_DECAY_TARGET = 0.01
LN_EPS = 1e-6
NEG_INF = -1e30
DEEPNORM_ALPHA = (2 * DEPTH) ** 0.25

IN_SPLITS = (
    SSD_WIDTH, SSD_CONV_CH, 2 * SSD_HEADS, MLA_Q_LORA, MLA_KV_LORA, MLA_ROPE, MLA_WIDTH, GQA_WIDTH,
    GQA_KV_WIDTH, GQA_KV_WIDTH, GQA_WIDTH, 3 * HY_WIDTH, HY_WIDTH, N_BRANCH * D_MODEL,
)

LANE = 128
ROW_TILE = 256
CHUNK = SSD_CHUNK
VMEM_LIMIT_BYTES = 56 * 1024 * 1024

P_MERGE = 0
P_HY_X0 = 8192
P_HY_X1 = 9216
P_HY_V = 10240
P_HY_Z = 11264
P_SSD_Z = 12288
P_SSD_X = 13312
P_MLA_Z = 14336
P_GQA_Q = 15360
P_GQA_Z = 16384
P_MLA_CQ = 17408
P_SSD_B = 17920
P_SSD_C = 18176
P_MLA_CKV = 18432
P_GQA_K = 18688
P_GQA_V = 18816
P_SMALL = 18944
P_WIDTH = 19072
SMALL_DT = 64

F32 = jnp.float32
BF16 = jnp.bfloat16


def _cparams(*sem):
    return pltpu.CompilerParams(dimension_semantics=sem, vmem_limit_bytes=VMEM_LIMIT_BYTES)


def _split3(v):
    h1 = v.astype(BF16)
    r1 = v - h1.astype(F32)
    h2 = r1.astype(BF16)
    h3 = (r1 - h2.astype(F32)).astype(BF16)
    return h1, h2, h3


def _dot(a, b):
    return jnp.dot(a, b, preferred_element_type=F32)


def _dot_nt(a, b):
    return lax.dot_general(a, b, (((1,), (1,)), ((), ())), preferred_element_type=F32)


def _dot_tn(a, b):
    return lax.dot_general(a, b, (((0,), (0,)), ((), ())), preferred_element_type=F32)


def _dot_exact_rhs(a, b_bf16, pieces):
    parts = _split3(a)[:pieces]
    out = _dot(parts[0], b_bf16)
    for p in parts[1:]:
        out = out + _dot(p, b_bf16)
    return out


def _dot3(a, b):
    a1, a2, _ = _split3(a)
    b1, b2, _ = _split3(b)
    return _dot(a1, b1) + (_dot(a1, b2) + _dot(a2, b1))


def _silu(v):
    return v * jax.nn.sigmoid(v)


def _softplus(v):
    return jnp.maximum(v, 0.0) + jnp.log(1.0 + jnp.exp(-jnp.abs(v)))


def _rotswap(t):
    lane = lax.broadcasted_iota(jnp.int32, t.shape, 1)
    return jnp.where((lane & 63) < 32, pltpu.roll(t, 96, 1), pltpu.roll(t, 32, 1))


def _rope(t, cos, sin_signed):
    return t * cos + _rotswap(t) * sin_signed


def _mm_kernel(a_ref, b_ref, o_ref):
    o_ref[...] = _dot(a_ref[...].astype(BF16), b_ref[...].astype(BF16)).astype(o_ref.dtype)


def _matmul(a, b, layer, out_dtype, tm, tn, name):
    m, k = a.shape
    n = b.shape[2]
    tm = min(tm, m)
    tn = min(tn, n)
    return pl.pallas_call(
        _mm_kernel,
        grid=(pl.cdiv(m, tm), pl.cdiv(n, tn)),
        in_specs=[pl.BlockSpec((tm, k), lambda i, j: (i, 0)),
                  pl.BlockSpec((None, k, tn), lambda i, j: (layer, 0, j))],
        out_specs=pl.BlockSpec((tm, tn), lambda i, j: (i, j)),
        out_shape=jax.ShapeDtypeStruct((m, n), out_dtype),
        compiler_params=_cparams("parallel", "arbitrary"),
        name=name,
    )(a, b)


def _mod_row(i, tiles_per_sample, n_samples):
    return jnp.where(i % tiles_per_sample == 0, n_samples, i // tiles_per_sample)


def _ln_mod_kernel(ctx_ref, x_ref, mod_ref, xx_ref, h_ref):
    def emit(x):
        xx_ref[...] = x
        mu = jnp.mean(x, axis=-1, keepdims=True)
        xc = x - mu
        var = jnp.mean(xc * xc, axis=-1, keepdims=True)
        m = mod_ref[0]
        h_ref[...] = (xc * lax.rsqrt(var + LN_EPS) * (1.0 + m[1:2]) + m[0:1]).astype(h_ref.dtype)

    @pl.when(pl.program_id(1) == 0)
    def _():
        emit(ctx_ref[...])

    @pl.when(pl.program_id(1) != 0)
    def _():
        emit(x_ref[...])


def _ln_mod(x, ctx, mod):
    nb, n_lat, d = x.shape
    assert ctx.shape[1] == ROW_TILE
    tps = 1 + n_lat // ROW_TILE
    rows = pl.BlockSpec((ROW_TILE, d), lambda b, i: (b * tps + i, 0))
    return pl.pallas_call(
        _ln_mod_kernel,
        grid=(nb, tps),
        in_specs=[pl.BlockSpec((None, ROW_TILE, d), lambda b, i: (b, 0, 0)),
                  pl.BlockSpec((None, ROW_TILE, d), lambda b, i: (b, jnp.maximum(i - 1, 0), 0)),
                  pl.BlockSpec((1, 3, d), lambda b, i: (jnp.where(i == 0, nb, b), 0, 0))],
        out_specs=[rows, rows],
        out_shape=[jax.ShapeDtypeStruct((nb * tps * ROW_TILE, d), F32),
                   jax.ShapeDtypeStruct((nb * tps * ROW_TILE, d), BF16)],
        compiler_params=_cparams("parallel", "arbitrary"),
        name="ln_modulate",
    )(ctx, x, mod)


IN_TILE = 1024
IN_ROWS = 1152
SUBLANE = 8


def _in_proj_plan():
    off, start = [], 0
    for size in IN_SPLITS:
        off.append(start)
        start += size
    (o_sz, o_xbc, o_dt, o_cq, o_ckv, o_kpe, o_mz, o_gq, o_gk, o_gv, o_gz, o_hy, o_hz, o_mg) = off
    wide = [(P_MERGE, o_mg, N_BRANCH * D_MODEL), (P_HY_X0, o_hy, 3 * HY_WIDTH), (P_HY_Z, o_hz, HY_WIDTH),
            (P_SSD_Z, o_sz, SSD_WIDTH), (P_SSD_X, o_xbc, SSD_WIDTH), (P_MLA_Z, o_mz, MLA_WIDTH),
            (P_GQA_Q, o_gq, GQA_WIDTH), (P_GQA_Z, o_gz, GQA_WIDTH)]
    n_tiles = -(-P_WIDTH // IN_TILE)
    src = [None] * n_tiles
    for p_off, s_off, width in wide:
        for t in range(width // IN_TILE):
            src[p_off // IN_TILE + t] = s_off + t * IN_TILE
    tail_tiles = [j for j in range(n_tiles) if src[j] is None]
    assert tail_tiles[0] * IN_TILE == P_MLA_CQ
    tail_src = [(o_cq, MLA_Q_LORA), (o_xbc + SSD_WIDTH, 2 * SSD_GROUPS * SSD_STATE), (o_ckv, MLA_KV_LORA),
                (o_gk, 2 * GQA_KV_WIDTH), (o_kpe, MLA_ROPE), (o_dt, 2 * SSD_HEADS)]
    return src, tail_tiles, tail_src


def _in_proj_kernel(start_ref, tail_idx_ref, a_ref, win_ref, tail_ref, o_ref, wb, *, n_ctx, tiles_per_sample):
    @pl.when(pl.program_id(1) == 0)
    def _():
        from_tail = tail_idx_ref[pl.program_id(0)] >= 0

        @pl.when(jnp.logical_not(from_tail))
        def _():
            wb[...] = win_ref[0].astype(wb.dtype)

        @pl.when(from_tail)
        def _():
            wb[...] = tail_ref[...].astype(wb.dtype)

    skip = jnp.logical_and(tail_idx_ref[pl.program_id(0)] == -2, pl.program_id(1) % tiles_per_sample == 0)

    @pl.when(skip)
    def _():
        o_ref[0:n_ctx, :] = jnp.zeros((n_ctx, o_ref.shape[1]), o_ref.dtype)
        o_ref[n_ctx:, :] = _dot_nt(a_ref[n_ctx:, :], wb[...]).astype(o_ref.dtype)

    @pl.when(jnp.logical_not(skip))
    def _():
        o_ref[...] = _dot_nt(a_ref[...], wb[...]).astype(o_ref.dtype)


def _in_proj(h, w_in, layer, tb, n_ctx, ctx_outputs):
    tt, d = h.shape
    w_t = jnp.swapaxes(w_in, 1, 2)
    src, tail_tiles, tail_src = _in_proj_plan()
    tail = jnp.concatenate([w_t[layer, s:s + w, :] for s, w in tail_src], axis=0)
    tail = jnp.pad(tail, ((0, len(tail_tiles) * IN_TILE - tail.shape[0]), (0, 0)))
    assert all(s is None or s % SUBLANE == 0 for s in src)
    rows = IN_ROWS if tb % IN_ROWS == 0 else tb // 2
    assert tb % rows == 0 and rows % (2 * SUBLANE) == 0 and rows > n_ctx
    starts = [0 if s is None else s // SUBLANE for s in src]
    wide_code = lambda j: -1 if ctx_outputs or j == P_SSD_X // IN_TILE else -2
    tidx = [tail_tiles.index(j) if s is None else wide_code(j) for j, s in enumerate(src)]
    as_i32 = lambda v: jnp.asarray(v, jnp.int32)
    gs = pltpu.PrefetchScalarGridSpec(
        num_scalar_prefetch=2, grid=(len(src), tt // rows),
        in_specs=[pl.BlockSpec((rows, d), lambda j, i, st, ti: (i, 0)),
                  pl.BlockSpec((pl.Element(1), pl.Element(IN_TILE), pl.Element(d)),
                               lambda j, i, st, ti: (layer, st[j] * SUBLANE, 0)),
                  pl.BlockSpec((IN_TILE, d), lambda j, i, st, ti: (jnp.maximum(ti[j], 0), 0),
                               pipeline_mode=pl.Buffered(1))],
        out_specs=pl.BlockSpec((rows, IN_TILE), lambda j, i, st, ti: (i, j)),
        scratch_shapes=[pltpu.VMEM((IN_TILE, d), BF16)])
    return pl.pallas_call(
        functools.partial(_in_proj_kernel, n_ctx=n_ctx, tiles_per_sample=tb // rows), grid_spec=gs,
        out_shape=jax.ShapeDtypeStruct((tt, P_WIDTH), BF16),
        compiler_params=_cparams("parallel", "arbitrary"),
        name="in_proj",
    )(as_i32(starts), as_i32(tidx), h, w_t, tail)


def _conv_chunk(src_ref, w, bias, c, n_ctx_chunks, n_chunks):
    tb = n_chunks * CHUNK
    r0 = pl.multiple_of(c * CHUNK, CHUNK)
    cur = src_ref[pl.ds(r0, CHUNK), :].astype(F32)
    rp = pl.multiple_of(jnp.maximum(r0 - 16, 0), 16)
    rn = pl.multiple_of(jnp.minimum(r0 + CHUNK, tb - 16), 16)
    prev_row = src_ref[pl.ds(rp, 16), :].astype(F32)[15:16]
    next_row = src_ref[pl.ds(rn, 16), :].astype(F32)[0:1]
    has_prev = jnp.logical_and(c != 0, c != n_ctx_chunks)
    has_next = jnp.logical_and(c != n_ctx_chunks - 1, c != n_chunks - 1)
    prev_row = jnp.where(has_prev, prev_row, 0.0)
    next_row = jnp.where(has_next, next_row, 0.0)
    row = lax.broadcasted_iota(jnp.int32, cur.shape, 0)
    sp = jnp.where(row == 0, prev_row, pltpu.roll(cur, 1, 0))
    sn = jnp.where(row == CHUNK - 1, next_row, pltpu.roll(cur, CHUNK - 1, 0))
    return sp * w[0:1] + cur * w[1:2] + sn * w[2:3] + bias


def _ssd_kernel(x_ref, b_ref, c_ref, sm_ref, z_ref, cwx_ref, cbx_ref, cwb_ref, cbb_ref, cwc_ref, cbc_ref,
                dtb_ref, alog_ref, dsk_ref, nw_ref, o_ref, ux, ub, uc, yacc, ybwd, st, st_b,
                *, n_ctx_chunks, n_chunks):
    g = pl.program_id(1)
    hp = ux.shape[1]
    nh = hp // SSD_HEAD_DIM

    def conv_body(c, carry):
        r0 = pl.multiple_of(c * CHUNK, CHUNK)
        ux[pl.ds(r0, CHUNK), :] = _silu(_conv_chunk(x_ref, cwx_ref[...], cbx_ref[...], c, n_ctx_chunks,
                                                    n_chunks)).astype(ux.dtype)
        ub[pl.ds(r0, CHUNK), :] = _silu(_conv_chunk(b_ref, cwb_ref[...], cbb_ref[...], c, n_ctx_chunks,
                                                    n_chunks)).astype(ub.dtype)
        uc[pl.ds(r0, CHUNK), :] = _silu(_conv_chunk(c_ref, cwc_ref[...], cbc_ref[...], c, n_ctx_chunks,
                                                    n_chunks)).astype(uc.dtype)
        return carry

    lax.fori_loop(0, n_chunks, conv_body, 0)

    ri = lax.broadcasted_iota(jnp.int32, (CHUNK, CHUNK), 0)
    ci = lax.broadcasted_iota(jnp.int32, (CHUNK, CHUNK), 1)
    lane_hp = lax.broadcasted_iota(jnp.int32, (CHUNK, hp), 1)
    lane128 = lax.broadcasted_iota(jnp.int32, (CHUNK, LANE), 1)
    a_all = -jnp.exp(alog_ref[0]) * math.log2(math.e)
    dtb = dtb_ref[0]

    def direction_consts(dirn):
        tri = (ci <= ri) if dirn == 0 else (ci >= ri)
        tri_bf = jnp.where(tri, 1.0, 0.0).astype(BF16)
        expand = jnp.where(lax.broadcasted_iota(jnp.int32, (LANE, hp), 0)
                           == dirn * 16 + (lax.broadcasted_iota(jnp.int32, (LANE, hp), 1) >> 6),
                           1.0, 0.0).astype(BF16)
        return tri, tri_bf, expand

    consts = (direction_consts(0), direction_consts(1))

    def chunk_step(dirn, c, st):
        tri, tri_bf, expand = consts[dirn]
        last = CHUNK - 1 if dirn == 0 else 0
        if True:
            r0 = pl.multiple_of(c * CHUNK, CHUNK)
            xs = ux[pl.ds(r0, CHUNK), :].astype(F32)
            bc = ub[pl.ds(r0, CHUNK), :]
            cc = uc[pl.ds(r0, CHUNK), :]
            sm = sm_ref[pl.ds(r0, CHUNK), :].astype(F32)
            sm = jnp.where(g == 0, pltpu.roll(sm, LANE - SMALL_DT, 1), pltpu.roll(sm, LANE - SMALL_DT - nh, 1))
            dt = _softplus(sm + dtb)
            la = dt * a_all
            cum = _dot_exact_rhs_lhs(tri_bf, la)
            cx = _dot_exact_rhs(cum, expand, 2)
            dtx = _dot_exact_rhs(dt, expand, 2)
            cx_last = cx[last:last + 1]
            ecum = jnp.exp2(cx)
            dec_end = jnp.exp2(cx_last - cx)
            xd = xs * dtx
            xd_bf = xd.astype(BF16)
            gmat = _dot_nt(cc, bc)
            cum_t = cum.T
            cols = []
            for j in range(nh // 2):
                sc = []
                for h in (2 * j, 2 * j + 1):
                    k = dirn * 16 + h
                    diff = cum[:, k:k + 1] - cum_t[k:k + 1, :]
                    lmat = jnp.exp2(jnp.where(tri, diff, NEG_INF))
                    sc.append((gmat * lmat).astype(BF16))
                xcol = xd_bf[:, j * LANE:(j + 1) * LANE]
                lo = lane128 < SSD_HEAD_DIM
                rhs = jnp.concatenate([jnp.where(lo, xcol, jnp.zeros_like(xcol)),
                                       jnp.where(lo, jnp.zeros_like(xcol), xcol)], axis=0)
                cols.append(_dot(jnp.concatenate(sc, axis=1), rhs))
            y = jnp.concatenate(cols, axis=1)
            s_prev = st[...]
            y = y + _dot(cc, s_prev.astype(BF16)) * ecum
            st[...] = jnp.exp2(cx_last) * s_prev + _dot_tn(bc, (xd * dec_end).astype(BF16))
            return y, xs, r0

    st[...] = jnp.zeros_like(st)
    st_b[...] = jnp.zeros_like(st_b)

    def scan_body(i, carry):
        y_f, xs_f, r0_f = chunk_step(0, i, st)
        yacc[pl.ds(r0_f, CHUNK), :] = y_f + xs_f * dsk_ref[...]
        c_b = jnp.where(i < n_ctx_chunks, n_ctx_chunks - 1 - i, n_chunks - 1 + n_ctx_chunks - i)
        y_b, _, r0_b = chunk_step(1, c_b, st_b)
        ybwd[pl.ds(r0_b, CHUNK), :] = y_b
        return carry

    lax.fori_loop(0, n_chunks, scan_body, 0)

    def gate_body(c, carry):
        r0 = pl.multiple_of(c * CHUNK, CHUNK)
        tot = (yacc[pl.ds(r0, CHUNK), :] + ybwd[pl.ds(r0, CHUNK), :]) * _silu(z_ref[pl.ds(r0, CHUNK), :].astype(F32))
        ms = jnp.mean(tot * tot, axis=-1, keepdims=True)
        o_ref[pl.ds(r0, CHUNK), :] = (tot * lax.rsqrt(ms + LN_EPS) * nw_ref[...]).astype(o_ref.dtype)
        return carry

    lax.fori_loop(0, n_chunks, gate_body, 0)


def _dot_exact_rhs_lhs(tri_bf, v):
    v1, v2, v3 = _split3(v)
    return _dot(tri_bf, v1) + (_dot(tri_bf, v2) + _dot(tri_bf, v3))


def _ssd(p3, conv_w, conv_b, dt_bias, a_log, d_skip, norm_w, n_ctx):
    nb, tb, _ = p3.shape
    hp = SSD_WIDTH // SSD_GROUPS
    n_chunks = tb // CHUNK
    gn = SSD_GROUPS * SSD_STATE
    cwx, cwb, cwc = conv_w[:, :SSD_WIDTH], conv_w[:, SSD_WIDTH:SSD_WIDTH + gn], conv_w[:, SSD_WIDTH + gn:]
    cb = conv_b[None, :]
    cbx, cbb, cbc = cb[:, :SSD_WIDTH], cb[:, SSD_WIDTH:SSD_WIDTH + gn], cb[:, SSD_WIDTH + gn:]
    hg = SSD_HEADS // SSD_GROUPS

    def per_group(v):
        v = v.reshape(2, SSD_GROUPS, hg).transpose(1, 0, 2)
        v = jnp.pad(v, ((0, 0), (0, 0), (0, 16 - hg))).reshape(SSD_GROUPS, 1, 32)
        return jnp.pad(v, ((0, 0), (0, 0), (0, LANE - 32)))

    dsk = jnp.repeat(d_skip, SSD_HEAD_DIM)[None, :]
    kern = functools.partial(_ssd_kernel, n_ctx_chunks=n_ctx // CHUNK, n_chunks=n_chunks)
    col = lambda off, w: (lambda b, g: (b, 0, off // w + g))
    par = lambda b, g: (0, g)
    return pl.pallas_call(
        kern,
        grid=(nb, SSD_GROUPS),
        in_specs=[pl.BlockSpec((None, tb, hp), col(P_SSD_X, hp)),
                  pl.BlockSpec((None, tb, SSD_STATE), col(P_SSD_B, SSD_STATE)),
                  pl.BlockSpec((None, tb, SSD_STATE), col(P_SSD_C, SSD_STATE)),
                  pl.BlockSpec((None, tb, LANE), lambda b, g: (b, 0, P_SMALL // LANE)),
                  pl.BlockSpec((None, tb, hp), col(P_SSD_Z, hp)),
                  pl.BlockSpec((3, hp), par), pl.BlockSpec((1, hp), par),
                  pl.BlockSpec((3, SSD_STATE), par), pl.BlockSpec((1, SSD_STATE), par),
                  pl.BlockSpec((3, SSD_STATE), par), pl.BlockSpec((1, SSD_STATE), par),
                  pl.BlockSpec((1, 1, LANE), lambda b, g: (g, 0, 0)),
                  pl.BlockSpec((1, 1, LANE), lambda b, g: (g, 0, 0)),
                  pl.BlockSpec((1, hp), par), pl.BlockSpec((1, hp), par)],
        out_specs=pl.BlockSpec((None, tb, hp), lambda b, g: (b, 0, g)),
        out_shape=jax.ShapeDtypeStruct((nb, tb, SSD_WIDTH), BF16),
        scratch_shapes=[pltpu.VMEM((tb, hp), BF16), pltpu.VMEM((tb, SSD_STATE), BF16),
                        pltpu.VMEM((tb, SSD_STATE), BF16), pltpu.VMEM((tb, hp), F32),
                        pltpu.VMEM((tb, hp), F32), pltpu.VMEM((SSD_STATE, hp), F32),
                        pltpu.VMEM((SSD_STATE, hp), F32)],
        compiler_params=_cparams("parallel", "parallel"),
        name="ssd_bidir",
    )(p3, p3, p3, p3, p3, cwx, cbx, cwb, cbb, cwc, cbc, per_group(dt_bias), per_group(a_log), dsk,
      norm_w[None, :])


def _rope_tables(n_ctx, n_lat):
    rows = n_lat // GRID_W
    row = jnp.repeat(jnp.arange(rows, dtype=F32), GRID_W)
    colp = jnp.tile(jnp.arange(GRID_W, dtype=F32), rows)
    n_freq = ROPE_DIM // 4
    inv_freq = ROPE_BASE ** (-jnp.arange(n_freq, dtype=F32) / n_freq)
    ang = jnp.concatenate([row[:, None] * inv_freq[None, :], colp[:, None] * inv_freq[None, :]], axis=-1)
    cos, sin = jnp.cos(ang), jnp.sin(ang)
    cos2 = jnp.tile(cos, (1, 4))
    sin2 = jnp.tile(jnp.concatenate([-sin, sin], axis=-1), (1, 2))
    cos2 = jnp.concatenate([jnp.ones((n_ctx, LANE), F32), cos2], axis=0)
    sin2 = jnp.concatenate([jnp.zeros((n_ctx, LANE), F32), sin2], axis=0)
    return cos2, sin2


def _attn_prep_kernel(cq_ref, ckv_ref, sm_ref, gk_ref, gv_ref, cos_ref, sin_ref, qn_ref, wq_ref, kn_ref,
                      wk_ref, wv_ref, q_ref, k_ref, v_ref, kg_ref, vg_ref):
    cos, sin = cos_ref[...], sin_ref[...]
    lane = lax.broadcasted_iota(jnp.int32, cos.shape, 1)
    scale = (MLA_NOPE + MLA_ROPE) ** -0.5 * math.log2(math.e)
    cq = cq_ref[...].astype(F32)
    cqn = cq * lax.rsqrt(jnp.mean(cq * cq, axis=-1, keepdims=True) + LN_EPS) * qn_ref[...]
    q = _dot(cqn.astype(BF16), wq_ref[...])
    ckv = ckv_ref[...].astype(F32)
    ckvn = (ckv * lax.rsqrt(jnp.mean(ckv * ckv, axis=-1, keepdims=True) + LN_EPS) * kn_ref[...]).astype(BF16)
    kn = _dot(ckvn, wk_ref[...])
    v_ref[...] = _dot(ckvn, wv_ref[...]).astype(v_ref.dtype)
    kpe = jnp.where(lane < MLA_ROPE, _rope(sm_ref[...].astype(F32), cos, sin), 0.0).astype(k_ref.dtype)
    for h in range(MLA_HEADS):
        o = 2 * h * LANE
        q_ref[:, o:o + LANE] = (q[:, o:o + LANE] * scale).astype(q_ref.dtype)
        q_ref[:, o + LANE:o + 2 * LANE] = (_rope(q[:, o + LANE:o + 2 * LANE], cos, sin) * scale).astype(q_ref.dtype)
        k_ref[:, o:o + LANE] = kn[:, h * LANE:(h + 1) * LANE].astype(k_ref.dtype)
        k_ref[:, o + LANE:o + 2 * LANE] = kpe
    kg_ref[...] = _rope(gk_ref[...].astype(F32), cos, sin).astype(kg_ref.dtype)
    vg_ref[...] = gv_ref[...].astype(F32).T.astype(vg_ref.dtype)


def _attn_prep(p, cos2, sin2, q_norm, w_uq, kv_norm, w_ukv, tps):
    tt = p.shape[0]
    hq = MLA_NOPE + MLA_ROPE
    wq = w_uq.reshape(MLA_Q_LORA, MLA_HEADS, hq)
    wq = jnp.pad(wq, ((0, 0), (0, 0), (0, 2 * LANE - hq))).reshape(MLA_Q_LORA, MLA_HEADS * 2 * LANE).astype(BF16)
    wkv = w_ukv.reshape(MLA_KV_LORA, MLA_HEADS, MLA_NOPE + MLA_V)
    wk = wkv[:, :, :MLA_NOPE].reshape(MLA_KV_LORA, MLA_HEADS * MLA_NOPE).astype(BF16)
    wv = wkv[:, :, MLA_NOPE:].reshape(MLA_KV_LORA, MLA_HEADS * MLA_V).astype(BF16)
    full = lambda shape: pl.BlockSpec(shape, lambda i: (0, 0))
    pcol = lambda off, w: pl.BlockSpec((ROW_TILE, w), lambda i: (i, off // w))
    rows = lambda w: pl.BlockSpec((ROW_TILE, w), lambda i: (i, 0))
    tab = pl.BlockSpec((ROW_TILE, LANE), lambda i: (i % tps, 0))
    return pl.pallas_call(
        _attn_prep_kernel,
        grid=(tt // ROW_TILE,),
        in_specs=[pcol(P_MLA_CQ, MLA_Q_LORA), pcol(P_MLA_CKV, MLA_KV_LORA), pcol(P_SMALL, LANE),
                  pcol(P_GQA_K, LANE), pcol(P_GQA_V, LANE), tab, tab,
                  full((1, MLA_Q_LORA)), full(wq.shape), full((1, MLA_KV_LORA)), full(wk.shape), full(wv.shape)],
        out_specs=[rows(MLA_HEADS * 2 * LANE), rows(MLA_HEADS * 2 * LANE), rows(MLA_WIDTH),
                   rows(LANE), pl.BlockSpec((LANE, ROW_TILE), lambda i: (0, i))],
        out_shape=[jax.ShapeDtypeStruct((tt, MLA_HEADS * 2 * LANE), BF16),
                   jax.ShapeDtypeStruct((tt, MLA_HEADS * 2 * LANE), BF16),
                   jax.ShapeDtypeStruct((tt, MLA_WIDTH), BF16),
                   jax.ShapeDtypeStruct((tt, LANE), BF16),
                   jax.ShapeDtypeStruct((LANE, tt), BF16)],
        compiler_params=_cparams("parallel"),
        name="attn_prep",
    )(p, p, p, p, p, cos2, sin2, q_norm[None, :], wq, kv_norm[None, :], wk, wv)


MLA_Q_CHUNK = 256


def _mla_kernel(q_ref, k_ref, v_ref, z_ref, o_ref, *, n_ctx):
    tb = k_ref.shape[0]
    ones_col = jnp.where(lax.broadcasted_iota(jnp.int32, (tb, LANE), 1) == 0, 1.0, 0.0).astype(BF16)
    v_ext = jnp.concatenate([v_ref[...], ones_col], axis=1)

    def attend(r0, rows, n_keys):
        s = _dot_nt(q_ref[r0:r0 + rows, :], k_ref[0:n_keys, :])
        m = jnp.max(s, axis=-1, keepdims=True)
        p = jnp.exp2(s - m).astype(BF16)
        o_ext = _dot(p, v_ext[0:n_keys])
        o = o_ext[:, :MLA_V] / o_ext[:, MLA_V:MLA_V + 1]
        o_ref[r0:r0 + rows, :] = (o * _silu(z_ref[r0:r0 + rows, :].astype(F32))).astype(o_ref.dtype)

    attend(0, n_ctx, n_ctx)
    for r0 in range(n_ctx, tb, MLA_Q_CHUNK):
        attend(r0, MLA_Q_CHUNK, tb)


def _mla(qm3, km3, vm3, p3, n_ctx):
    nb, tb, _ = p3.shape
    assert (tb - n_ctx) % MLA_Q_CHUNK == 0
    blk = lambda w, off: pl.BlockSpec((None, tb, w), lambda b, h: (b, 0, off // w + h))
    return pl.pallas_call(
        functools.partial(_mla_kernel, n_ctx=n_ctx),
        grid=(nb, MLA_HEADS),
        in_specs=[blk(2 * LANE, 0), blk(2 * LANE, 0), blk(LANE, 0), blk(LANE, P_MLA_Z)],
        out_specs=blk(LANE, 0),
        out_shape=jax.ShapeDtypeStruct((nb, tb, MLA_WIDTH), BF16),
        compiler_params=_cparams("parallel", "parallel"),
        name="mla_attention",
    )(qm3, km3, vm3, p3)


GQA_HEAD_BATCH = 8

def _gqa_kernel(sink_ref, q_ref, kp_ref, ko_ref, kn_ref, kc_ref, vp_ref, vo_ref, vn_ref, vc_ref, cos_ref,
                sin_ref, z_ref, o_ref, *, n_ctx_blocks):
    blk = pl.program_id(1)
    n_blk = pl.num_programs(1)
    is_lat = blk >= n_ctx_blocks
    has_prev = blk > n_ctx_blocks
    has_next = jnp.logical_and(is_lat, blk < n_blk - 1)
    cos, sin = cos_ref[...], sin_ref[...]
    log2e = math.log2(math.e)
    scale = GQA_HEAD_DIM ** -0.5 * log2e
    w = q_ref.shape[0]
    hd = GQA_HEAD_DIM
    hpk = GQA_HEADS // GQA_KV_HEADS
    q = jnp.concatenate([_rope(q_ref[:, c * LANE:(c + 1) * LANE].astype(F32), cos, sin) * scale
                         for c in range(GQA_WIDTH // LANE)], axis=1)
    qt = q.T.astype(BF16)
    kall = jnp.concatenate([kp_ref[...], ko_ref[...], kn_ref[...], kc_ref[...]], axis=0)
    vt = jnp.concatenate([vp_ref[...], vo_ref[...], vn_ref[...], vc_ref[...]], axis=1)
    kj = lax.broadcasted_iota(jnp.int32, (w, w), 0)
    qi = lax.broadcasted_iota(jnp.int32, (w, w), 1)
    m_prev = jnp.logical_and(kj >= qi, has_prev)
    m_next = jnp.logical_and(kj <= qi, has_next)
    zeros = jnp.zeros((hd, w), BF16)
    ones_rows = jnp.ones((16, kall.shape[0]), BF16)
    nbat = GQA_HEAD_BATCH
    tile = lambda a: jnp.concatenate([a] * nbat, axis=1)
    m_prev, m_next = tile(m_prev), tile(m_next)
    outs = []
    for h0 in range(0, GQA_HEADS, nbat):
        kh = h0 // hpk
        heads = range(h0, h0 + nbat)
        rhs = jnp.concatenate(
            [jnp.concatenate([qt[h * hd:(h + 1) * hd], zeros] if kh == 0 else [zeros, qt[h * hd:(h + 1) * hd]],
                             axis=0) for h in heads], axis=1)
        st = _dot(kall, rhs)
        s_p = jnp.where(m_prev, st[0:w], NEG_INF)
        s_o = jnp.where(is_lat, st[w:2 * w], NEG_INF)
        s_n = jnp.where(m_next, st[2 * w:3 * w], NEG_INF)
        s = jnp.concatenate([s_p, s_o, s_n, st[3 * w:]], axis=0)
        sink = jnp.concatenate([jnp.full((1, w), sink_ref[h] * log2e, F32) for h in heads], axis=1)
        m = jnp.maximum(jnp.max(s, axis=0, keepdims=True), sink)
        pr = jnp.exp2(s - m).astype(BF16)
        ot = _dot(jnp.concatenate([vt[kh * hd:(kh + 1) * hd], ones_rows], axis=0), pr)
        l = ot[hd:hd + 1] + jnp.exp2(sink - m)
        ot = ot[:hd] / l
        outs.extend(ot[:, i * w:(i + 1) * w] for i in range(nbat))
    y = jnp.concatenate(outs, axis=0).T
    o_ref[...] = (y * _silu(z_ref[...].astype(F32))).astype(o_ref.dtype)


def _gqa(p, kg, vgt, cos2, sin2, sink, nb, tb, n_ctx):
    w = WINDOW
    bps = tb // w
    row = lambda b, i: b * bps + i
    prev = lambda b, i: b * bps + jnp.maximum(i - 1, 0)
    nxt = lambda b, i: b * bps + jnp.minimum(i + 1, bps - 1)
    kspec = lambda f: pl.BlockSpec((w, LANE), lambda b, i: (f(b, i), 0))
    vspec = lambda f: pl.BlockSpec((LANE, w), lambda b, i: (0, f(b, i)))
    kctx = pl.BlockSpec((n_ctx, LANE), lambda b, i: (b * (tb // n_ctx), 0))
    vctx = pl.BlockSpec((LANE, n_ctx), lambda b, i: (0, b * (tb // n_ctx)))
    tab = pl.BlockSpec((w, LANE), lambda b, i: (i, 0))
    return pl.pallas_call(
        functools.partial(_gqa_kernel, n_ctx_blocks=n_ctx // w),
        grid=(nb, bps),
        in_specs=[pl.BlockSpec(memory_space=pltpu.SMEM),
                  pl.BlockSpec((w, GQA_WIDTH), lambda b, i: (row(b, i), P_GQA_Q // GQA_WIDTH)),
                  kspec(prev), kspec(row), kspec(nxt), kctx, vspec(prev), vspec(row), vspec(nxt), vctx, tab, tab,
                  pl.BlockSpec((w, GQA_WIDTH), lambda b, i: (row(b, i), P_GQA_Z // GQA_WIDTH))],
        out_specs=pl.BlockSpec((w, GQA_WIDTH), lambda b, i: (row(b, i), 0)),
        out_shape=jax.ShapeDtypeStruct((nb * tb, GQA_WIDTH), BF16),
        compiler_params=_cparams("parallel", "arbitrary"),
        name="gqa_window_attention",
    )(sink, p, kg, kg, kg, kg, vgt, vgt, vgt, vgt, cos2, sin2, p)


def _dft_tables(n):
    m = jnp.arange(n, dtype=jnp.int32)[None, :]

    def small(kvals):
        ang = ((kvals[:, None] * m) % (2 * n)).astype(F32) * (math.pi / n)
        return jnp.cos(ang), jnp.sin(ang)

    ca, sa = small(64 * jnp.arange(n // 64, dtype=jnp.int32))
    cb, sb = small(jnp.arange(64, dtype=jnp.int32))
    cos_t = ca[:, None, :] * cb[None, :, :] - sa[:, None, :] * sb[None, :, :]
    sin_t = sa[:, None, :] * cb[None, :, :] + ca[:, None, :] * sb[None, :, :]
    return cos_t.reshape(n, n).astype(BF16), sin_t.reshape(n, n).astype(BF16)


def _hyena_feats(n):
    t = jnp.linspace(0.0, 1.0, n, dtype=F32)[:, None]
    w_ang = (2.0 * math.pi / n) * jnp.arange(n, dtype=F32)[:, None]
    bands = jnp.linspace(1e-4, HY_BANDS - 1, HY_BANDS, dtype=F32)[None, :]
    feats = jnp.concatenate([t, jnp.cos(bands * w_ang), -jnp.sin(bands * w_ang)], axis=-1)
    return jnp.pad(feats, ((0, 0), (0, LANE - HY_POS_EMB)))


def _hyena_filter_kernel(feats_ref, w1_ref, b1_ref, w2_ref, b2_ref, w3f_ref, w3b_ref, freq_ref, dl_ref,
                         cos_ref, sin_ref, kr_ref, ks_ref, kq_ref, hs, hd, *, n):
    kt = pl.program_id(1)
    tk = cos_ref.shape[0]

    @pl.when(kt == 0)
    def _():
        feats = feats_ref[...]
        t = feats[:, 0:1]
        hdn = jnp.sin(freq_ref[0:1, :] * (_dot3(feats, w1_ref[...]) + b1_ref[...]))
        hdn = jnp.sin(freq_ref[1:2, :] * (_dot3(hdn, w2_ref[...]) + b2_ref[...]))
        decay = jnp.exp(-t * dl_ref[...])
        hf = _dot3(hdn, w3f_ref[...]) * decay
        hb = _dot3(hdn, w3b_ref[...]) * decay
        row = lax.broadcasted_iota(jnp.int32, hb.shape, 0)
        hb = jnp.where(row == 0, 0.0, hb)
        hsum = hf + hb
        hs[...] = hsum.astype(hs.dtype)
        hd[...] = (hf - hb).astype(hd.dtype)
        alt = jnp.where((row & 1) == 0, 1.0, -1.0)
        kq_ref[...] = jnp.sum(hsum * alt, axis=0, keepdims=True) * (1.0 / (2 * n))

    kidx = kt * tk + lax.broadcasted_iota(jnp.int32, (tk, 1), 0)
    wk = jnp.where(kidx == 0, 0.5 / n, 1.0 / n)
    kr_ref[...] = _dot(cos_ref[...], hs[...]) * wk
    ks_ref[...] = _dot(sin_ref[...], hd[...]) * wk


def _hyena_filter(n, cos_t, sin_t, w1, b1, w2, b2, w3, freq):
    cbw = 512
    tk = min(256, n)
    deltas = jnp.abs(jnp.linspace(math.log(HY_DECAY_TARGET) / HY_FAST_DECAY,
                                  math.log(HY_DECAY_TARGET) / HY_SLOW_DECAY, HY_WIDTH, dtype=F32))[None, :]
    w1p = jnp.pad(w1, ((0, LANE - HY_POS_EMB), (0, 0)))
    full = lambda a: pl.BlockSpec(a.shape, lambda c, k: (0,) * a.ndim)
    nblk = HY_WIDTH // cbw
    kern = functools.partial(_hyena_filter_kernel, n=n)
    feats = _hyena_feats(n)
    return pl.pallas_call(
        kern,
        grid=(nblk, n // tk),
        in_specs=[full(feats), full(w1p), pl.BlockSpec((1, HY_FILTER_HIDDEN), lambda c, k: (0, 0)),
                  full(w2), pl.BlockSpec((1, HY_FILTER_HIDDEN), lambda c, k: (0, 0)),
                  pl.BlockSpec((HY_FILTER_HIDDEN, cbw), lambda c, k: (0, c)),
                  pl.BlockSpec((HY_FILTER_HIDDEN, cbw), lambda c, k: (0, nblk + c)),
                  full(freq), pl.BlockSpec((1, cbw), lambda c, k: (0, c)),
                  pl.BlockSpec((tk, n), lambda c, k: (k, 0)), pl.BlockSpec((tk, n), lambda c, k: (k, 0))],
        out_specs=[pl.BlockSpec((tk, cbw), lambda c, k: (k, c)), pl.BlockSpec((tk, cbw), lambda c, k: (k, c)),
                   pl.BlockSpec((1, cbw), lambda c, k: (0, c))],
        out_shape=[jax.ShapeDtypeStruct((n, HY_WIDTH), F32), jax.ShapeDtypeStruct((n, HY_WIDTH), F32),
                   jax.ShapeDtypeStruct((1, HY_WIDTH), F32)],
        scratch_shapes=[pltpu.VMEM((n, cbw), BF16), pltpu.VMEM((n, cbw), BF16)],
        compiler_params=_cparams("parallel", "arbitrary"),
        name="hyena_filter",
    )(feats, w1p, b1[None, :], w2, b2[None, :], w3, w3, freq, deltas, cos_t, sin_t)


def _spectral_product(a, bq, kr, ks):
    return (a * kr - bq * ks).astype(BF16), (a * ks + bq * kr).astype(BF16)


HY_TILE = 512


def _hyena_kernel(x0_ref, x1_ref, v_ref, z_ref, cw0_ref, cb0_ref, cw1_ref, cb1_ref, cwv_ref, cbv_ref, d_ref,
                  cr_ref, sr_ref, kr_ref, ks_ref, kq_ref, ccx_ref, scx_ref, krx_ref, ksx_ref, kqx_ref,
                  o_ref, u, yr_s, ys_s, unyq, *, n_ctx, n_chunks, with_ctx):
    kt = pl.program_id(2)
    nk = pl.num_programs(2) // 2
    n_ctx_chunks = n_ctx // CHUNK
    tb = n_chunks * CHUNK
    tk = cr_ref.shape[0]

    def alt_sign(shape):
        return jnp.where((lax.broadcasted_iota(jnp.int32, shape, 0) & 1) == 0, 1.0, -1.0)

    def gate_chunk(c):
        x0c = _conv_chunk(x0_ref, cw0_ref[...], cb0_ref[...], c, n_ctx_chunks, n_chunks)
        r0 = pl.multiple_of(c * CHUNK, CHUNK)
        return x0c * _silu(z_ref[pl.ds(r0, CHUNK), :].astype(F32))

    @pl.when(kt == 0)
    def _():
        def conv_body(c, carry):
            r0 = pl.multiple_of(c * CHUNK, CHUNK)
            x1c = _conv_chunk(x1_ref, cw1_ref[...], cb1_ref[...], c, n_ctx_chunks, n_chunks)
            vc = _conv_chunk(v_ref, cwv_ref[...], cbv_ref[...], c, n_ctx_chunks, n_chunks)
            u[pl.ds(r0, CHUNK), :] = (x1c * vc).astype(u.dtype)
            return carry

        lax.fori_loop(0, n_chunks, conv_body, 0)
        ulf = u[n_ctx:tb, :].astype(F32)
        unyq[...] = jnp.sum(ulf * alt_sign(ulf.shape), axis=0, keepdims=True) * kq_ref[...]

    @pl.when(kt < nk)
    def _():
        ul = u[n_ctx:tb, :]
        yr, ys = _spectral_product(_dot(cr_ref[...], ul), _dot(sr_ref[...], ul), kr_ref[...], ks_ref[...])
        f0 = pl.multiple_of(kt * tk, tk)
        yr_s[pl.ds(f0, tk), :] = yr
        ys_s[pl.ds(f0, tk), :] = ys

    @pl.when(kt >= nk)
    def _():
        t = kt - nk
        y = _dot(cr_ref[...], yr_s[...]) + _dot(sr_ref[...], ys_s[...])
        for j in range(tk // CHUNK):
            c = n_ctx_chunks + t * (tk // CHUNK) + j
            r0 = pl.multiple_of(c * CHUNK, CHUNK)
            uf = u[pl.ds(r0, CHUNK), :].astype(F32)
            yj = y[j * CHUNK:(j + 1) * CHUNK] + alt_sign(uf.shape) * unyq[...] + uf * d_ref[...]
            o_ref[pl.ds(r0, CHUNK), :] = (yj * gate_chunk(c)).astype(o_ref.dtype)

    @pl.when(kt == 2 * nk - 1)
    def _():
        if with_ctx:
            uc = u[0:n_ctx, :]
            ucf = uc.astype(F32)
            alt = alt_sign(ucf.shape)
            yr, ys = _spectral_product(_dot(ccx_ref[...], uc), _dot(scx_ref[...], uc), krx_ref[...], ksx_ref[...])
            y = _dot(ccx_ref[...], yr) + _dot(scx_ref[...], ys)
            y = y + alt * (jnp.sum(ucf * alt, axis=0, keepdims=True) * kqx_ref[...]) + ucf * d_ref[...]
            for c in range(n_ctx_chunks):
                o_ref[c * CHUNK:(c + 1) * CHUNK, :] = (y[c * CHUNK:(c + 1) * CHUNK] * gate_chunk(c)).astype(o_ref.dtype)
        else:
            o_ref[0:n_ctx, :] = jnp.zeros((n_ctx, o_ref.shape[1]), o_ref.dtype)


def _hyena(p3, conv_w, conv_b, d_skip, tabs, filt, tabs_c, filt_c, n_ctx, with_ctx):
    nb, tb, _ = p3.shape
    n_lat = tb - n_ctx
    cbw = 512
    tk = HY_TILE
    nk = n_lat // tk
    cos_t, sin_t = tabs
    kr, ks, kq = filt
    cos_c, sin_c = tabs_c
    krc, ksc, kqc = filt_c
    cb = conv_b[None, :]
    pcol = lambda off: pl.BlockSpec((None, tb, cbw), lambda b, c, k: (b, 0, off // cbw + c))
    par = lambda rows, sec: pl.BlockSpec((rows, cbw), lambda b, c, k: (0, sec * (HY_WIDTH // cbw) + c))
    kern = functools.partial(_hyena_kernel, n_ctx=n_ctx, n_chunks=tb // CHUNK, with_ctx=with_ctx)
    small = lambda a: pl.BlockSpec(a.shape, lambda b, c, k: (0, 0))
    chan = lambda rows: pl.BlockSpec((rows, cbw), lambda b, c, k: (0, c))
    table = pl.BlockSpec((tk, n_lat), lambda b, c, k: (k % nk, 0))
    spec = pl.BlockSpec((tk, cbw), lambda b, c, k: (jnp.minimum(k, nk - 1), c))
    return pl.pallas_call(
        kern,
        grid=(nb, HY_WIDTH // cbw, 2 * nk),
        in_specs=[pcol(P_HY_X0), pcol(P_HY_X1), pcol(P_HY_V), pcol(P_HY_Z),
                  par(3, 0), par(1, 0), par(3, 1), par(1, 1), par(3, 2), par(1, 2), chan(1),
                  table, table, spec, spec, chan(1), small(cos_c), small(sin_c), chan(n_ctx), chan(n_ctx), chan(1)],
        out_specs=pl.BlockSpec((None, tb, cbw), lambda b, c, k: (b, 0, c)),
        out_shape=jax.ShapeDtypeStruct((nb, tb, HY_WIDTH), BF16),
        scratch_shapes=[pltpu.VMEM((tb, cbw), BF16), pltpu.VMEM((n_lat, cbw), BF16), pltpu.VMEM((n_lat, cbw), BF16),
                        pltpu.VMEM((1, cbw), F32)],
        compiler_params=_cparams("parallel", "parallel", "arbitrary"),
        name="hyena_dft_conv",
    )(p3, p3, p3, p3, conv_w, cb, conv_w, cb, conv_w, cb, d_skip[None, :], cos_t, sin_t,
      kr, ks, kq, cos_c, sin_c, krc, ksc, kqc)


def _mix_kernel(g0_ref, g1_ref, g2_ref, g3_ref, lg_ref, w_ref, o_ref, acc):
    k = pl.program_id(1)
    for kk, g_ref in enumerate((g0_ref, g1_ref, g2_ref, g3_ref)):
        @pl.when(k == kk)
        def _(g_ref=g_ref, kk=kk):
            gate = 0.5 * jnp.tanh(0.5 * lg_ref[...].astype(F32)) + 0.5
            contrib = gate * _dot(g_ref[...], w_ref[0])
            if kk == 0:
                acc[...] = contrib
            else:
                acc[...] += contrib

    @pl.when(k == N_BRANCH - 1)
    def _():
        o_ref[...] = acc[...].astype(o_ref.dtype)


def _mix(gated, p, w_branch_bf):
    tt = p.shape[0]
    tm = math.gcd(tt, 1024)
    g_spec = pl.BlockSpec((tm, BRANCH_WIDTH), lambda i, k: (i, 0))
    return pl.pallas_call(
        _mix_kernel,
        grid=(tt // tm, N_BRANCH),
        in_specs=[g_spec, g_spec, g_spec, g_spec,
                  pl.BlockSpec((tm, D_MODEL), lambda i, k: (i, P_MERGE // D_MODEL + k)),
                  pl.BlockSpec((1, BRANCH_WIDTH, D_MODEL), lambda i, k: (k, 0, 0))],
        out_specs=pl.BlockSpec((tm, D_MODEL), lambda i, k: (i, 0)),
        out_shape=jax.ShapeDtypeStruct((tt, D_MODEL), BF16),
        scratch_shapes=[pltpu.VMEM((tm, D_MODEL), F32)],
        compiler_params=_cparams("parallel", "arbitrary"),
        name="branch_mix",
    )(*gated, p, w_branch_bf)


def _out_kernel(a_ref, w_ref, x_ref, mod_ref, g_ref, b_ref, *rest, with_next):
    mixed = _dot(a_ref[...], w_ref[...])
    r = DEEPNORM_ALPHA * x_ref[...] + mod_ref[0][2:3] * mixed
    mu = jnp.mean(r, axis=-1, keepdims=True)
    rc = r - mu
    var = jnp.mean(rc * rc, axis=-1, keepdims=True)
    xn = rc * lax.rsqrt(var + LN_EPS) * g_ref[...] + b_ref[...]
    if with_next:
        modn_ref, o_ref, h_ref = rest
        o_ref[...] = xn
        mu = jnp.mean(xn, axis=-1, keepdims=True)
        xc = xn - mu
        var = jnp.mean(xc * xc, axis=-1, keepdims=True)
        mn = modn_ref[0]
        h_ref[...] = (xc * lax.rsqrt(var + LN_EPS) * (1.0 + mn[1:2]) + mn[0:1]).astype(h_ref.dtype)
    else:
        (o_ref,) = rest
        o_ref[...] = xn


def _out_next(mixed, w_out_bf, xx, mod, ln_g, ln_b, mod_next, tps, nb):
    tt, d = xx.shape
    tile = pl.BlockSpec((ROW_TILE, d), lambda i: (i, 0))
    modspec = pl.BlockSpec((1, 3, d), lambda i: (_mod_row(i, tps, nb), 0, 0))
    vec = pl.BlockSpec((1, d), lambda i: (0, 0))
    return pl.pallas_call(
        functools.partial(_out_kernel, with_next=True),
        grid=(tt // ROW_TILE,),
        in_specs=[tile, pl.BlockSpec((d, d), lambda i: (0, 0)), tile, modspec, vec, vec, modspec],
        out_specs=[tile, tile],
        out_shape=[jax.ShapeDtypeStruct((tt, d), F32), jax.ShapeDtypeStruct((tt, d), BF16)],
        compiler_params=_cparams("parallel"),
        name="out_proj_norm_next",
    )(mixed, w_out_bf, xx, mod, ln_g[None, :], ln_b[None, :], mod_next)


def _out_last(mixed, w_out_bf, xx, mod, ln_g, ln_b, tps, nb, n_ctx):
    tt, d = xx.shape
    off = n_ctx // ROW_TILE
    tile = pl.BlockSpec((ROW_TILE, d), lambda b, j: (b * tps + off + j, 0))
    vec = pl.BlockSpec((1, d), lambda b, j: (0, 0))
    return pl.pallas_call(
        functools.partial(_out_kernel, with_next=False),
        grid=(nb, tps - off),
        in_specs=[tile, pl.BlockSpec((d, d), lambda b, j: (0, 0)), tile,
                  pl.BlockSpec((1, 3, d), lambda b, j: (b, 0, 0)), vec, vec],
        out_specs=pl.BlockSpec((None, ROW_TILE, d), lambda b, j: (b, j, 0)),
        out_shape=jax.ShapeDtypeStruct((nb, (tps - off) * ROW_TILE, d), F32),
        compiler_params=_cparams("parallel", "parallel"),
        name="out_proj_norm_last",
    )(mixed, w_out_bf, xx, mod, ln_g[None, :], ln_b[None, :])


def kernel(x, c, ctx, c_ctx, w_ada, b_ada, w_in, ssd_conv_w, ssd_conv_b, ssd_dt_bias, ssd_a_log, ssd_d,
           ssd_norm_w, mla_q_norm, mla_w_uq, mla_kv_norm, mla_w_ukv, gqa_sink, hy_conv_w, hy_conv_b,
           hy_w1, hy_b1, hy_w2, hy_b2, hy_w3, hy_freq, hy_d, w_branch, w_out, ln_g, ln_b):
    nb, n_lat, d = x.shape
    n_ctx = ctx.shape[1]
    tb = n_ctx + n_lat
    tt = nb * tb
    tps = tb // ROW_TILE
    assert d == D_MODEL and n_ctx == ROW_TILE and n_lat % 512 == 0 and n_lat % GRID_W == 0

    cos2, sin2 = _rope_tables(n_ctx, n_lat)
    tabs, tabs_c = _dft_tables(n_lat), _dft_tables(n_ctx)
    cvec = jax.nn.silu(jnp.concatenate([c, c_ctx[None]], axis=0))
    mods = [(_matmul(cvec, w_ada, i, F32, 8, 512, "adaln_mod") + b_ada[i]).reshape(nb + 1, 3, d)
            for i in range(DEPTH)]

    xx, h = _ln_mod(x, ctx, mods[0])
    out = None
    for i in range(DEPTH):
        last = i == DEPTH - 1
        p = _in_proj(h, w_in, i, tb, n_ctx, not last)
        p3 = p.reshape(nb, tb, P_WIDTH)
        g_ssd = _ssd(p3, ssd_conv_w[i], ssd_conv_b[i], ssd_dt_bias[i], ssd_a_log[i], ssd_d[i], ssd_norm_w[i],
                     n_ctx).reshape(tt, SSD_WIDTH)
        qm, km, vm, kg, vg = _attn_prep(p, cos2, sin2, mla_q_norm[i], mla_w_uq[i], mla_kv_norm[i], mla_w_ukv[i],
                                        tps)
        r3 = lambda a: a.reshape(nb, tb, a.shape[-1])
        g_mla = _mla(r3(qm), r3(km), r3(vm), p3, n_ctx).reshape(tt, MLA_WIDTH)
        g_gqa = _gqa(p, kg, vg, cos2, sin2, gqa_sink[i], nb, tb, n_ctx)
        fargs = (hy_w1[i], hy_b1[i], hy_w2[i], hy_b2[i], hy_w3[i], hy_freq[i])
        filt = _hyena_filter(n_lat, *tabs, *fargs)
        filt_c = _hyena_filter(n_ctx, *tabs_c, *fargs)
        g_hy = _hyena(p3, hy_conv_w[i], hy_conv_b[i], hy_d[i], tabs, filt, tabs_c, filt_c, n_ctx,
                      not last).reshape(tt, HY_WIDTH)
        mixed = _mix((g_ssd, g_mla, g_gqa, g_hy), p, w_branch[i].astype(BF16))
        w_out_bf = w_out[i].astype(BF16)
        if last:
            out = _out_last(mixed, w_out_bf, xx, mods[i], ln_g[i], ln_b[i], tps, nb, n_ctx)
        else:
            xx, h = _out_next(mixed, w_out_bf, xx, mods[i], ln_g[i], ln_b[i], mods[i + 1], tps, nb)
    return out
```

```python
import functools
import math

import jax
import jax.numpy as jnp
from jax import lax
from jax.experimental import pallas as pl
from jax.experimental.pallas import tpu as pltpu

D_MODEL = 2048
DEPTH = 2
GRID_W = 64
N_BRANCH = 4
BRANCH_WIDTH = D_MODEL // 2
ROPE_DIM = 64
ROPE_BASE = 10000.0
SSD_WIDTH = BRANCH_WIDTH
SSD_HEAD_DIM = 64
SSD_HEADS = SSD_WIDTH // SSD_HEAD_DIM
SSD_GROUPS = 2
SSD_STATE = 128
SSD_CHUNK = 128
SSD_CONV_CH = SSD_WIDTH + 2 * SSD_GROUPS * SSD_STATE
MLA_HEADS = 8
MLA_NOPE = 128
MLA_ROPE = ROPE_DIM
MLA_V = 128
MLA_Q_LORA = 512
MLA_KV_LORA = 256
MLA_WIDTH = MLA_HEADS * MLA_V
GQA_HEADS = 16
GQA_KV_HEADS = 2
GQA_HEAD_DIM = ROPE_DIM
GQA_WIDTH = GQA_HEADS * GQA_HEAD_DIM
GQA_KV_WIDTH = GQA_KV_HEADS * GQA_HEAD_DIM
WINDOW = 128
HY_WIDTH = BRANCH_WIDTH
HY_POS_EMB = 33
HY_BANDS = (HY_POS_EMB - 1) // 2
HY_FILTER_HIDDEN = 64
HY_FAST_DECAY = 0.3
HY_SLOW_DECAY = 1.5
HY_DECAY_TARGET = 0.01
LN_EPS = 1e-6
NEG_INF = -1e30
DEEPNORM_ALPHA = (2 * DEPTH) ** 0.25

IN_SPLITS = (
    SSD_WIDTH, SSD_CONV_CH, 2 * SSD_HEADS, MLA_Q_LORA, MLA_KV_LORA, MLA_ROPE, MLA_WIDTH, GQA_WIDTH,
    GQA_KV_WIDTH, GQA_KV_WIDTH, GQA_WIDTH, 3 * HY_WIDTH, HY_WIDTH, N_BRANCH * D_MODEL,
)

LANE = 128
ROW_TILE = 256
CHUNK = SSD_CHUNK
VMEM_LIMIT_BYTES = 56 * 1024 * 1024

P_MERGE = 0
P_HY_X0 = 8192
P_HY_X1 = 9216
P_HY_V = 10240
P_HY_Z = 11264
P_SSD_Z = 12288
P_SSD_X = 13312
P_MLA_Z = 14336
P_GQA_Q = 15360
P_GQA_Z = 16384
P_MLA_CQ = 17408
P_SSD_B = 17920
P_SSD_C = 18176
P_MLA_CKV = 18432
P_GQA_K = 18688
P_GQA_V = 18816
P_SMALL = 18944
P_WIDTH = 19072
SMALL_DT = 64

F32 = jnp.float32
BF16 = jnp.bfloat16


def _cparams(*sem):
    return pltpu.CompilerParams(dimension_semantics=sem, vmem_limit_bytes=VMEM_LIMIT_BYTES)


def _split3(v):
    h1 = v.astype(BF16)
    r1 = v - h1.astype(F32)
    h2 = r1.astype(BF16)
    h3 = (r1 - h2.astype(F32)).astype(BF16)
    return h1, h2, h3


def _dot(a, b):
    return jnp.dot(a, b, preferred_element_type=F32)


def _dot_nt(a, b):
    return lax.dot_general(a, b, (((1,), (1,)), ((), ())), preferred_element_type=F32)


def _dot_tn(a, b):
    return lax.dot_general(a, b, (((0,), (0,)), ((), ())), preferred_element_type=F32)


def _dot_exact_rhs(a, b_bf16, pieces):
    parts = _split3(a)[:pieces]
    out = _dot(parts[0], b_bf16)
    for p in parts[1:]:
        out = out + _dot(p, b_bf16)
    return out


def _dot_exact_lhs(a_bf16, b):
    b1, b2, b3 = _split3(b)
    return _dot(a_bf16, b1) + (_dot(a_bf16, b2) + _dot(a_bf16, b3))


def _dot3(a, b):
    a1, a2, _ = _split3(a)
    b1, b2, _ = _split3(b)
    return _dot(a1, b1) + (_dot(a1, b2) + _dot(a2, b1))


def _silu(v):
    return v * jax.nn.sigmoid(v)


def _softplus(v):
    return jnp.maximum(v, 0.0) + jnp.log(1.0 + jnp.exp(-jnp.abs(v)))


def _rotswap(t):
    lane = lax.broadcasted_iota(jnp.int32, t.shape, 1)
    return jnp.where((lane & 63) < 32, pltpu.roll(t, 96, 1), pltpu.roll(t, 32, 1))


def _rope(t, cos, sin_signed):
    return t * cos + _rotswap(t) * sin_signed


def _mm_kernel(a_ref, b_ref, o_ref):
    o_ref[...] = _dot(a_ref[...].astype(BF16), b_ref[...].astype(BF16)).astype(o_ref.dtype)


def _matmul(a, b, layer, out_dtype, tm, tn, name):
    m, k = a.shape
    n = b.shape[2]
    tm = min(tm, m)
    tn = min(tn, n)
    return pl.pallas_call(
        _mm_kernel,
        grid=(pl.cdiv(m, tm), pl.cdiv(n, tn)),
        in_specs=[pl.BlockSpec((tm, k), lambda i, j: (i, 0)),
                  pl.BlockSpec((None, k, tn), lambda i, j: (layer, 0, j))],
        out_specs=pl.BlockSpec((tm, tn), lambda i, j: (i, j)),
        out_shape=jax.ShapeDtypeStruct((m, n), out_dtype),
        compiler_params=_cparams("parallel", "arbitrary"),
        name=name,
    )(a, b)


def _mod_row(i, tiles_per_sample, n_samples):
    return jnp.where(i % tiles_per_sample == 0, n_samples, i // tiles_per_sample)


def _ln_mod_kernel(ctx_ref, x_ref, mod_ref, xx_ref, h_ref):
    def emit(x):
        xx_ref[...] = x
        mu = jnp.mean(x, axis=-1, keepdims=True)
        xc = x - mu
        var = jnp.mean(xc * xc, axis=-1, keepdims=True)
        m = mod_ref[0]
        h_ref[...] = (xc * lax.rsqrt(var + LN_EPS) * (1.0 + m[1:2]) + m[0:1]).astype(h_ref.dtype)

    @pl.when(pl.program_id(1) == 0)
    def _():
        emit(ctx_ref[...])

    @pl.when(pl.program_id(1) != 0)
    def _():
        emit(x_ref[...])


def _ln_mod(x, ctx, mod):
    nb, n_lat, d = x.shape
    assert ctx.shape[1] == ROW_TILE
    tps = 1 + n_lat // ROW_TILE
    rows = pl.BlockSpec((ROW_TILE, d), lambda b, i: (b * tps + i, 0))
    return pl.pallas_call(
        _ln_mod_kernel,
        grid=(nb, tps),
        in_specs=[pl.BlockSpec((None, ROW_TILE, d), lambda b, i: (b, 0, 0)),
                  pl.BlockSpec((None, ROW_TILE, d), lambda b, i: (b, jnp.maximum(i - 1, 0), 0)),
                  pl.BlockSpec((1, 3, d), lambda b, i: (jnp.where(i == 0, nb, b), 0, 0))],
        out_specs=[rows, rows],
        out_shape=[jax.ShapeDtypeStruct((nb * tps * ROW_TILE, d), F32),
                   jax.ShapeDtypeStruct((nb * tps * ROW_TILE, d), BF16)],
        compiler_params=_cparams("parallel", "arbitrary"),
        name="ln_modulate",
    )(ctx, x, mod)


IN_TILE = 1024
IN_ROWS = 1152
SUBLANE = 8


def _in_proj_plan():
    off, start = [], 0
    for size in IN_SPLITS:
        off.append(start)
        start += size
    (o_sz, o_xbc, o_dt, o_cq, o_ckv, o_kpe, o_mz, o_gq, o_gk, o_gv, o_gz, o_hy, o_hz, o_mg) = off
    wide = [(P_MERGE, o_mg, N_BRANCH * D_MODEL), (P_HY_X0, o_hy, 3 * HY_WIDTH), (P_HY_Z, o_hz, HY_WIDTH),
            (P_SSD_Z, o_sz, SSD_WIDTH), (P_SSD_X, o_xbc, SSD_WIDTH), (P_MLA_Z, o_mz, MLA_WIDTH),
            (P_GQA_Q, o_gq, GQA_WIDTH), (P_GQA_Z, o_gz, GQA_WIDTH)]
    n_tiles = -(-P_WIDTH // IN_TILE)
    src = [None] * n_tiles
    for p_off, s_off, width in wide:
        for t in range(width // IN_TILE):
            src[p_off // IN_TILE + t] = s_off + t * IN_TILE
    tail_tiles = [j for j in range(n_tiles) if src[j] is None]
    assert tail_tiles[0] * IN_TILE == P_MLA_CQ
    tail_src = [(o_cq, MLA_Q_LORA), (o_xbc + SSD_WIDTH, 2 * SSD_GROUPS * SSD_STATE), (o_ckv, MLA_KV_LORA),
                (o_gk, 2 * GQA_KV_WIDTH), (o_kpe, MLA_ROPE), (o_dt, 2 * SSD_HEADS)]
    return src, tail_tiles, tail_src


def _in_proj_kernel(start_ref, tail_idx_ref, a_ref, win_ref, tail_ref, o_ref, wb, *, n_ctx, tiles_per_sample):
    @pl.when(pl.program_id(1) == 0)
    def _():
        from_tail = tail_idx_ref[pl.program_id(0)] >= 0

        @pl.when(jnp.logical_not(from_tail))
        def _():
            wb[...] = win_ref[0].astype(wb.dtype)

        @pl.when(from_tail)
        def _():
            wb[...] = tail_ref[...].astype(wb.dtype)

    skip = jnp.logical_and(tail_idx_ref[pl.program_id(0)] == -2, pl.program_id(1) % tiles_per_sample == 0)

    @pl.when(skip)
    def _():
        o_ref[0:n_ctx, :] = jnp.zeros((n_ctx, o_ref.shape[1]), o_ref.dtype)
        o_ref[n_ctx:, :] = _dot_nt(a_ref[n_ctx:, :], wb[...]).astype(o_ref.dtype)

    @pl.when(jnp.logical_not(skip))
    def _():
        o_ref[...] = _dot_nt(a_ref[...], wb[...]).astype(o_ref.dtype)


def _in_proj(h, w_in, layer, tb, n_ctx, ctx_outputs):
    tt, d = h.shape
    w_t = jnp.swapaxes(w_in, 1, 2)
    src, tail_tiles, tail_src = _in_proj_plan()
    tail = jnp.concatenate([w_t[layer, s:s + w, :] for s, w in tail_src], axis=0)
    tail = jnp.pad(tail, ((0, len(tail_tiles) * IN_TILE - tail.shape[0]), (0, 0)))
    assert all(s is None or s % SUBLANE == 0 for s in src)
    rows = IN_ROWS if tb % IN_ROWS == 0 else tb // 2
    assert tb % rows == 0 and rows % (2 * SUBLANE) == 0 and rows > n_ctx
    starts = [0 if s is None else s // SUBLANE for s in src]
    wide_code = lambda j: -1 if ctx_outputs or j == P_SSD_X // IN_TILE else -2
    tidx = [tail_tiles.index(j) if s is None else wide_code(j) for j, s in enumerate(src)]
    as_i32 = lambda v: jnp.asarray(v, jnp.int32)
    gs = pltpu.PrefetchScalarGridSpec(
        num_scalar_prefetch=2, grid=(len(src), tt // rows),
        in_specs=[pl.BlockSpec((rows, d), lambda j, i, st, ti: (i, 0)),
                  pl.BlockSpec((pl.Element(1), pl.Element(IN_TILE), pl.Element(d)),
                               lambda j, i, st, ti: (layer, st[j] * SUBLANE, 0)),
                  pl.BlockSpec((IN_TILE, d), lambda j, i, st, ti: (jnp.maximum(ti[j], 0), 0),
                               pipeline_mode=pl.Buffered(1))],
        out_specs=pl.BlockSpec((rows, IN_TILE), lambda j, i, st, ti: (i, j)),
        scratch_shapes=[pltpu.VMEM((IN_TILE, d), BF16)])
    return pl.pallas_call(
        functools.partial(_in_proj_kernel, n_ctx=n_ctx, tiles_per_sample=tb // rows), grid_spec=gs,
        out_shape=jax.ShapeDtypeStruct((tt, P_WIDTH), BF16),
        compiler_params=_cparams("parallel", "arbitrary"),
        name="in_proj",
    )(as_i32(starts), as_i32(tidx), h, w_t, tail)


def _conv_chunk(src_ref, w, bias, c, n_ctx_chunks, n_chunks):
    tb = n_chunks * CHUNK
    r0 = pl.multiple_of(c * CHUNK, CHUNK)
    cur = src_ref[pl.ds(r0, CHUNK), :].astype(F32)
    rp = pl.multiple_of(jnp.maximum(r0 - 16, 0), 16)
    rn = pl.multiple_of(jnp.minimum(r0 + CHUNK, tb - 16), 16)
    prev_row = src_ref[pl.ds(rp, 16), :].astype(F32)[15:16]
    next_row = src_ref[pl.ds(rn, 16), :].astype(F32)[0:1]
    has_prev = jnp.logical_and(c != 0, c != n_ctx_chunks)
    has_next = jnp.logical_and(c != n_ctx_chunks - 1, c != n_chunks - 1)
    prev_row = jnp.where(has_prev, prev_row, 0.0)
    next_row = jnp.where(has_next, next_row, 0.0)
    row = lax.broadcasted_iota(jnp.int32, cur.shape, 0)
    sp = jnp.where(row == 0, prev_row, pltpu.roll(cur, 1, 0))
    sn = jnp.where(row == CHUNK - 1, next_row, pltpu.roll(cur, CHUNK - 1, 0))
    return sp * w[0:1] + cur * w[1:2] + sn * w[2:3] + bias


def _ssd_kernel(x_ref, b_ref, c_ref, sm_ref, z_ref, cwx_ref, cbx_ref, cwb_ref, cbb_ref, cwc_ref, cbc_ref,
                dtb_ref, alog_ref, dsk_ref, nw_ref, o_ref, ux, ub, uc, yacc, ybwd, st, st_b,
                *, n_ctx_chunks, n_chunks):
    g = pl.program_id(1)
    hp = ux.shape[1]
    nh = hp // SSD_HEAD_DIM

    def conv_body(c, carry):
        r0 = pl.multiple_of(c * CHUNK, CHUNK)
        ux[pl.ds(r0, CHUNK), :] = _silu(_conv_chunk(x_ref, cwx_ref[...], cbx_ref[...], c, n_ctx_chunks,
                                                    n_chunks)).astype(ux.dtype)
        ub[pl.ds(r0, CHUNK), :] = _silu(_conv_chunk(b_ref, cwb_ref[...], cbb_ref[...], c, n_ctx_chunks,
                                                    n_chunks)).astype(ub.dtype)
        uc[pl.ds(r0, CHUNK), :] = _silu(_conv_chunk(c_ref, cwc_ref[...], cbc_ref[...], c, n_ctx_chunks,
                                                    n_chunks)).astype(uc.dtype)
        return carry

    lax.fori_loop(0, n_chunks, conv_body, 0)

    ri = lax.broadcasted_iota(jnp.int32, (CHUNK, CHUNK), 0)
    ci = lax.broadcasted_iota(jnp.int32, (CHUNK, CHUNK), 1)
    lane128 = lax.broadcasted_iota(jnp.int32, (CHUNK, LANE), 1)
    a_all = -jnp.exp(alog_ref[0]) * math.log2(math.e)
    dtb = dtb_ref[0]

    def direction_consts(dirn):
        tri = (ci <= ri) if dirn == 0 else (ci >= ri)
        tri_bf = jnp.where(tri, 1.0, 0.0).astype(BF16)
        expand = jnp.where(lax.broadcasted_iota(jnp.int32, (LANE, hp), 0)
                           == dirn * 16 + (lax.broadcasted_iota(jnp.int32, (LANE, hp), 1) >> 6),
                           1.0, 0.0).astype(BF16)
        return tri, tri_bf, expand

    consts = (direction_consts(0), direction_consts(1))

    def chunk_step(dirn, c, st):
        tri, tri_bf, expand = consts[dirn]
        last = CHUNK - 1 if dirn == 0 else 0
        r0 = pl.multiple_of(c * CHUNK, CHUNK)
        xs = ux[pl.ds(r0, CHUNK), :].astype(F32)
        bc = ub[pl.ds(r0, CHUNK), :]
        cc = uc[pl.ds(r0, CHUNK), :]
        sm = sm_ref[pl.ds(r0, CHUNK), :].astype(F32)
        sm = jnp.where(g == 0, pltpu.roll(sm, LANE - SMALL_DT, 1), pltpu.roll(sm, LANE - SMALL_DT - nh, 1))
        dt = _softplus(sm + dtb)
        la = dt * a_all
        cum = _dot_exact_lhs(tri_bf, la)
        cx = _dot_exact_rhs(cum, expand, 2)
        dtx = _dot_exact_rhs(dt, expand, 2)
        cx_last = cx[last:last + 1]
        ecum = jnp.exp2(cx)
        dec_end = jnp.exp2(cx_last - cx)
        xd = xs * dtx
        xd_bf = xd.astype(BF16)
        gmat = _dot_nt(cc, bc)
        cum_t = cum.T
        cols = []
        for j in range(nh // 2):
            sc = []
            for h in (2 * j, 2 * j + 1):
                k = dirn * 16 + h
                diff = cum[:, k:k + 1] - cum_t[k:k + 1, :]
                lmat = jnp.exp2(jnp.where(tri, diff, NEG_INF))
                sc.append((gmat * lmat).astype(BF16))
            xcol = xd_bf[:, j * LANE:(j + 1) * LANE]
            lo = lane128 < SSD_HEAD_DIM
            rhs = jnp.concatenate([jnp.where(lo, xcol, jnp.zeros_like(xcol)),
                                   jnp.where(lo, jnp.zeros_like(xcol), xcol)], axis=0)
            cols.append(_dot(jnp.concatenate(sc, axis=1), rhs))
        y = jnp.concatenate(cols, axis=1)
        s_prev = st[...]
        y = y + _dot(cc, s_prev.astype(BF16)) * ecum
        st[...] = jnp.exp2(cx_last) * s_prev + _dot_tn(bc, (xd * dec_end).astype(BF16))
        return y, xs, r0

    st[...] = jnp.zeros_like(st)
    st_b[...] = jnp.zeros_like(st_b)

    def scan_body(i, carry):
        y_f, xs_f, r0_f = chunk_step(0, i, st)
        yacc[pl.ds(r0_f, CHUNK), :] = y_f + xs_f * dsk_ref[...]
        c_b = jnp.where(i < n_ctx_chunks, n_ctx_chunks - 1 - i, n_chunks - 1 + n_ctx_chunks - i)
        y_b, _, r0_b = chunk_step(1, c_b, st_b)
        ybwd[pl.ds(r0_b, CHUNK), :] = y_b
        return carry

    lax.fori_loop(0, n_chunks, scan_body, 0)

    def gate_body(c, carry):
        r0 = pl.multiple_of(c * CHUNK, CHUNK)
        tot = (yacc[pl.ds(r0, CHUNK), :] + ybwd[pl.ds(r0, CHUNK), :]) * _silu(z_ref[pl.ds(r0, CHUNK), :].astype(F32))
        ms = jnp.mean(tot * tot, axis=-1, keepdims=True)
        o_ref[pl.ds(r0, CHUNK), :] = (tot * lax.rsqrt(ms + LN_EPS) * nw_ref[...]).astype(o_ref.dtype)
        return carry

    lax.fori_loop(0, n_chunks, gate_body, 0)


def _ssd(p3, conv_w, conv_b, dt_bias, a_log, d_skip, norm_w, n_ctx):
    nb, tb, _ = p3.shape
    hp = SSD_WIDTH // SSD_GROUPS
    n_chunks = tb // CHUNK
    gn = SSD_GROUPS * SSD_STATE
    cwx, cwb, cwc = conv_w[:, :SSD_WIDTH], conv_w[:, SSD_WIDTH:SSD_WIDTH + gn], conv_w[:, SSD_WIDTH + gn:]
    cb = conv_b[None, :]
    cbx, cbb, cbc = cb[:, :SSD_WIDTH], cb[:, SSD_WIDTH:SSD_WIDTH + gn], cb[:, SSD_WIDTH + gn:]
    hg = SSD_HEADS // SSD_GROUPS

    def per_group(v):
        v = v.reshape(2, SSD_GROUPS, hg).transpose(1, 0, 2)
        v = jnp.pad(v, ((0, 0), (0, 0), (0, 16 - hg))).reshape(SSD_GROUPS, 1, 32)
        return jnp.pad(v, ((0, 0), (0, 0), (0, LANE - 32)))

    dsk = jnp.repeat(d_skip, SSD_HEAD_DIM)[None, :]
    kern = functools.partial(_ssd_kernel, n_ctx_chunks=n_ctx // CHUNK, n_chunks=n_chunks)
    col = lambda off, w: (lambda b, g: (b, 0, off // w + g))
    par = lambda b, g: (0, g)
    return pl.pallas_call(
        kern,
        grid=(nb, SSD_GROUPS),
        in_specs=[pl.BlockSpec((None, tb, hp), col(P_SSD_X, hp)),
                  pl.BlockSpec((None, tb, SSD_STATE), col(P_SSD_B, SSD_STATE)),
                  pl.BlockSpec((None, tb, SSD_STATE), col(P_SSD_C, SSD_STATE)),
                  pl.BlockSpec((None, tb, LANE), lambda b, g: (b, 0, P_SMALL // LANE)),
                  pl.BlockSpec((None, tb, hp), col(P_SSD_Z, hp)),
                  pl.BlockSpec((3, hp), par), pl.BlockSpec((1, hp), par),
                  pl.BlockSpec((3, SSD_STATE), par), pl.BlockSpec((1, SSD_STATE), par),
                  pl.BlockSpec((3, SSD_STATE), par), pl.BlockSpec((1, SSD_STATE), par),
                  pl.BlockSpec((1, 1, LANE), lambda b, g: (g, 0, 0)),
                  pl.BlockSpec((1, 1, LANE), lambda b, g: (g, 0, 0)),
                  pl.BlockSpec((1, hp), par), pl.BlockSpec((1, hp), par)],
        out_specs=pl.BlockSpec((None, tb, hp), lambda b, g: (b, 0, g)),
        out_shape=jax.ShapeDtypeStruct((nb, tb, SSD_WIDTH), BF16),
        scratch_shapes=[pltpu.VMEM((tb, hp), BF16), pltpu.VMEM((tb, SSD_STATE), BF16),
                        pltpu.VMEM((tb, SSD_STATE), BF16), pltpu.VMEM((tb, hp), F32),
                        pltpu.VMEM((tb, hp), F32), pltpu.VMEM((SSD_STATE, hp), F32),
                        pltpu.VMEM((SSD_STATE, hp), F32)],
        compiler_params=_cparams("parallel", "parallel"),
        name="ssd_bidir",
    )(p3, p3, p3, p3, p3, cwx, cbx, cwb, cbb, cwc, cbc, per_group(dt_bias), per_group(a_log), dsk,
      norm_w[None, :])


def _rope_tables(n_ctx, n_lat):
    rows = n_lat // GRID_W
    row = jnp.repeat(jnp.arange(rows, dtype=F32), GRID_W)
    colp = jnp.tile(jnp.arange(GRID_W, dtype=F32), rows)
    n_freq = ROPE_DIM // 4
    inv_freq = ROPE_BASE ** (-jnp.arange(n_freq, dtype=F32) / n_freq)
    ang = jnp.concatenate([row[:, None] * inv_freq[None, :], colp[:, None] * inv_freq[None, :]], axis=-1)
    cos, sin = jnp.cos(ang), jnp.sin(ang)
    cos2 = jnp.tile(cos, (1, 4))
    sin2 = jnp.tile(jnp.concatenate([-sin, sin], axis=-1), (1, 2))
    cos2 = jnp.concatenate([jnp.ones((n_ctx, LANE), F32), cos2], axis=0)
    sin2 = jnp.concatenate([jnp.zeros((n_ctx, LANE), F32), sin2], axis=0)
    return cos2, sin2


def _attn_prep_kernel(cq_ref, ckv_ref, sm_ref, gk_ref, gv_ref, cos_ref, sin_ref, qn_ref, wq_ref, kn_ref,
                      wk_ref, wv_ref, q_ref, k_ref, v_ref, kg_ref, vg_ref):
    cos, sin = cos_ref[...], sin_ref[...]
    lane = lax.broadcasted_iota(jnp.int32, cos.shape, 1)
    scale = (MLA_NOPE + MLA_ROPE) ** -0.5 * math.log2(math.e)
    cq = cq_ref[...].astype(F32)
    cqn = cq * lax.rsqrt(jnp.mean(cq * cq, axis=-1, keepdims=True) + LN_EPS) * qn_ref[...]
    q = _dot(cqn.astype(BF16), wq_ref[...])
    ckv = ckv_ref[...].astype(F32)
    ckvn = (ckv * lax.rsqrt(jnp.mean(ckv * ckv, axis=-1, keepdims=True) + LN_EPS) * kn_ref[...]).astype(BF16)
    kn = _dot(ckvn, wk_ref[...])
    v_ref[...] = _dot(ckvn, wv_ref[...]).astype(v_ref.dtype)
    kpe = jnp.where(lane < MLA_ROPE, _rope(sm_ref[...].astype(F32), cos, sin), 0.0).astype(k_ref.dtype)
    for h in range(MLA_HEADS):
        o = 2 * h * LANE
        q_ref[:, o:o + LANE] = (q[:, o:o + LANE] * scale).astype(q_ref.dtype)
        q_ref[:, o + LANE:o + 2 * LANE] = (_rope(q[:, o + LANE:o + 2 * LANE], cos, sin) * scale).astype(q_ref.dtype)
        k_ref[:, o:o + LANE] = kn[:, h * LANE:(h + 1) * LANE].astype(k_ref.dtype)
        k_ref[:, o + LANE:o + 2 * LANE] = kpe
    kg_ref[...] = _rope(gk_ref[...].astype(F32), cos, sin).astype(kg_ref.dtype)
    vg_ref[...] = gv_ref[...].astype(F32).T.astype(vg_ref.dtype)


def _attn_prep(p, cos2, sin2, q_norm, w_uq, kv_norm, w_ukv, tps):
    tt = p.shape[0]
    hq = MLA_NOPE + MLA_ROPE
    wq = w_uq.reshape(MLA_Q_LORA, MLA_HEADS, hq)
    wq = jnp.pad(wq, ((0, 0), (0, 0), (0, 2 * LANE - hq))).reshape(MLA_Q_LORA, MLA_HEADS * 2 * LANE).astype(BF16)
    wkv = w_ukv.reshape(MLA_KV_LORA, MLA_HEADS, MLA_NOPE + MLA_V)
    wk = wkv[:, :, :MLA_NOPE].reshape(MLA_KV_LORA, MLA_HEADS * MLA_NOPE).astype(BF16)
    wv = wkv[:, :, MLA_NOPE:].reshape(MLA_KV_LORA, MLA_HEADS * MLA_V).astype(BF16)
    full = lambda shape: pl.BlockSpec(shape, lambda i: (0, 0))
    pcol = lambda off, w: pl.BlockSpec((ROW_TILE, w), lambda i: (i, off // w))
    rows = lambda w: pl.BlockSpec((ROW_TILE, w), lambda i: (i, 0))
    tab = pl.BlockSpec((ROW_TILE, LANE), lambda i: (i % tps, 0))
    return pl.pallas_call(
        _attn_prep_kernel,
        grid=(tt // ROW_TILE,),
        in_specs=[pcol(P_MLA_CQ, MLA_Q_LORA), pcol(P_MLA_CKV, MLA_KV_LORA), pcol(P_SMALL, LANE),
                  pcol(P_GQA_K, LANE), pcol(P_GQA_V, LANE), tab, tab,
                  full((1, MLA_Q_LORA)), full(wq.shape), full((1, MLA_KV_LORA)), full(wk.shape), full(wv.shape)],
        out_specs=[rows(MLA_HEADS * 2 * LANE), rows(MLA_HEADS * 2 * LANE), rows(MLA_WIDTH),
                   rows(LANE), pl.BlockSpec((LANE, ROW_TILE), lambda i: (0, i))],
        out_shape=[jax.ShapeDtypeStruct((tt, MLA_HEADS * 2 * LANE), BF16),
                   jax.ShapeDtypeStruct((tt, MLA_HEADS * 2 * LANE), BF16),
                   jax.ShapeDtypeStruct((tt, MLA_WIDTH), BF16),
                   jax.ShapeDtypeStruct((tt, LANE), BF16),
                   jax.ShapeDtypeStruct((LANE, tt), BF16)],
        compiler_params=_cparams("parallel"),
        name="attn_prep",
    )(p, p, p, p, p, cos2, sin2, q_norm[None, :], wq, kv_norm[None, :], wk, wv)


MLA_Q_CHUNK = 256


def _mla_kernel(q_ref, k_ref, v_ref, z_ref, o_ref, *, n_ctx):
    tb = k_ref.shape[0]
    ones_col = jnp.where(lax.broadcasted_iota(jnp.int32, (tb, LANE), 1) == 0, 1.0, 0.0).astype(BF16)
    v_ext = jnp.concatenate([v_ref[...], ones_col], axis=1)

    def attend(r0, rows, n_keys):
        s = _dot_nt(q_ref[r0:r0 + rows, :], k_ref[0:n_keys, :])
        m = jnp.max(s, axis=-1, keepdims=True)
        p = jnp.exp2(s - m).astype(BF16)
        o_ext = _dot(p, v_ext[0:n_keys])
        o = o_ext[:, :MLA_V] / o_ext[:, MLA_V:MLA_V + 1]
        o_ref[r0:r0 + rows, :] = (o * _silu(z_ref[r0:r0 + rows, :].astype(F32))).astype(o_ref.dtype)

    attend(0, n_ctx, n_ctx)
    for r0 in range(n_ctx, tb, MLA_Q_CHUNK):
        attend(r0, MLA_Q_CHUNK, tb)


def _mla(qm3, km3, vm3, p3, n_ctx):
    nb, tb, _ = p3.shape
    assert (tb - n_ctx) % MLA_Q_CHUNK == 0
    blk = lambda w, off: pl.BlockSpec((None, tb, w), lambda b, h: (b, 0, off // w + h))
    return pl.pallas_call(
        functools.partial(_mla_kernel, n_ctx=n_ctx),
        grid=(nb, MLA_HEADS),
        in_specs=[blk(2 * LANE, 0), blk(2 * LANE, 0), blk(LANE, 0), blk(LANE, P_MLA_Z)],
        out_specs=blk(LANE, 0),
        out_shape=jax.ShapeDtypeStruct((nb, tb, MLA_WIDTH), BF16),
        compiler_params=_cparams("parallel", "parallel"),
        name="mla_attention",
    )(qm3, km3, vm3, p3)


GQA_HEAD_BATCH = 8

def _gqa_kernel(sink_ref, q_ref, kp_ref, ko_ref, kn_ref, kc_ref, vp_ref, vo_ref, vn_ref, vc_ref, cos_ref,
                sin_ref, z_ref, o_ref, *, n_ctx_blocks):
    blk = pl.program_id(1)
    n_blk = pl.num_programs(1)
    is_lat = blk >= n_ctx_blocks
    has_prev = blk > n_ctx_blocks
    has_next = jnp.logical_and(is_lat, blk < n_blk - 1)
    cos, sin = cos_ref[...], sin_ref[...]
    log2e = math.log2(math.e)
    scale = GQA_HEAD_DIM ** -0.5 * log2e
    w = q_ref.shape[0]
    hd = GQA_HEAD_DIM
    hpk = GQA_HEADS // GQA_KV_HEADS
    q = jnp.concatenate([_rope(q_ref[:, c * LANE:(c + 1) * LANE].astype(F32), cos, sin) * scale
                         for c in range(GQA_WIDTH // LANE)], axis=1)
    qt = q.T.astype(BF16)
    kall = jnp.concatenate([kp_ref[...], ko_ref[...], kn_ref[...], kc_ref[...]], axis=0)
    vt = jnp.concatenate([vp_ref[...], vo_ref[...], vn_ref[...], vc_ref[...]], axis=1)
    kj = lax.broadcasted_iota(jnp.int32, (w, w), 0)
    qi = lax.broadcasted_iota(jnp.int32, (w, w), 1)
    m_prev = jnp.logical_and(kj >= qi, has_prev)
    m_next = jnp.logical_and(kj <= qi, has_next)
    zeros = jnp.zeros((hd, w), BF16)
    ones_rows = jnp.ones((16, kall.shape[0]), BF16)
    nbat = GQA_HEAD_BATCH
    tile = lambda a: jnp.concatenate([a] * nbat, axis=1)
    m_prev, m_next = tile(m_prev), tile(m_next)
    outs = []
    for h0 in range(0, GQA_HEADS, nbat):
        kh = h0 // hpk
        heads = range(h0, h0 + nbat)
        rhs = jnp.concatenate(
            [jnp.concatenate([qt[h * hd:(h + 1) * hd], zeros] if kh == 0 else [zeros, qt[h * hd:(h + 1) * hd]],
                             axis=0) for h in heads], axis=1)
        st = _dot(kall, rhs)
        s_p = jnp.where(m_prev, st[0:w], NEG_INF)
        s_o = jnp.where(is_lat, st[w:2 * w], NEG_INF)
        s_n = jnp.where(m_next, st[2 * w:3 * w], NEG_INF)
        s = jnp.concatenate([s_p, s_o, s_n, st[3 * w:]], axis=0)
        sink = jnp.concatenate([jnp.full((1, w), sink_ref[h] * log2e, F32) for h in heads], axis=1)
        m = jnp.maximum(jnp.max(s, axis=0, keepdims=True), sink)
        pr = jnp.exp2(s - m).astype(BF16)
        ot = _dot(jnp.concatenate([vt[kh * hd:(kh + 1) * hd], ones_rows], axis=0), pr)
        l = ot[hd:hd + 1] + jnp.exp2(sink - m)
        ot = ot[:hd] / l
        outs.extend(ot[:, i * w:(i + 1) * w] for i in range(nbat))
    y = jnp.concatenate(outs, axis=0).T
    o_ref[...] = (y * _silu(z_ref[...].astype(F32))).astype(o_ref.dtype)


def _gqa(p, kg, vgt, cos2, sin2, sink, nb, tb, n_ctx):
    w = WINDOW
    bps = tb // w
    row = lambda b, i: b * bps + i
    prev = lambda b, i: b * bps + jnp.maximum(i - 1, 0)
    nxt = lambda b, i: b * bps + jnp.minimum(i + 1, bps - 1)
    kspec = lambda f: pl.BlockSpec((w, LANE), lambda b, i: (f(b, i), 0))
    vspec = lambda f: pl.BlockSpec((LANE, w), lambda b, i: (0, f(b, i)))
    kctx = pl.BlockSpec((n_ctx, LANE), lambda b, i: (b * (tb // n_ctx), 0))
    vctx = pl.BlockSpec((LANE, n_ctx), lambda b, i: (0, b * (tb // n_ctx)))
    tab = pl.BlockSpec((w, LANE), lambda b, i: (i, 0))
    return pl.pallas_call(
        functools.partial(_gqa_kernel, n_ctx_blocks=n_ctx // w),
        grid=(nb, bps),
        in_specs=[pl.BlockSpec(memory_space=pltpu.SMEM),
                  pl.BlockSpec((w, GQA_WIDTH), lambda b, i: (row(b, i), P_GQA_Q // GQA_WIDTH)),
                  kspec(prev), kspec(row), kspec(nxt), kctx, vspec(prev), vspec(row), vspec(nxt), vctx, tab, tab,
                  pl.BlockSpec((w, GQA_WIDTH), lambda b, i: (row(b, i), P_GQA_Z // GQA_WIDTH))],
        out_specs=pl.BlockSpec((w, GQA_WIDTH), lambda b, i: (row(b, i), 0)),
        out_shape=jax.ShapeDtypeStruct((nb * tb, GQA_WIDTH), BF16),
        compiler_params=_cparams("parallel", "arbitrary"),
        name="gqa_window_attention",
    )(sink, p, kg, kg, kg, kg, vgt, vgt, vgt, vgt, cos2, sin2, p)


def _dft_tables(n):
    m = jnp.arange(n, dtype=jnp.int32)[None, :]

    def small(kvals):
        ang = ((kvals[:, None] * m) % (2 * n)).astype(F32) * (math.pi / n)
        return jnp.cos(ang), jnp.sin(ang)

    ca, sa = small(64 * jnp.arange(n // 64, dtype=jnp.int32))
    cb, sb = small(jnp.arange(64, dtype=jnp.int32))
    cos_t = ca[:, None, :] * cb[None, :, :] - sa[:, None, :] * sb[None, :, :]
    sin_t = sa[:, None, :] * cb[None, :, :] + ca[:, None, :] * sb[None, :, :]
    return cos_t.reshape(n, n).astype(BF16), sin_t.reshape(n, n).astype(BF16)


def _hyena_feats(n):
    t = jnp.linspace(0.0, 1.0, n, dtype=F32)[:, None]
    w_ang = (2.0 * math.pi / n) * jnp.arange(n, dtype=F32)[:, None]
    bands = jnp.linspace(1e-4, HY_BANDS - 1, HY_BANDS, dtype=F32)[None, :]
    feats = jnp.concatenate([t, jnp.cos(bands * w_ang), -jnp.sin(bands * w_ang)], axis=-1)
    return jnp.pad(feats, ((0, 0), (0, LANE - HY_POS_EMB)))


def _hyena_filter_kernel(feats_ref, w1_ref, b1_ref, w2_ref, b2_ref, w3f_ref, w3b_ref, freq_ref, dl_ref,
                         cos_ref, sin_ref, kr_ref, ks_ref, kq_ref, hs, hd, *, n):
    kt = pl.program_id(1)
    tk = cos_ref.shape[0]

    @pl.when(kt == 0)
    def _():
        feats = feats_ref[...]
        t = feats[:, 0:1]
        hdn = jnp.sin(freq_ref[0:1, :] * (_dot3(feats, w1_ref[...]) + b1_ref[...]))
        hdn = jnp.sin(freq_ref[1:2, :] * (_dot3(hdn, w2_ref[...]) + b2_ref[...]))
        decay = jnp.exp(-t * dl_ref[...])
        hf = _dot3(hdn, w3f_ref[...]) * decay
        hb = _dot3(hdn, w3b_ref[...]) * decay
        row = lax.broadcasted_iota(jnp.int32, hb.shape, 0)
        hb = jnp.where(row == 0, 0.0, hb)
        hsum = hf + hb
        hs[...] = hsum.astype(hs.dtype)
        hd[...] = (hf - hb).astype(hd.dtype)
        alt = jnp.where((row & 1) == 0, 1.0, -1.0)
        kq_ref[...] = jnp.sum(hsum * alt, axis=0, keepdims=True) * (1.0 / (2 * n))

    kidx = kt * tk + lax.broadcasted_iota(jnp.int32, (tk, 1), 0)
    wk = jnp.where(kidx == 0, 0.5 / n, 1.0 / n)
    kr_ref[...] = _dot(cos_ref[...], hs[...]) * wk
    ks_ref[...] = _dot(sin_ref[...], hd[...]) * wk


def _hyena_filter(n, cos_t, sin_t, w1, b1, w2, b2, w3, freq):
    cbw = 512
    tk = min(256, n)
    deltas = jnp.abs(jnp.linspace(math.log(HY_DECAY_TARGET) / HY_FAST_DECAY,
                                  math.log(HY_DECAY_TARGET) / HY_SLOW_DECAY, HY_WIDTH, dtype=F32))[None, :]
    w1p = jnp.pad(w1, ((0, LANE - HY_POS_EMB), (0, 0)))
    full = lambda a: pl.BlockSpec(a.shape, lambda c, k: (0,) * a.ndim)
    nblk = HY_WIDTH // cbw
    kern = functools.partial(_hyena_filter_kernel, n=n)
    feats = _hyena_feats(n)
    return pl.pallas_call(
        kern,
        grid=(nblk, n // tk),
        in_specs=[full(feats), full(w1p), pl.BlockSpec((1, HY_FILTER_HIDDEN), lambda c, k: (0, 0)),
                  full(w2), pl.BlockSpec((1, HY_FILTER_HIDDEN), lambda c, k: (0, 0)),
                  pl.BlockSpec((HY_FILTER_HIDDEN, cbw), lambda c, k: (0, c)),
                  pl.BlockSpec((HY_FILTER_HIDDEN, cbw), lambda c, k: (0, nblk + c)),
                  full(freq), pl.BlockSpec((1, cbw), lambda c, k: (0, c)),
                  pl.BlockSpec((tk, n), lambda c, k: (k, 0)), pl.BlockSpec((tk, n), lambda c, k: (k, 0))],
        out_specs=[pl.BlockSpec((tk, cbw), lambda c, k: (k, c)), pl.BlockSpec((tk, cbw), lambda c, k: (k, c)),
                   pl.BlockSpec((1, cbw), lambda c, k: (0, c))],
        out_shape=[jax.ShapeDtypeStruct((n, HY_WIDTH), F32), jax.ShapeDtypeStruct((n, HY_WIDTH), F32),
                   jax.ShapeDtypeStruct((1, HY_WIDTH), F32)],
        scratch_shapes=[pltpu.VMEM((n, cbw), BF16), pltpu.VMEM((n, cbw), BF16)],
        compiler_params=_cparams("parallel", "arbitrary"),
        name="hyena_filter",
    )(feats, w1p, b1[None, :], w2, b2[None, :], w3, w3, freq, deltas, cos_t, sin_t)


def _spectral_product(a, bq, kr, ks):
    return (a * kr - bq * ks).astype(BF16), (a * ks + bq * kr).astype(BF16)


HY_TILE = 512


def _hyena_kernel(x0_ref, x1_ref, v_ref, z_ref, cw0_ref, cb0_ref, cw1_ref, cb1_ref, cwv_ref, cbv_ref, d_ref,
                  cr_ref, sr_ref, kr_ref, ks_ref, kq_ref, ccx_ref, scx_ref, krx_ref, ksx_ref, kqx_ref,
                  o_ref, u, yr_s, ys_s, unyq, *, n_ctx, n_chunks, with_ctx):
    kt = pl.program_id(2)
    nk = pl.num_programs(2) // 2
    n_ctx_chunks = n_ctx // CHUNK
    tb = n_chunks * CHUNK
    tk = cr_ref.shape[0]

    def alt_sign(shape):
        return jnp.where((lax.broadcasted_iota(jnp.int32, shape, 0) & 1) == 0, 1.0, -1.0)

    def gate_chunk(c):
        x0c = _conv_chunk(x0_ref, cw0_ref[...], cb0_ref[...], c, n_ctx_chunks, n_chunks)
        r0 = pl.multiple_of(c * CHUNK, CHUNK)
        return x0c * _silu(z_ref[pl.ds(r0, CHUNK), :].astype(F32))

    @pl.when(kt == 0)
    def _():
        def conv_body(c, carry):
            r0 = pl.multiple_of(c * CHUNK, CHUNK)
            x1c = _conv_chunk(x1_ref, cw1_ref[...], cb1_ref[...], c, n_ctx_chunks, n_chunks)
            vc = _conv_chunk(v_ref, cwv_ref[...], cbv_ref[...], c, n_ctx_chunks, n_chunks)
            u[pl.ds(r0, CHUNK), :] = (x1c * vc).astype(u.dtype)
            return carry

        lax.fori_loop(0, n_chunks, conv_body, 0)
        ulf = u[n_ctx:tb, :].astype(F32)
        unyq[...] = jnp.sum(ulf * alt_sign(ulf.shape), axis=0, keepdims=True) * kq_ref[...]

    @pl.when(kt < nk)
    def _():
        ul = u[n_ctx:tb, :]
        yr, ys = _spectral_product(_dot(cr_ref[...], ul), _dot(sr_ref[...], ul), kr_ref[...], ks_ref[...])
        f0 = pl.multiple_of(kt * tk, tk)
        yr_s[pl.ds(f0, tk), :] = yr
        ys_s[pl.ds(f0, tk), :] = ys

    @pl.when(kt >= nk)
    def _():
        t = kt - nk
        y = _dot(cr_ref[...], yr_s[...]) + _dot(sr_ref[...], ys_s[...])
        for j in range(tk // CHUNK):
            c = n_ctx_chunks + t * (tk // CHUNK) + j
            r0 = pl.multiple_of(c * CHUNK, CHUNK)
            uf = u[pl.ds(r0, CHUNK), :].astype(F32)
            yj = y[j * CHUNK:(j + 1) * CHUNK] + alt_sign(uf.shape) * unyq[...] + uf * d_ref[...]
            o_ref[pl.ds(r0, CHUNK), :] = (yj * gate_chunk(c)).astype(o_ref.dtype)

    @pl.when(kt == 2 * nk - 1)
    def _():
        if with_ctx:
            uc = u[0:n_ctx, :]
            ucf = uc.astype(F32)
            alt = alt_sign(ucf.shape)
            yr, ys = _spectral_product(_dot(ccx_ref[...], uc), _dot(scx_ref[...], uc), krx_ref[...], ksx_ref[...])
            y = _dot(ccx_ref[...], yr) + _dot(scx_ref[...], ys)
            y = y + alt * (jnp.sum(ucf * alt, axis=0, keepdims=True) * kqx_ref[...]) + ucf * d_ref[...]
            for c in range(n_ctx_chunks):
                o_ref[c * CHUNK:(c + 1) * CHUNK, :] = (y[c * CHUNK:(c + 1) * CHUNK] * gate_chunk(c)).astype(o_ref.dtype)
        else:
            o_ref[0:n_ctx, :] = jnp.zeros((n_ctx, o_ref.shape[1]), o_ref.dtype)


def _hyena(p3, conv_w, conv_b, d_skip, tabs, filt, tabs_c, filt_c, n_ctx, with_ctx):
    nb, tb, _ = p3.shape
    n_lat = tb - n_ctx
    cbw = 512
    tk = HY_TILE
    nk = n_lat // tk
    cos_t, sin_t = tabs
    kr, ks, kq = filt
    cos_c, sin_c = tabs_c
    krc, ksc, kqc = filt_c
    cb = conv_b[None, :]
    pcol = lambda off: pl.BlockSpec((None, tb, cbw), lambda b, c, k: (b, 0, off // cbw + c))
    par = lambda rows, sec: pl.BlockSpec((rows, cbw), lambda b, c, k: (0, sec * (HY_WIDTH // cbw) + c))
    kern = functools.partial(_hyena_kernel, n_ctx=n_ctx, n_chunks=tb // CHUNK, with_ctx=with_ctx)
    small = lambda a: pl.BlockSpec(a.shape, lambda b, c, k: (0, 0))
    chan = lambda rows: pl.BlockSpec((rows, cbw), lambda b, c, k: (0, c))
    table = pl.BlockSpec((tk, n_lat), lambda b, c, k: (k % nk, 0))
    spec = pl.BlockSpec((tk, cbw), lambda b, c, k: (jnp.minimum(k, nk - 1), c))
    return pl.pallas_call(
        kern,
        grid=(nb, HY_WIDTH // cbw, 2 * nk),
        in_specs=[pcol(P_HY_X0), pcol(P_HY_X1), pcol(P_HY_V), pcol(P_HY_Z),
                  par(3, 0), par(1, 0), par(3, 1), par(1, 1), par(3, 2), par(1, 2), chan(1),
                  table, table, spec, spec, chan(1), small(cos_c), small(sin_c), chan(n_ctx), chan(n_ctx), chan(1)],
        out_specs=pl.BlockSpec((None, tb, cbw), lambda b, c, k: (b, 0, c)),
        out_shape=jax.ShapeDtypeStruct((nb, tb, HY_WIDTH), BF16),
        scratch_shapes=[pltpu.VMEM((tb, cbw), BF16), pltpu.VMEM((n_lat, cbw), BF16), pltpu.VMEM((n_lat, cbw), BF16),
                        pltpu.VMEM((1, cbw), F32)],
        compiler_params=_cparams("parallel", "parallel", "arbitrary"),
        name="hyena_dft_conv",
    )(p3, p3, p3, p3, conv_w, cb, conv_w, cb, conv_w, cb, d_skip[None, :], cos_t, sin_t,
      kr, ks, kq, cos_c, sin_c, krc, ksc, kqc)


def _mix_kernel(g0_ref, g1_ref, g2_ref, g3_ref, lg_ref, w_ref, o_ref, acc):
    k = pl.program_id(1)
    for kk, g_ref in enumerate((g0_ref, g1_ref, g2_ref, g3_ref)):
        @pl.when(k == kk)
        def _(g_ref=g_ref, kk=kk):
            gate = 0.5 * jnp.tanh(0.5 * lg_ref[...].astype(F32)) + 0.5
            contrib = gate * _dot(g_ref[...], w_ref[0])
            if kk == 0:
                acc[...] = contrib
            else:
                acc[...] += contrib

    @pl.when(k == N_BRANCH - 1)
    def _():
        o_ref[...] = acc[...].astype(o_ref.dtype)


def _mix(gated, p, w_branch_bf):
    tt = p.shape[0]
    tm = math.gcd(tt, 1024)
    g_spec = pl.BlockSpec((tm, BRANCH_WIDTH), lambda i, k: (i, 0))
    return pl.pallas_call(
        _mix_kernel,
        grid=(tt // tm, N_BRANCH),
        in_specs=[g_spec, g_spec, g_spec, g_spec,
                  pl.BlockSpec((tm, D_MODEL), lambda i, k: (i, P_MERGE // D_MODEL + k)),
                  pl.BlockSpec((1, BRANCH_WIDTH, D_MODEL), lambda i, k: (k, 0, 0))],
        out_specs=pl.BlockSpec((tm, D_MODEL), lambda i, k: (i, 0)),
        out_shape=jax.ShapeDtypeStruct((tt, D_MODEL), BF16),
        scratch_shapes=[pltpu.VMEM((tm, D_MODEL), F32)],
        compiler_params=_cparams("parallel", "arbitrary"),
        name="branch_mix",
    )(*gated, p, w_branch_bf)


def _out_kernel(a_ref, w_ref, x_ref, mod_ref, g_ref, b_ref, *rest, with_next):
    mixed = _dot(a_ref[...], w_ref[...])
    r = DEEPNORM_ALPHA * x_ref[...] + mod_ref[0][2:3] * mixed
    mu = jnp.mean(r, axis=-1, keepdims=True)
    rc = r - mu
    var = jnp.mean(rc * rc, axis=-1, keepdims=True)
    xn = rc * lax.rsqrt(var + LN_EPS) * g_ref[...] + b_ref[...]
    if with_next:
        modn_ref, o_ref, h_ref = rest
        o_ref[...] = xn
        mu = jnp.mean(xn, axis=-1, keepdims=True)
        xc = xn - mu
        var = jnp.mean(xc * xc, axis=-1, keepdims=True)
        mn = modn_ref[0]
        h_ref[...] = (xc * lax.rsqrt(var + LN_EPS) * (1.0 + mn[1:2]) + mn[0:1]).astype(h_ref.dtype)
    else:
        (o_ref,) = rest
        o_ref[...] = xn


def _out_next(mixed, w_out_bf, xx, mod, ln_g, ln_b, mod_next, tps, nb):
    tt, d = xx.shape
    tile = pl.BlockSpec((ROW_TILE, d), lambda i: (i, 0))
    modspec = pl.BlockSpec((1, 3, d), lambda i: (_mod_row(i, tps, nb), 0, 0))
    vec = pl.BlockSpec((1, d), lambda i: (0, 0))
    return pl.pallas_call(
        functools.partial(_out_kernel, with_next=True),
        grid=(tt // ROW_TILE,),
        in_specs=[tile, pl.BlockSpec((d, d), lambda i: (0, 0)), tile, modspec, vec, vec, modspec],
        out_specs=[tile, tile],
        out_shape=[jax.ShapeDtypeStruct((tt, d), F32), jax.ShapeDtypeStruct((tt, d), BF16)],
        compiler_params=_cparams("parallel"),
        name="out_proj_norm_next",
    )(mixed, w_out_bf, xx, mod, ln_g[None, :], ln_b[None, :], mod_next)


def _out_last(mixed, w_out_bf, xx, mod, ln_g, ln_b, tps, nb, n_ctx):
    tt, d = xx.shape
    off = n_ctx // ROW_TILE
    tile = pl.BlockSpec((ROW_TILE, d), lambda b, j: (b * tps + off + j, 0))
    vec = pl.BlockSpec((1, d), lambda b, j: (0, 0))
    return pl.pallas_call(
        functools.partial(_out_kernel, with_next=False),
        grid=(nb, tps - off),
        in_specs=[tile, pl.BlockSpec((d, d), lambda b, j: (0, 0)), tile,
                  pl.BlockSpec((1, 3, d), lambda b, j: (b, 0, 0)), vec, vec],
        out_specs=pl.BlockSpec((None, ROW_TILE, d), lambda b, j: (b, j, 0)),
        out_shape=jax.ShapeDtypeStruct((nb, (tps - off) * ROW_TILE, d), F32),
        compiler_params=_cparams("parallel", "parallel"),
        name="out_proj_norm_last",
    )(mixed, w_out_bf, xx, mod, ln_g[None, :], ln_b[None, :])


def kernel(x, c, ctx, c_ctx, w_ada, b_ada, w_in, ssd_conv_w, ssd_conv_b, ssd_dt_bias, ssd_a_log, ssd_d,
           ssd_norm_w, mla_q_norm, mla_w_uq, mla_kv_norm, mla_w_ukv, gqa_sink, hy_conv_w, hy_conv_b,
           hy_w1, hy_b1, hy_w2, hy_b2, hy_w3, hy_freq, hy_d, w_branch, w_out, ln_g, ln_b):
    nb, n_lat, d = x.shape
    n_ctx = ctx.shape[1]
    tb = n_ctx + n_lat
    tt = nb * tb
    tps = tb // ROW_TILE
    assert d == D_MODEL and n_ctx == ROW_TILE and n_lat % 512 == 0 and n_lat % GRID_W == 0

    cos2, sin2 = _rope_tables(n_ctx, n_lat)
    tabs, tabs_c = _dft_tables(n_lat), _dft_tables(n_ctx)
    cvec = jax.nn.silu(jnp.concatenate([c, c_ctx[None]], axis=0))
    mods = [(_matmul(cvec, w_ada, i, F32, 8, 512, "adaln_mod") + b_ada[i]).reshape(nb + 1, 3, d)
            for i in range(DEPTH)]

    xx, h = _ln_mod(x, ctx, mods[0])
    out = None
    for i in range(DEPTH):
        last = i == DEPTH - 1
        p = _in_proj(h, w_in, i, tb, n_ctx, not last)
        p3 = p.reshape(nb, tb, P_WIDTH)
        g_ssd = _ssd(p3, ssd_conv_w[i], ssd_conv_b[i], ssd_dt_bias[i], ssd_a_log[i], ssd_d[i], ssd_norm_w[i],
                     n_ctx).reshape(tt, SSD_WIDTH)
        qm, km, vm, kg, vg = _attn_prep(p, cos2, sin2, mla_q_norm[i], mla_w_uq[i], mla_kv_norm[i], mla_w_ukv[i],
                                        tps)
        r3 = lambda a: a.reshape(nb, tb, a.shape[-1])
        g_mla = _mla(r3(qm), r3(km), r3(vm), p3, n_ctx).reshape(tt, MLA_WIDTH)
        g_gqa = _gqa(p, kg, vg, cos2, sin2, gqa_sink[i], nb, tb, n_ctx)
        fargs = (hy_w1[i], hy_b1[i], hy_w2[i], hy_b2[i], hy_w3[i], hy_freq[i])
        filt = _hyena_filter(n_lat, *tabs, *fargs)
        filt_c = _hyena_filter(n_ctx, *tabs_c, *fargs)
        g_hy = _hyena(p3, hy_conv_w[i], hy_conv_b[i], hy_d[i], tabs, filt, tabs_c, filt_c, n_ctx,
                      not last).reshape(tt, HY_WIDTH)
        mixed = _mix((g_ssd, g_mla, g_gqa, g_hy), p, w_branch[i].astype(BF16))
        w_out_bf = w_out[i].astype(BF16)
        if last:
            out = _out_last(mixed, w_out_bf, xx, mods[i], ln_g[i], ln_b[i], tps, nb, n_ctx)
        else:
            xx, h = _out_next(mixed, w_out_bf, xx, mods[i], ln_g[i], ln_b[i], mods[i + 1], tps, nb)
    return out
```

```python
import functools
import math

import jax
import jax.numpy as jnp
from jax import lax
from jax.experimental import pallas as pl
from jax.experimental.pallas import tpu as pltpu

D_MODEL = 2048
DEPTH = 2
GRID_W = 64
N_BRANCH = 4
BRANCH_WIDTH = D_MODEL // 2
ROPE_DIM = 64
ROPE_BASE = 10000.0
SSD_WIDTH = BRANCH_WIDTH
SSD_HEAD_DIM = 64
SSD_HEADS = SSD_WIDTH // SSD_HEAD_DIM
SSD_GROUPS = 2
SSD_STATE = 128
SSD_CHUNK = 128
SSD_CONV_CH = SSD_WIDTH + 2 * SSD_GROUPS * SSD_STATE
MLA_HEADS = 8
MLA_NOPE = 128
MLA_ROPE = ROPE_DIM
MLA_V = 128
MLA_Q_LORA = 512
MLA_KV_LORA = 256
MLA_WIDTH = MLA_HEADS * MLA_V
GQA_HEADS = 16
GQA_KV_HEADS = 2
GQA_HEAD_DIM = ROPE_DIM
GQA_WIDTH = GQA_HEADS * GQA_HEAD_DIM
GQA_KV_WIDTH = GQA_KV_HEADS * GQA_HEAD_DIM
WINDOW = 128
HY_WIDTH = BRANCH_WIDTH
HY_POS_EMB = 33
HY_BANDS = (HY_POS_EMB - 1) // 2
HY_FILTER_HIDDEN = 64
HY_FAST_DECAY = 0.3
HY_SLOW_DECAY = 1.5
HY_DECAY_TARGET = 0.01
LN_EPS = 1e-6
NEG_INF = -1e30
DEEPNORM_ALPHA = (2 * DEPTH) ** 0.25

IN_SPLITS = (
    SSD_WIDTH, SSD_CONV_CH, 2 * SSD_HEADS, MLA_Q_LORA, MLA_KV_LORA, MLA_ROPE, MLA_WIDTH, GQA_WIDTH,
    GQA_KV_WIDTH, GQA_KV_WIDTH, GQA_WIDTH, 3 * HY_WIDTH, HY_WIDTH, N_BRANCH * D_MODEL,
)

LANE = 128
ROW_TILE = 256
CHUNK = SSD_CHUNK
VMEM_LIMIT_BYTES = 56 * 1024 * 1024

P_MERGE = 0
P_HY_X0 = 8192
P_HY_X1 = 9216
P_HY_V = 10240
P_HY_Z = 11264
P_SSD_Z = 12288
P_SSD_X = 13312
P_MLA_Z = 14336
P_GQA_Q = 15360
P_GQA_Z = 16384
P_MLA_CQ = 17408
P_SSD_B = 17920
P_SSD_C = 18176
P_MLA_CKV = 18432
P_GQA_K = 18688
P_GQA_V = 18816
P_SMALL = 18944
P_WIDTH = 19072
SMALL_DT = 64

F32 = jnp.float32
BF16 = jnp.bfloat16


def _cparams(*sem):
    return pltpu.CompilerParams(dimension_semantics=sem, vmem_limit_bytes=VMEM_LIMIT_BYTES)


def _split3(v):
    h1 = v.astype(BF16)
    r1 = v - h1.astype(F32)
    h2 = r1.astype(BF16)
    h3 = (r1 - h2.astype(F32)).astype(BF16)
    return h1, h2, h3


def _dot(a, b):
    return jnp.dot(a, b, preferred_element_type=F32)


def _dot_nt(a, b):
    return lax.dot_general(a, b, (((1,), (1,)), ((), ())), preferred_element_type=F32)


def _dot_tn(a, b):
    return lax.dot_general(a, b, (((0,), (0,)), ((), ())), preferred_element_type=F32)


def _dot_exact_rhs(a, b_bf16, pieces):
    parts = _split3(a)[:pieces]
    out = _dot(parts[0], b_bf16)
    for p in parts[1:]:
        out = out + _dot(p, b_bf16)
    return out


def _dot_exact_lhs(a_bf16, b):
    b1, b2, b3 = _split3(b)
    return _dot(a_bf16, b1) + (_dot(a_bf16, b2) + _dot(a_bf16, b3))


def _dot3(a, b):
    a1, a2, _ = _split3(a)
    b1, b2, _ = _split3(b)
    return _dot(a1, b1) + (_dot(a1, b2) + _dot(a2, b1))


def _silu(v):
    return v * jax.nn.sigmoid(v)


def _softplus(v):
    return jnp.maximum(v, 0.0) + jnp.log(1.0 + jnp.exp(-jnp.abs(v)))


def _rotswap(t):
    lane = lax.broadcasted_iota(jnp.int32, t.shape, 1)
    return jnp.where((lane & 63) < 32, pltpu.roll(t, 96, 1), pltpu.roll(t, 32, 1))


def _rope(t, cos, sin_signed):
    return t * cos + _rotswap(t) * sin_signed


def _mm_kernel(a_ref, b_ref, o_ref):
    o_ref[...] = _dot(a_ref[...].astype(BF16), b_ref[...].astype(BF16)).astype(o_ref.dtype)


def _matmul(a, b, layer, out_dtype, tm, tn, name):
    m, k = a.shape
    n = b.shape[2]
    tm = min(tm, m)
    tn = min(tn, n)
    return pl.pallas_call(
        _mm_kernel,
        grid=(pl.cdiv(m, tm), pl.cdiv(n, tn)),
        in_specs=[pl.BlockSpec((tm, k), lambda i, j: (i, 0)),
                  pl.BlockSpec((None, k, tn), lambda i, j: (layer, 0, j))],
        out_specs=pl.BlockSpec((tm, tn), lambda i, j: (i, j)),
        out_shape=jax.ShapeDtypeStruct((m, n), out_dtype),
        compiler_params=_cparams("parallel", "arbitrary"),
        name=name,
    )(a, b)


def _mod_row(i, tiles_per_sample, n_samples):
    return jnp.where(i % tiles_per_sample == 0, n_samples, i // tiles_per_sample)


def _ln_mod_kernel(ctx_ref, x_ref, mod_ref, xx_ref, h_ref):
    def emit(x):
        xx_ref[...] = x
        mu = jnp.mean(x, axis=-1, keepdims=True)
        xc = x - mu
        var = jnp.mean(xc * xc, axis=-1, keepdims=True)
        m = mod_ref[0]
        h_ref[...] = (xc * lax.rsqrt(var + LN_EPS) * (1.0 + m[1:2]) + m[0:1]).astype(h_ref.dtype)

    @pl.when(pl.program_id(1) == 0)
    def _():
        emit(ctx_ref[...])

    @pl.when(pl.program_id(1) != 0)
    def _():
        emit(x_ref[...])


def _ln_mod(x, ctx, mod):
    nb, n_lat, d = x.shape
    assert ctx.shape[1] == ROW_TILE
    tps = 1 + n_lat // ROW_TILE
    rows = pl.BlockSpec((ROW_TILE, d), lambda b, i: (b * tps + i, 0))
    return pl.pallas_call(
        _ln_mod_kernel,
        grid=(nb, tps),
        in_specs=[pl.BlockSpec((None, ROW_TILE, d), lambda b, i: (b, 0, 0)),
                  pl.BlockSpec((None, ROW_TILE, d), lambda b, i: (b, jnp.maximum(i - 1, 0), 0)),
                  pl.BlockSpec((1, 3, d), lambda b, i: (jnp.where(i == 0, nb, b), 0, 0))],
        out_specs=[rows, rows],
        out_shape=[jax.ShapeDtypeStruct((nb * tps * ROW_TILE, d), F32),
                   jax.ShapeDtypeStruct((nb * tps * ROW_TILE, d), BF16)],
        compiler_params=_cparams("parallel", "arbitrary"),
        name="ln_modulate",
    )(ctx, x, mod)


IN_TILE = 1024
IN_ROWS = 1152
SUBLANE = 8


def _in_proj_plan():
    off, start = [], 0
    for size in IN_SPLITS:
        off.append(start)
        start += size
    (o_sz, o_xbc, o_dt, o_cq, o_ckv, o_kpe, o_mz, o_gq, o_gk, o_gv, o_gz, o_hy, o_hz, o_mg) = off
    wide = [(P_MERGE, o_mg, N_BRANCH * D_MODEL), (P_HY_X0, o_hy, 3 * HY_WIDTH), (P_HY_Z, o_hz, HY_WIDTH),
            (P_SSD_Z, o_sz, SSD_WIDTH), (P_SSD_X, o_xbc, SSD_WIDTH), (P_MLA_Z, o_mz, MLA_WIDTH),
            (P_GQA_Q, o_gq, GQA_WIDTH), (P_GQA_Z, o_gz, GQA_WIDTH)]
    n_tiles = -(-P_WIDTH // IN_TILE)
    src = [None] * n_tiles
    for p_off, s_off, width in wide:
        for t in range(width // IN_TILE):
            src[p_off // IN_TILE + t] = s_off + t * IN_TILE
    tail_tiles = [j for j in range(n_tiles) if src[j] is None]
    assert tail_tiles[0] * IN_TILE == P_MLA_CQ
    tail_src = [(o_cq, MLA_Q_LORA), (o_xbc + SSD_WIDTH, 2 * SSD_GROUPS * SSD_STATE), (o_ckv, MLA_KV_LORA),
                (o_gk, 2 * GQA_KV_WIDTH), (o_kpe, MLA_ROPE), (o_dt, 2 * SSD_HEADS)]
    return src, tail_tiles, tail_src


def _in_proj_kernel(start_ref, tail_idx_ref, a_ref, win_ref, tail_ref, o_ref, wb, *, n_ctx, tiles_per_sample):
    @pl.when(pl.program_id(1) == 0)
    def _():
        from_tail = tail_idx_ref[pl.program_id(0)] >= 0

        @pl.when(jnp.logical_not(from_tail))
        def _():
            wb[...] = win_ref[0].astype(wb.dtype)

        @pl.when(from_tail)
        def _():
            wb[...] = tail_ref[...].astype(wb.dtype)

    skip = jnp.logical_and(tail_idx_ref[pl.program_id(0)] == -2, pl.program_id(1) % tiles_per_sample == 0)

    @pl.when(skip)
    def _():
        o_ref[0:n_ctx, :] = jnp.zeros((n_ctx, o_ref.shape[1]), o_ref.dtype)
        o_ref[n_ctx:, :] = _dot_nt(a_ref[n_ctx:, :], wb[...]).astype(o_ref.dtype)

    @pl.when(jnp.logical_not(skip))
    def _():
        o_ref[...] = _dot_nt(a_ref[...], wb[...]).astype(o_ref.dtype)


def _in_proj(h, w_in, layer, tb, n_ctx, ctx_outputs):
    tt, d = h.shape
    w_t = jnp.swapaxes(w_in, 1, 2)
    src, tail_tiles, tail_src = _in_proj_plan()
    tail = jnp.concatenate([w_t[layer, s:s + w, :] for s, w in tail_src], axis=0)
    tail = jnp.pad(tail, ((0, len(tail_tiles) * IN_TILE - tail.shape[0]), (0, 0)))
    assert all(s is None or s % SUBLANE == 0 for s in src)
    rows = IN_ROWS if tb % IN_ROWS == 0 else tb // 2
    assert tb % rows == 0 and rows % (2 * SUBLANE) == 0 and rows > n_ctx
    starts = [0 if s is None else s // SUBLANE for s in src]
    wide_code = lambda j: -1 if ctx_outputs or j == P_SSD_X // IN_TILE else -2
    tidx = [tail_tiles.index(j) if s is None else wide_code(j) for j, s in enumerate(src)]
    as_i32 = lambda v: jnp.asarray(v, jnp.int32)
    gs = pltpu.PrefetchScalarGridSpec(
        num_scalar_prefetch=2, grid=(len(src), tt // rows),
        in_specs=[pl.BlockSpec((rows, d), lambda j, i, st, ti: (i, 0)),
                  pl.BlockSpec((pl.Element(1), pl.Element(IN_TILE), pl.Element(d)),
                               lambda j, i, st, ti: (layer, st[j] * SUBLANE, 0)),
                  pl.BlockSpec((IN_TILE, d), lambda j, i, st, ti: (jnp.maximum(ti[j], 0), 0),
                               pipeline_mode=pl.Buffered(1))],
        out_specs=pl.BlockSpec((rows, IN_TILE), lambda j, i, st, ti: (i, j)),
        scratch_shapes=[pltpu.VMEM((IN_TILE, d), BF16)])
    return pl.pallas_call(
        functools.partial(_in_proj_kernel, n_ctx=n_ctx, tiles_per_sample=tb // rows), grid_spec=gs,
        out_shape=jax.ShapeDtypeStruct((tt, P_WIDTH), BF16),
        compiler_params=_cparams("parallel", "arbitrary"),
        name="in_proj",
    )(as_i32(starts), as_i32(tidx), h, w_t, tail)


def _conv_chunk(src_ref, w, bias, c, n_ctx_chunks, n_chunks):
    tb = n_chunks * CHUNK
    r0 = pl.multiple_of(c * CHUNK, CHUNK)
    cur = src_ref[pl.ds(r0, CHUNK), :].astype(F32)
    rp = pl.multiple_of(jnp.maximum(r0 - 16, 0), 16)
    rn = pl.multiple_of(jnp.minimum(r0 + CHUNK, tb - 16), 16)
    prev_row = src_ref[pl.ds(rp, 16), :].astype(F32)[15:16]
    next_row = src_ref[pl.ds(rn, 16), :].astype(F32)[0:1]
    has_prev = jnp.logical_and(c != 0, c != n_ctx_chunks)
    has_next = jnp.logical_and(c != n_ctx_chunks - 1, c != n_chunks - 1)
    prev_row = jnp.where(has_prev, prev_row, 0.0)
    next_row = jnp.where(has_next, next_row, 0.0)
    row = lax.broadcasted_iota(jnp.int32, cur.shape, 0)
    sp = jnp.where(row == 0, prev_row, pltpu.roll(cur, 1, 0))
    sn = jnp.where(row == CHUNK - 1, next_row, pltpu.roll(cur, CHUNK - 1, 0))
    return sp * w[0:1] + cur * w[1:2] + sn * w[2:3] + bias


def _ssd_kernel(x_ref, b_ref, c_ref, sm_ref, z_ref, cwx_ref, cbx_ref, cwb_ref, cbb_ref, cwc_ref, cbc_ref,
                dtb_ref, alog_ref, dsk_ref, nw_ref, o_ref, ux, ub, uc, yacc, ybwd, st, st_b,
                *, n_ctx_chunks, n_chunks):
    g = pl.program_id(1)
    hp = ux.shape[1]
    nh = hp // SSD_HEAD_DIM

    def conv_body(c, carry):
        r0 = pl.multiple_of(c * CHUNK, CHUNK)
        ux[pl.ds(r0, CHUNK), :] = _silu(_conv_chunk(x_ref, cwx_ref[...], cbx_ref[...], c, n_ctx_chunks,
                                                    n_chunks)).astype(ux.dtype)
        ub[pl.ds(r0, CHUNK), :] = _silu(_conv_chunk(b_ref, cwb_ref[...], cbb_ref[...], c, n_ctx_chunks,
                                                    n_chunks)).astype(ub.dtype)
        uc[pl.ds(r0, CHUNK), :] = _silu(_conv_chunk(c_ref, cwc_ref[...], cbc_ref[...], c, n_ctx_chunks,
                                                    n_chunks)).astype(uc.dtype)
        return carry

    lax.fori_loop(0, n_chunks, conv_body, 0)

    ri = lax.broadcasted_iota(jnp.int32, (CHUNK, CHUNK), 0)
    ci = lax.broadcasted_iota(jnp.int32, (CHUNK, CHUNK), 1)
    lane128 = lax.broadcasted_iota(jnp.int32, (CHUNK, LANE), 1)
    a_all = -jnp.exp(alog_ref[0]) * math.log2(math.e)
    dtb = dtb_ref[0]

    def direction_consts(dirn):
        tri = (ci <= ri) if dirn == 0 else (ci >= ri)
        tri_bf = jnp.where(tri, 1.0, 0.0).astype(BF16)
        expand = jnp.where(lax.broadcasted_iota(jnp.int32, (LANE, hp), 0)
                           == dirn * 16 + (lax.broadcasted_iota(jnp.int32, (LANE, hp), 1) >> 6),
                           1.0, 0.0).astype(BF16)
        return tri, tri_bf, expand

    consts = (direction_consts(0), direction_consts(1))

    def chunk_step(dirn, c, st):
        tri, tri_bf, expand = consts[dirn]
        last = CHUNK - 1 if dirn == 0 else 0
        r0 = pl.multiple_of(c * CHUNK, CHUNK)
        xs = ux[pl.ds(r0, CHUNK), :].astype(F32)
        bc = ub[pl.ds(r0, CHUNK), :]
        cc = uc[pl.ds(r0, CHUNK), :]
        sm = sm_ref[pl.ds(r0, CHUNK), :].astype(F32)
        sm = jnp.where(g == 0, pltpu.roll(sm, LANE - SMALL_DT, 1), pltpu.roll(sm, LANE - SMALL_DT - nh, 1))
        dt = _softplus(sm + dtb)
        la = dt * a_all
        cum = _dot_exact_lhs(tri_bf, la)
        cx = _dot_exact_rhs(cum, expand, 2)
        dtx = _dot_exact_rhs(dt, expand, 2)
        cx_last = cx[last:last + 1]
        ecum = jnp.exp2(cx)
        dec_end = jnp.exp2(cx_last - cx)
        xd = xs * dtx
        xd_bf = xd.astype(BF16)
        gmat = _dot_nt(cc, bc)
        cum_t = cum.T
        cols = []
        for j in range(nh // 2):
            sc = []
            for h in (2 * j, 2 * j + 1):
                k = dirn * 16 + h
                diff = cum[:, k:k + 1] - cum_t[k:k + 1, :]
                lmat = jnp.exp2(jnp.where(tri, diff, NEG_INF))
                sc.append((gmat * lmat).astype(BF16))
            xcol = xd_bf[:, j * LANE:(j + 1) * LANE]
            lo = lane128 < SSD_HEAD_DIM
            rhs = jnp.concatenate([jnp.where(lo, xcol, jnp.zeros_like(xcol)),
                                   jnp.where(lo, jnp.zeros_like(xcol), xcol)], axis=0)
            cols.append(_dot(jnp.concatenate(sc, axis=1), rhs))
        y = jnp.concatenate(cols, axis=1)
        s_prev = st[...]
        y = y + _dot(cc, s_prev.astype(BF16)) * ecum
        st[...] = jnp.exp2(cx_last) * s_prev + _dot_tn(bc, (xd * dec_end).astype(BF16))
        return y, xs, r0

    st[...] = jnp.zeros_like(st)
    st_b[...] = jnp.zeros_like(st_b)

    def scan_body(i, carry):
        y_f, xs_f, r0_f = chunk_step(0, i, st)
        yacc[pl.ds(r0_f, CHUNK), :] = y_f + xs_f * dsk_ref[...]
        c_b = jnp.where(i < n_ctx_chunks, n_ctx_chunks - 1 - i, n_chunks - 1 + n_ctx_chunks - i)
        y_b, _, r0_b = chunk_step(1, c_b, st_b)
        ybwd[pl.ds(r0_b, CHUNK), :] = y_b
        return carry

    lax.fori_loop(0, n_chunks, scan_body, 0, unroll=3)

    def gate_body(c, carry):
        r0 = pl.multiple_of(c * CHUNK, CHUNK)
        tot = (yacc[pl.ds(r0, CHUNK), :] + ybwd[pl.ds(r0, CHUNK), :]) * _silu(z_ref[pl.ds(r0, CHUNK), :].astype(F32))
        ms = jnp.mean(tot * tot, axis=-1, keepdims=True)
        o_ref[pl.ds(r0, CHUNK), :] = (tot * lax.rsqrt(ms + LN_EPS) * nw_ref[...]).astype(o_ref.dtype)
        return carry

    lax.fori_loop(0, n_chunks, gate_body, 0)


def _ssd(p3, conv_w, conv_b, dt_bias, a_log, d_skip, norm_w, n_ctx):
    nb, tb, _ = p3.shape
    hp = SSD_WIDTH // SSD_GROUPS
    n_chunks = tb // CHUNK
    gn = SSD_GROUPS * SSD_STATE
    cwx, cwb, cwc = conv_w[:, :SSD_WIDTH], conv_w[:, SSD_WIDTH:SSD_WIDTH + gn], conv_w[:, SSD_WIDTH + gn:]
    cb = conv_b[None, :]
    cbx, cbb, cbc = cb[:, :SSD_WIDTH], cb[:, SSD_WIDTH:SSD_WIDTH + gn], cb[:, SSD_WIDTH + gn:]
    hg = SSD_HEADS // SSD_GROUPS

    def per_group(v):
        v = v.reshape(2, SSD_GROUPS, hg).transpose(1, 0, 2)
        v = jnp.pad(v, ((0, 0), (0, 0), (0, 16 - hg))).reshape(SSD_GROUPS, 1, 32)
        return jnp.pad(v, ((0, 0), (0, 0), (0, LANE - 32)))

    dsk = jnp.repeat(d_skip, SSD_HEAD_DIM)[None, :]
    kern = functools.partial(_ssd_kernel, n_ctx_chunks=n_ctx // CHUNK, n_chunks=n_chunks)
    col = lambda off, w: (lambda b, g: (b, 0, off // w + g))
    par = lambda b, g: (0, g)
    return pl.pallas_call(
        kern,
        grid=(nb, SSD_GROUPS),
        in_specs=[pl.BlockSpec((None, tb, hp), col(P_SSD_X, hp)),
                  pl.BlockSpec((None, tb, SSD_STATE), col(P_SSD_B, SSD_STATE)),
                  pl.BlockSpec((None, tb, SSD_STATE), col(P_SSD_C, SSD_STATE)),
                  pl.BlockSpec((None, tb, LANE), lambda b, g: (b, 0, P_SMALL // LANE)),
                  pl.BlockSpec((None, tb, hp), col(P_SSD_Z, hp)),
                  pl.BlockSpec((3, hp), par), pl.BlockSpec((1, hp), par),
                  pl.BlockSpec((3, SSD_STATE), par), pl.BlockSpec((1, SSD_STATE), par),
                  pl.BlockSpec((3, SSD_STATE), par), pl.BlockSpec((1, SSD_STATE), par),
                  pl.BlockSpec((1, 1, LANE), lambda b, g: (g, 0, 0)),
                  pl.BlockSpec((1, 1, LANE), lambda b, g: (g, 0, 0)),
                  pl.BlockSpec((1, hp), par), pl.BlockSpec((1, hp), par)],
        out_specs=pl.BlockSpec((None, tb, hp), lambda b, g: (b, 0, g)),
        out_shape=jax.ShapeDtypeStruct((nb, tb, SSD_WIDTH), BF16),
        scratch_shapes=[pltpu.VMEM((tb, hp), BF16), pltpu.VMEM((tb, SSD_STATE), BF16),
                        pltpu.VMEM((tb, SSD_STATE), BF16), pltpu.VMEM((tb, hp), F32),
                        pltpu.VMEM((tb, hp), F32), pltpu.VMEM((SSD_STATE, hp), F32),
                        pltpu.VMEM((SSD_STATE, hp), F32)],
        compiler_params=_cparams("parallel", "parallel"),
        name="ssd_bidir",
    )(p3, p3, p3, p3, p3, cwx, cbx, cwb, cbb, cwc, cbc, per_group(dt_bias), per_group(a_log), dsk,
      norm_w[None, :])


def _rope_tables(n_ctx, n_lat):
    rows = n_lat // GRID_W
    row = jnp.repeat(jnp.arange(rows, dtype=F32), GRID_W)
    colp = jnp.tile(jnp.arange(GRID_W, dtype=F32), rows)
    n_freq = ROPE_DIM // 4
    inv_freq = ROPE_BASE ** (-jnp.arange(n_freq, dtype=F32) / n_freq)
    ang = jnp.concatenate([row[:, None] * inv_freq[None, :], colp[:, None] * inv_freq[None, :]], axis=-1)
    cos, sin = jnp.cos(ang), jnp.sin(ang)
    cos2 = jnp.tile(cos, (1, 4))
    sin2 = jnp.tile(jnp.concatenate([-sin, sin], axis=-1), (1, 2))
    cos2 = jnp.concatenate([jnp.ones((n_ctx, LANE), F32), cos2], axis=0)
    sin2 = jnp.concatenate([jnp.zeros((n_ctx, LANE), F32), sin2], axis=0)
    return cos2, sin2


def _attn_prep_kernel(cq_ref, ckv_ref, sm_ref, gk_ref, gv_ref, cos_ref, sin_ref, qn_ref, wq_ref, kn_ref,
                      wk_ref, wv_ref, q_ref, k_ref, v_ref, kg_ref, vg_ref):
    cos, sin = cos_ref[...], sin_ref[...]
    lane = lax.broadcasted_iota(jnp.int32, cos.shape, 1)
    scale = (MLA_NOPE + MLA_ROPE) ** -0.5 * math.log2(math.e)
    cq = cq_ref[...].astype(F32)
    cqn = cq * lax.rsqrt(jnp.mean(cq * cq, axis=-1, keepdims=True) + LN_EPS) * qn_ref[...]
    q = _dot(cqn.astype(BF16), wq_ref[...])
    ckv = ckv_ref[...].astype(F32)
    ckvn = (ckv * lax.rsqrt(jnp.mean(ckv * ckv, axis=-1, keepdims=True) + LN_EPS) * kn_ref[...]).astype(BF16)
    kn = _dot(ckvn, wk_ref[...])
    v_ref[...] = _dot(ckvn, wv_ref[...]).astype(v_ref.dtype)
    kpe = jnp.where(lane < MLA_ROPE, _rope(sm_ref[...].astype(F32), cos, sin), 0.0).astype(k_ref.dtype)
    for h in range(MLA_HEADS):
        o = 2 * h * LANE
        q_ref[:, o:o + LANE] = (q[:, o:o + LANE] * scale).astype(q_ref.dtype)
        q_ref[:, o + LANE:o + 2 * LANE] = (_rope(q[:, o + LANE:o + 2 * LANE], cos, sin) * scale).astype(q_ref.dtype)
        k_ref[:, o:o + LANE] = kn[:, h * LANE:(h + 1) * LANE].astype(k_ref.dtype)
        k_ref[:, o + LANE:o + 2 * LANE] = kpe
    kg_ref[...] = _rope(gk_ref[...].astype(F32), cos, sin).astype(kg_ref.dtype)
    vg_ref[...] = gv_ref[...].astype(F32).T.astype(vg_ref.dtype)


def _attn_prep(p, cos2, sin2, q_norm, w_uq, kv_norm, w_ukv, tps):
    tt = p.shape[0]
    hq = MLA_NOPE + MLA_ROPE
    wq = w_uq.reshape(MLA_Q_LORA, MLA_HEADS, hq)
    wq = jnp.pad(wq, ((0, 0), (0, 0), (0, 2 * LANE - hq))).reshape(MLA_Q_LORA, MLA_HEADS * 2 * LANE).astype(BF16)
    wkv = w_ukv.reshape(MLA_KV_LORA, MLA_HEADS, MLA_NOPE + MLA_V)
    wk = wkv[:, :, :MLA_NOPE].reshape(MLA_KV_LORA, MLA_HEADS * MLA_NOPE).astype(BF16)
    wv = wkv[:, :, MLA_NOPE:].reshape(MLA_KV_LORA, MLA_HEADS * MLA_V).astype(BF16)
    full = lambda shape: pl.BlockSpec(shape, lambda i: (0, 0))
    pcol = lambda off, w: pl.BlockSpec((ROW_TILE, w), lambda i: (i, off // w))
    rows = lambda w: pl.BlockSpec((ROW_TILE, w), lambda i: (i, 0))
    tab = pl.BlockSpec((ROW_TILE, LANE), lambda i: (i % tps, 0))
    return pl.pallas_call(
        _attn_prep_kernel,
        grid=(tt // ROW_TILE,),
        in_specs=[pcol(P_MLA_CQ, MLA_Q_LORA), pcol(P_MLA_CKV, MLA_KV_LORA), pcol(P_SMALL, LANE),
                  pcol(P_GQA_K, LANE), pcol(P_GQA_V, LANE), tab, tab,
                  full((1, MLA_Q_LORA)), full(wq.shape), full((1, MLA_KV_LORA)), full(wk.shape), full(wv.shape)],
        out_specs=[rows(MLA_HEADS * 2 * LANE), rows(MLA_HEADS * 2 * LANE), rows(MLA_WIDTH),
                   rows(LANE), pl.BlockSpec((LANE, ROW_TILE), lambda i: (0, i))],
        out_shape=[jax.ShapeDtypeStruct((tt, MLA_HEADS * 2 * LANE), BF16),
                   jax.ShapeDtypeStruct((tt, MLA_HEADS * 2 * LANE), BF16),
                   jax.ShapeDtypeStruct((tt, MLA_WIDTH), BF16),
                   jax.ShapeDtypeStruct((tt, LANE), BF16),
                   jax.ShapeDtypeStruct((LANE, tt), BF16)],
        compiler_params=_cparams("parallel"),
        name="attn_prep",
    )(p, p, p, p, p, cos2, sin2, q_norm[None, :], wq, kv_norm[None, :], wk, wv)


MLA_Q_CHUNK = 256


def _mla_kernel(q_ref, k_ref, v_ref, z_ref, o_ref, *, n_ctx):
    tb = k_ref.shape[0]
    ones_col = jnp.where(lax.broadcasted_iota(jnp.int32, (tb, LANE), 1) == 0, 1.0, 0.0).astype(BF16)
    v_ext = jnp.concatenate([v_ref[...], ones_col], axis=1)

    def attend(r0, rows, n_keys):
        s = _dot_nt(q_ref[r0:r0 + rows, :], k_ref[0:n_keys, :])
        m = jnp.max(s, axis=-1, keepdims=True)
        p = jnp.exp2(s - m).astype(BF16)
        o_ext = _dot(p, v_ext[0:n_keys])
        o = o_ext[:, :MLA_V] / o_ext[:, MLA_V:MLA_V + 1]
        o_ref[r0:r0 + rows, :] = (o * _silu(z_ref[r0:r0 + rows, :].astype(F32))).astype(o_ref.dtype)

    attend(0, n_ctx, n_ctx)
    for r0 in range(n_ctx, tb, MLA_Q_CHUNK):
        attend(r0, MLA_Q_CHUNK, tb)


def _mla(qm3, km3, vm3, p3, n_ctx):
    nb, tb, _ = p3.shape
    assert (tb - n_ctx) % MLA_Q_CHUNK == 0
    blk = lambda w, off: pl.BlockSpec((None, tb, w), lambda b, h: (b, 0, off // w + h))
    return pl.pallas_call(
        functools.partial(_mla_kernel, n_ctx=n_ctx),
        grid=(nb, MLA_HEADS),
        in_specs=[blk(2 * LANE, 0), blk(2 * LANE, 0), blk(LANE, 0), blk(LANE, P_MLA_Z)],
        out_specs=blk(LANE, 0),
        out_shape=jax.ShapeDtypeStruct((nb, tb, MLA_WIDTH), BF16),
        compiler_params=_cparams("parallel", "parallel"),
        name="mla_attention",
    )(qm3, km3, vm3, p3)


GQA_HEAD_BATCH = 8

def _gqa_kernel(sink_ref, q_ref, kp_ref, ko_ref, kn_ref, kc_ref, vp_ref, vo_ref, vn_ref, vc_ref, cos_ref,
                sin_ref, z_ref, o_ref, *, n_ctx_blocks):
    blk = pl.program_id(1)
    n_blk = pl.num_programs(1)
    is_lat = blk >= n_ctx_blocks
    has_prev = blk > n_ctx_blocks
    has_next = jnp.logical_and(is_lat, blk < n_blk - 1)
    cos, sin = cos_ref[...], sin_ref[...]
    log2e = math.log2(math.e)
    scale = GQA_HEAD_DIM ** -0.5 * log2e
    w = q_ref.shape[0]
    hd = GQA_HEAD_DIM
    hpk = GQA_HEADS // GQA_KV_HEADS
    q = jnp.concatenate([_rope(q_ref[:, c * LANE:(c + 1) * LANE].astype(F32), cos, sin) * scale
                         for c in range(GQA_WIDTH // LANE)], axis=1)
    qt = q.T.astype(BF16)
    kall = jnp.concatenate([kp_ref[...], ko_ref[...], kn_ref[...], kc_ref[...]], axis=0)
    vt = jnp.concatenate([vp_ref[...], vo_ref[...], vn_ref[...], vc_ref[...]], axis=1)
    kj = lax.broadcasted_iota(jnp.int32, (w, w), 0)
    qi = lax.broadcasted_iota(jnp.int32, (w, w), 1)
    m_prev = jnp.logical_and(kj >= qi, has_prev)
    m_next = jnp.logical_and(kj <= qi, has_next)
    zeros = jnp.zeros((hd, w), BF16)
    ones_rows = jnp.ones((16, kall.shape[0]), BF16)
    nbat = GQA_HEAD_BATCH
    tile = lambda a: jnp.concatenate([a] * nbat, axis=1)
    m_prev, m_next = tile(m_prev), tile(m_next)
    outs = []
    for h0 in range(0, GQA_HEADS, nbat):
        kh = h0 // hpk
        heads = range(h0, h0 + nbat)
        rhs = jnp.concatenate(
            [jnp.concatenate([qt[h * hd:(h + 1) * hd], zeros] if kh == 0 else [zeros, qt[h * hd:(h + 1) * hd]],
                             axis=0) for h in heads], axis=1)
        st = _dot(kall, rhs)
        s_p = jnp.where(m_prev, st[0:w], NEG_INF)
        s_o = jnp.where(is_lat, st[w:2 * w], NEG_INF)
        s_n = jnp.where(m_next, st[2 * w:3 * w], NEG_INF)
        s = jnp.concatenate([s_p, s_o, s_n, st[3 * w:]], axis=0)
        sink = jnp.concatenate([jnp.full((1, w), sink_ref[h] * log2e, F32) for h in heads], axis=1)
        m = jnp.maximum(jnp.max(s, axis=0, keepdims=True), sink)
        pr = jnp.exp2(s - m).astype(BF16)
        ot = _dot(jnp.concatenate([vt[kh * hd:(kh + 1) * hd], ones_rows], axis=0), pr)
        l = ot[hd:hd + 1] + jnp.exp2(sink - m)
        ot = ot[:hd] / l
        outs.extend(ot[:, i * w:(i + 1) * w] for i in range(nbat))
    y = jnp.concatenate(outs, axis=0).T
    o_ref[...] = (y * _silu(z_ref[...].astype(F32))).astype(o_ref.dtype)


def _gqa(p, kg, vgt, cos2, sin2, sink, nb, tb, n_ctx):
    w = WINDOW
    bps = tb // w
    row = lambda b, i: b * bps + i
    prev = lambda b, i: b * bps + jnp.maximum(i - 1, 0)
    nxt = lambda b, i: b * bps + jnp.minimum(i + 1, bps - 1)
    kspec = lambda f: pl.BlockSpec((w, LANE), lambda b, i: (f(b, i), 0))
    vspec = lambda f: pl.BlockSpec((LANE, w), lambda b, i: (0, f(b, i)))
    kctx = pl.BlockSpec((n_ctx, LANE), lambda b, i: (b * (tb // n_ctx), 0))
    vctx = pl.BlockSpec((LANE, n_ctx), lambda b, i: (0, b * (tb // n_ctx)))
    tab = pl.BlockSpec((w, LANE), lambda b, i: (i, 0))
    return pl.pallas_call(
        functools.partial(_gqa_kernel, n_ctx_blocks=n_ctx // w),
        grid=(nb, bps),
        in_specs=[pl.BlockSpec(memory_space=pltpu.SMEM),
                  pl.BlockSpec((w, GQA_WIDTH), lambda b, i: (row(b, i), P_GQA_Q // GQA_WIDTH)),
                  kspec(prev), kspec(row), kspec(nxt), kctx, vspec(prev), vspec(row), vspec(nxt), vctx, tab, tab,
                  pl.BlockSpec((w, GQA_WIDTH), lambda b, i: (row(b, i), P_GQA_Z // GQA_WIDTH))],
        out_specs=pl.BlockSpec((w, GQA_WIDTH), lambda b, i: (row(b, i), 0)),
        out_shape=jax.ShapeDtypeStruct((nb * tb, GQA_WIDTH), BF16),
        compiler_params=_cparams("parallel", "arbitrary"),
        name="gqa_window_attention",
    )(sink, p, kg, kg, kg, kg, vgt, vgt, vgt, vgt, cos2, sin2, p)


def _dft_tables(n):
    m = jnp.arange(n, dtype=jnp.int32)[None, :]

    def small(kvals):
        ang = ((kvals[:, None] * m) % (2 * n)).astype(F32) * (math.pi / n)
        return jnp.cos(ang), jnp.sin(ang)

    ca, sa = small(64 * jnp.arange(n // 64, dtype=jnp.int32))
    cb, sb = small(jnp.arange(64, dtype=jnp.int32))
    cos_t = ca[:, None, :] * cb[None, :, :] - sa[:, None, :] * sb[None, :, :]
    sin_t = sa[:, None, :] * cb[None, :, :] + ca[:, None, :] * sb[None, :, :]
    return cos_t.reshape(n, n).astype(BF16), sin_t.reshape(n, n).astype(BF16)


def _hyena_feats(n):
    t = jnp.linspace(0.0, 1.0, n, dtype=F32)[:, None]
    w_ang = (2.0 * math.pi / n) * jnp.arange(n, dtype=F32)[:, None]
    bands = jnp.linspace(1e-4, HY_BANDS - 1, HY_BANDS, dtype=F32)[None, :]
    feats = jnp.concatenate([t, jnp.cos(bands * w_ang), -jnp.sin(bands * w_ang)], axis=-1)
    return jnp.pad(feats, ((0, 0), (0, LANE - HY_POS_EMB)))


def _hyena_filter_kernel(feats_ref, w1_ref, b1_ref, w2_ref, b2_ref, w3f_ref, w3b_ref, freq_ref, dl_ref,
                         cos_ref, sin_ref, kr_ref, ks_ref, kq_ref, hs, hd, *, n):
    kt = pl.program_id(1)
    tk = cos_ref.shape[0]

    @pl.when(kt == 0)
    def _():
        feats = feats_ref[...]
        t = feats[:, 0:1]
        hdn = jnp.sin(freq_ref[0:1, :] * (_dot3(feats, w1_ref[...]) + b1_ref[...]))
        hdn = jnp.sin(freq_ref[1:2, :] * (_dot3(hdn, w2_ref[...]) + b2_ref[...]))
        decay = jnp.exp(-t * dl_ref[...])
        hf = _dot3(hdn, w3f_ref[...]) * decay
        hb = _dot3(hdn, w3b_ref[...]) * decay
        row = lax.broadcasted_iota(jnp.int32, hb.shape, 0)
        hb = jnp.where(row == 0, 0.0, hb)
        hsum = hf + hb
        hs[...] = hsum.astype(hs.dtype)
        hd[...] = (hf - hb).astype(hd.dtype)
        alt = jnp.where((row & 1) == 0, 1.0, -1.0)
        kq_ref[...] = jnp.sum(hsum * alt, axis=0, keepdims=True) * (1.0 / (2 * n))

    kidx = kt * tk + lax.broadcasted_iota(jnp.int32, (tk, 1), 0)
    wk = jnp.where(kidx == 0, 0.5 / n, 1.0 / n)
    kr_ref[...] = _dot(cos_ref[...], hs[...]) * wk
    ks_ref[...] = _dot(sin_ref[...], hd[...]) * wk


def _hyena_filter(n, cos_t, sin_t, w1, b1, w2, b2, w3, freq):
    cbw = 512
    tk = min(256, n)
    deltas = jnp.abs(jnp.linspace(math.log(HY_DECAY_TARGET) / HY_FAST_DECAY,
                                  math.log(HY_DECAY_TARGET) / HY_SLOW_DECAY, HY_WIDTH, dtype=F32))[None, :]
    w1p = jnp.pad(w1, ((0, LANE - HY_POS_EMB), (0, 0)))
    full = lambda a: pl.BlockSpec(a.shape, lambda c, k: (0,) * a.ndim)
    nblk = HY_WIDTH // cbw
    kern = functools.partial(_hyena_filter_kernel, n=n)
    feats = _hyena_feats(n)
    return pl.pallas_call(
        kern,
        grid=(nblk, n // tk),
        in_specs=[full(feats), full(w1p), pl.BlockSpec((1, HY_FILTER_HIDDEN), lambda c, k: (0, 0)),
                  full(w2), pl.BlockSpec((1, HY_FILTER_HIDDEN), lambda c, k: (0, 0)),
                  pl.BlockSpec((HY_FILTER_HIDDEN, cbw), lambda c, k: (0, c)),
                  pl.BlockSpec((HY_FILTER_HIDDEN, cbw), lambda c, k: (0, nblk + c)),
                  full(freq), pl.BlockSpec((1, cbw), lambda c, k: (0, c)),
                  pl.BlockSpec((tk, n), lambda c, k: (k, 0)), pl.BlockSpec((tk, n), lambda c, k: (k, 0))],
        out_specs=[pl.BlockSpec((tk, cbw), lambda c, k: (k, c)), pl.BlockSpec((tk, cbw), lambda c, k: (k, c)),
                   pl.BlockSpec((1, cbw), lambda c, k: (0, c))],
        out_shape=[jax.ShapeDtypeStruct((n, HY_WIDTH), F32), jax.ShapeDtypeStruct((n, HY_WIDTH), F32),
                   jax.ShapeDtypeStruct((1, HY_WIDTH), F32)],
        scratch_shapes=[pltpu.VMEM((n, cbw), BF16), pltpu.VMEM((n, cbw), BF16)],
        compiler_params=_cparams("parallel", "arbitrary"),
        name="hyena_filter",
    )(feats, w1p, b1[None, :], w2, b2[None, :], w3, w3, freq, deltas, cos_t, sin_t)


def _spectral_product(a, bq, kr, ks):
    return (a * kr - bq * ks).astype(BF16), (a * ks + bq * kr).astype(BF16)


HY_TILE = 512


def _hyena_kernel(x0_ref, x1_ref, v_ref, z_ref, cw0_ref, cb0_ref, cw1_ref, cb1_ref, cwv_ref, cbv_ref, d_ref,
                  cr_ref, sr_ref, kr_ref, ks_ref, kq_ref, ccx_ref, scx_ref, krx_ref, ksx_ref, kqx_ref,
                  o_ref, u, yr_s, ys_s, unyq, *, n_ctx, n_chunks, with_ctx):
    kt = pl.program_id(2)
    nk = pl.num_programs(2) // 2
    n_ctx_chunks = n_ctx // CHUNK
    tb = n_chunks * CHUNK
    tk = cr_ref.shape[0]

    def alt_sign(shape):
        return jnp.where((lax.broadcasted_iota(jnp.int32, shape, 0) & 1) == 0, 1.0, -1.0)

    def gate_chunk(c):
        x0c = _conv_chunk(x0_ref, cw0_ref[...], cb0_ref[...], c, n_ctx_chunks, n_chunks)
        r0 = pl.multiple_of(c * CHUNK, CHUNK)
        return x0c * _silu(z_ref[pl.ds(r0, CHUNK), :].astype(F32))

    @pl.when(kt == 0)
    def _():
        def conv_body(c, carry):
            r0 = pl.multiple_of(c * CHUNK, CHUNK)
            x1c = _conv_chunk(x1_ref, cw1_ref[...], cb1_ref[...], c, n_ctx_chunks, n_chunks)
            vc = _conv_chunk(v_ref, cwv_ref[...], cbv_ref[...], c, n_ctx_chunks, n_chunks)
            u[pl.ds(r0, CHUNK), :] = (x1c * vc).astype(u.dtype)
            return carry

        lax.fori_loop(0, n_chunks, conv_body, 0)
        ulf = u[n_ctx:tb, :].astype(F32)
        unyq[...] = jnp.sum(ulf * alt_sign(ulf.shape), axis=0, keepdims=True) * kq_ref[...]

    @pl.when(kt < nk)
    def _():
        ul = u[n_ctx:tb, :]
        yr, ys = _spectral_product(_dot(cr_ref[...], ul), _dot(sr_ref[...], ul), kr_ref[...], ks_ref[...])
        f0 = pl.multiple_of(kt * tk, tk)
        yr_s[pl.ds(f0, tk), :] = yr
        ys_s[pl.ds(f0, tk), :] = ys

    @pl.when(kt >= nk)
    def _():
        t = kt - nk
        y = _dot(cr_ref[...], yr_s[...]) + _dot(sr_ref[...], ys_s[...])
        for j in range(tk // CHUNK):
            c = n_ctx_chunks + t * (tk // CHUNK) + j
            r0 = pl.multiple_of(c * CHUNK, CHUNK)
            uf = u[pl.ds(r0, CHUNK), :].astype(F32)
            yj = y[j * CHUNK:(j + 1) * CHUNK] + alt_sign(uf.shape) * unyq[...] + uf * d_ref[...]
            o_ref[pl.ds(r0, CHUNK), :] = (yj * gate_chunk(c)).astype(o_ref.dtype)

    @pl.when(kt == 2 * nk - 1)
    def _():
        if with_ctx:
            uc = u[0:n_ctx, :]
            ucf = uc.astype(F32)
            alt = alt_sign(ucf.shape)
            yr, ys = _spectral_product(_dot(ccx_ref[...], uc), _dot(scx_ref[...], uc), krx_ref[...], ksx_ref[...])
            y = _dot(ccx_ref[...], yr) + _dot(scx_ref[...], ys)
            y = y + alt * (jnp.sum(ucf * alt, axis=0, keepdims=True) * kqx_ref[...]) + ucf * d_ref[...]
            for c in range(n_ctx_chunks):
                o_ref[c * CHUNK:(c + 1) * CHUNK, :] = (y[c * CHUNK:(c + 1) * CHUNK] * gate_chunk(c)).astype(o_ref.dtype)
        else:
            o_ref[0:n_ctx, :] = jnp.zeros((n_ctx, o_ref.shape[1]), o_ref.dtype)


def _hyena(p3, conv_w, conv_b, d_skip, tabs, filt, tabs_c, filt_c, n_ctx, with_ctx):
    nb, tb, _ = p3.shape
    n_lat = tb - n_ctx
    cbw = 512
    tk = HY_TILE
    nk = n_lat // tk
    cos_t, sin_t = tabs
    kr, ks, kq = filt
    cos_c, sin_c = tabs_c
    krc, ksc, kqc = filt_c
    cb = conv_b[None, :]
    pcol = lambda off: pl.BlockSpec((None, tb, cbw), lambda b, c, k: (b, 0, off // cbw + c))
    par = lambda rows, sec: pl.BlockSpec((rows, cbw), lambda b, c, k: (0, sec * (HY_WIDTH // cbw) + c))
    kern = functools.partial(_hyena_kernel, n_ctx=n_ctx, n_chunks=tb // CHUNK, with_ctx=with_ctx)
    small = lambda a: pl.BlockSpec(a.shape, lambda b, c, k: (0, 0))
    chan = lambda rows: pl.BlockSpec((rows, cbw), lambda b, c, k: (0, c))
    table = pl.BlockSpec((tk, n_lat), lambda b, c, k: (k % nk, 0))
    spec = pl.BlockSpec((tk, cbw), lambda b, c, k: (jnp.minimum(k, nk - 1), c))
    return pl.pallas_call(
        kern,
        grid=(nb, HY_WIDTH // cbw, 2 * nk),
        in_specs=[pcol(P_HY_X0), pcol(P_HY_X1), pcol(P_HY_V), pcol(P_HY_Z),
                  par(3, 0), par(1, 0), par(3, 1), par(1, 1), par(3, 2), par(1, 2), chan(1),
                  table, table, spec, spec, chan(1), small(cos_c), small(sin_c), chan(n_ctx), chan(n_ctx), chan(1)],
        out_specs=pl.BlockSpec((None, tb, cbw), lambda b, c, k: (b, 0, c)),
        out_shape=jax.ShapeDtypeStruct((nb, tb, HY_WIDTH), BF16),
        scratch_shapes=[pltpu.VMEM((tb, cbw), BF16), pltpu.VMEM((n_lat, cbw), BF16), pltpu.VMEM((n_lat, cbw), BF16),
                        pltpu.VMEM((1, cbw), F32)],
        compiler_params=_cparams("parallel", "parallel", "arbitrary"),
        name="hyena_dft_conv",
    )(p3, p3, p3, p3, conv_w, cb, conv_w, cb, conv_w, cb, d_skip[None, :], cos_t, sin_t,
      kr, ks, kq, cos_c, sin_c, krc, ksc, kqc)


def _mix_kernel(g0_ref, g1_ref, g2_ref, g3_ref, lg_ref, w_ref, o_ref, acc):
    k = pl.program_id(1)
    for kk, g_ref in enumerate((g0_ref, g1_ref, g2_ref, g3_ref)):
        @pl.when(k == kk)
        def _(g_ref=g_ref, kk=kk):
            gate = 0.5 * jnp.tanh(0.5 * lg_ref[...].astype(F32)) + 0.5
            contrib = gate * _dot(g_ref[...], w_ref[0])
            if kk == 0:
                acc[...] = contrib
            else:
                acc[...] += contrib

    @pl.when(k == N_BRANCH - 1)
    def _():
        o_ref[...] = acc[...].astype(o_ref.dtype)


def _mix(gated, p, w_branch_bf):
    tt = p.shape[0]
    tm = math.gcd(tt, 1024)
    g_spec = pl.BlockSpec((tm, BRANCH_WIDTH), lambda i, k: (i, 0))
    return pl.pallas_call(
        _mix_kernel,
        grid=(tt // tm, N_BRANCH),
        in_specs=[g_spec, g_spec, g_spec, g_spec,
                  pl.BlockSpec((tm, D_MODEL), lambda i, k: (i, P_MERGE // D_MODEL + k)),
                  pl.BlockSpec((1, BRANCH_WIDTH, D_MODEL), lambda i, k: (k, 0, 0))],
        out_specs=pl.BlockSpec((tm, D_MODEL), lambda i, k: (i, 0)),
        out_shape=jax.ShapeDtypeStruct((tt, D_MODEL), BF16),
        scratch_shapes=[pltpu.VMEM((tm, D_MODEL), F32)],
        compiler_params=_cparams("parallel", "arbitrary"),
        name="branch_mix",
    )(*gated, p, w_branch_bf)


def _out_kernel(a_ref, w_ref, x_ref, mod_ref, g_ref, b_ref, *rest, with_next):
    mixed = _dot(a_ref[...], w_ref[...])
    r = DEEPNORM_ALPHA * x_ref[...] + mod_ref[0][2:3] * mixed
    mu = jnp.mean(r, axis=-1, keepdims=True)
    rc = r - mu
    var = jnp.mean(rc * rc, axis=-1, keepdims=True)
    xn = rc * lax.rsqrt(var + LN_EPS) * g_ref[...] + b_ref[...]
    if with_next:
        modn_ref, o_ref, h_ref = rest
        o_ref[...] = xn
        mu = jnp.mean(xn, axis=-1, keepdims=True)
        xc = xn - mu
        var = jnp.mean(xc * xc, axis=-1, keepdims=True)
        mn = modn_ref[0]
        h_ref[...] = (xc * lax.rsqrt(var + LN_EPS) * (1.0 + mn[1:2]) + mn[0:1]).astype(h_ref.dtype)
    else:
        (o_ref,) = rest
        o_ref[...] = xn


def _out_next(mixed, w_out_bf, xx, mod, ln_g, ln_b, mod_next, tps, nb):
    tt, d = xx.shape
    tile = pl.BlockSpec((ROW_TILE, d), lambda i: (i, 0))
    modspec = pl.BlockSpec((1, 3, d), lambda i: (_mod_row(i, tps, nb), 0, 0))
    vec = pl.BlockSpec((1, d), lambda i: (0, 0))
    return pl.pallas_call(
        functools.partial(_out_kernel, with_next=True),
        grid=(tt // ROW_TILE,),
        in_specs=[tile, pl.BlockSpec((d, d), lambda i: (0, 0)), tile, modspec, vec, vec, modspec],
        out_specs=[tile, tile],
        out_shape=[jax.ShapeDtypeStruct((tt, d), F32), jax.ShapeDtypeStruct((tt, d), BF16)],
        compiler_params=_cparams("parallel"),
        name="out_proj_norm_next",
    )(mixed, w_out_bf, xx, mod, ln_g[None, :], ln_b[None, :], mod_next)


def _out_last(mixed, w_out_bf, xx, mod, ln_g, ln_b, tps, nb, n_ctx):
    tt, d = xx.shape
    off = n_ctx // ROW_TILE
    tile = pl.BlockSpec((ROW_TILE, d), lambda b, j: (b * tps + off + j, 0))
    vec = pl.BlockSpec((1, d), lambda b, j: (0, 0))
    return pl.pallas_call(
        functools.partial(_out_kernel, with_next=False),
        grid=(nb, tps - off),
        in_specs=[tile, pl.BlockSpec((d, d), lambda b, j: (0, 0)), tile,
                  pl.BlockSpec((1, 3, d), lambda b, j: (b, 0, 0)), vec, vec],
        out_specs=pl.BlockSpec((None, ROW_TILE, d), lambda b, j: (b, j, 0)),
        out_shape=jax.ShapeDtypeStruct((nb, (tps - off) * ROW_TILE, d), F32),
        compiler_params=_cparams("parallel", "parallel"),
        name="out_proj_norm_last",
    )(mixed, w_out_bf, xx, mod, ln_g[None, :], ln_b[None, :])


def kernel(x, c, ctx, c_ctx, w_ada, b_ada, w_in, ssd_conv_w, ssd_conv_b, ssd_dt_bias, ssd_a_log, ssd_d,
           ssd_norm_w, mla_q_norm, mla_w_uq, mla_kv_norm, mla_w_ukv, gqa_sink, hy_conv_w, hy_conv_b,
           hy_w1, hy_b1, hy_w2, hy_b2, hy_w3, hy_freq, hy_d, w_branch, w_out, ln_g, ln_b):
    nb, n_lat, d = x.shape
    n_ctx = ctx.shape[1]
    tb = n_ctx + n_lat
    tt = nb * tb
    tps = tb // ROW_TILE
    assert d == D_MODEL and n_ctx == ROW_TILE and n_lat % 512 == 0 and n_lat % GRID_W == 0

    cos2, sin2 = _rope_tables(n_ctx, n_lat)
    tabs, tabs_c = _dft_tables(n_lat), _dft_tables(n_ctx)
    cvec = jax.nn.silu(jnp.concatenate([c, c_ctx[None]], axis=0))
    mods = [(_matmul(cvec, w_ada, i, F32, 8, 512, "adaln_mod") + b_ada[i]).reshape(nb + 1, 3, d)
            for i in range(DEPTH)]

    xx, h = _ln_mod(x, ctx, mods[0])
    out = None
    for i in range(DEPTH):
        last = i == DEPTH - 1
        p = _in_proj(h, w_in, i, tb, n_ctx, not last)
        p3 = p.reshape(nb, tb, P_WIDTH)
        g_ssd = _ssd(p3, ssd_conv_w[i], ssd_conv_b[i], ssd_dt_bias[i], ssd_a_log[i], ssd_d[i], ssd_norm_w[i],
                     n_ctx).reshape(tt, SSD_WIDTH)
        qm, km, vm, kg, vg = _attn_prep(p, cos2, sin2, mla_q_norm[i], mla_w_uq[i], mla_kv_norm[i], mla_w_ukv[i],
                                        tps)
        r3 = lambda a: a.reshape(nb, tb, a.shape[-1])
        g_mla = _mla(r3(qm), r3(km), r3(vm), p3, n_ctx).reshape(tt, MLA_WIDTH)
        g_gqa = _gqa(p, kg, vg, cos2, sin2, gqa_sink[i], nb, tb, n_ctx)
        fargs = (hy_w1[i], hy_b1[i], hy_w2[i], hy_b2[i], hy_w3[i], hy_freq[i])
        filt = _hyena_filter(n_lat, *tabs, *fargs)
        filt_c = _hyena_filter(n_ctx, *tabs_c, *fargs)
        g_hy = _hyena(p3, hy_conv_w[i], hy_conv_b[i], hy_d[i], tabs, filt, tabs_c, filt_c, n_ctx,
                      not last).reshape(tt, HY_WIDTH)
        mixed = _mix((g_ssd, g_mla, g_gqa, g_hy), p, w_branch[i].astype(BF16))
        w_out_bf = w_out[i].astype(BF16)
        if last:
            out = _out_last(mixed, w_out_bf, xx, mods[i], ln_g[i], ln_b[i], tps, nb, n_ctx)
        else:
            xx, h = _out_next(mixed, w_out_bf, xx, mods[i], ln_g[i], ln_b[i], mods[i + 1], tps, nb)
    return out
```

```python
import functools
import math

import jax
import jax.numpy as jnp
from jax import lax
from jax.experimental import pallas as pl
from jax.experimental.pallas import tpu as pltpu

D_MODEL = 2048
DEPTH = 2
GRID_W = 64
N_BRANCH = 4
BRANCH_WIDTH = D_MODEL // 2
ROPE_DIM = 64
ROPE_BASE = 10000.0
SSD_WIDTH = BRANCH_WIDTH
SSD_HEAD_DIM = 64
SSD_HEADS = SSD_WIDTH // SSD_HEAD_DIM
SSD_GROUPS = 2
SSD_STATE = 128
SSD_CHUNK = 128
SSD_CONV_CH = SSD_WIDTH + 2 * SSD_GROUPS * SSD_STATE
MLA_HEADS = 8
MLA_NOPE = 128
MLA_ROPE = ROPE_DIM
MLA_V = 128
MLA_Q_LORA = 512
MLA_KV_LORA = 256
MLA_WIDTH = MLA_HEADS * MLA_V
GQA_HEADS = 16
GQA_KV_HEADS = 2
GQA_HEAD_DIM = ROPE_DIM
GQA_WIDTH = GQA_HEADS * GQA_HEAD_DIM
GQA_KV_WIDTH = GQA_KV_HEADS * GQA_HEAD_DIM
WINDOW = 128
HY_WIDTH = BRANCH_WIDTH
HY_POS_EMB = 33
HY_BANDS = (HY_POS_EMB - 1) // 2
HY_FILTER_HIDDEN = 64
HY_FAST_DECAY = 0.3
HY_SLOW_DECAY = 1.5
HY_DECAY_TARGET = 0.01
LN_EPS = 1e-6
NEG_INF = -1e30
DEEPNORM_ALPHA = (2 * DEPTH) ** 0.25

IN_SPLITS = (
    SSD_WIDTH, SSD_CONV_CH, 2 * SSD_HEADS, MLA_Q_LORA, MLA_KV_LORA, MLA_ROPE, MLA_WIDTH, GQA_WIDTH,
    GQA_KV_WIDTH, GQA_KV_WIDTH, GQA_WIDTH, 3 * HY_WIDTH, HY_WIDTH, N_BRANCH * D_MODEL,
)

LANE = 128
ROW_TILE = 256
CHUNK = SSD_CHUNK
VMEM_LIMIT_BYTES = 56 * 1024 * 1024

P_MERGE = 0
P_HY_X0 = 8192
P_HY_X1 = 9216
P_HY_V = 10240
P_HY_Z = 11264
P_SSD_Z = 12288
P_SSD_X = 13312
P_MLA_Z = 14336
P_GQA_Q = 15360
P_GQA_Z = 16384
P_MLA_CQ = 17408
P_SSD_B = 17920
P_SSD_C = 18176
P_MLA_CKV = 18432
P_GQA_K = 18688
P_GQA_V = 18816
P_SMALL = 18944
P_WIDTH = 19072
SMALL_DT = 64

F32 = jnp.float32
BF16 = jnp.bfloat16


def _cparams(*sem):
    return pltpu.CompilerParams(dimension_semantics=sem, vmem_limit_bytes=VMEM_LIMIT_BYTES)


def _split3(v):
    h1 = v.astype(BF16)
    r1 = v - h1.astype(F32)
    h2 = r1.astype(BF16)
    h3 = (r1 - h2.astype(F32)).astype(BF16)
    return h1, h2, h3


def _dot(a, b):
    return jnp.dot(a, b, preferred_element_type=F32)


def _dot_nt(a, b):
    return lax.dot_general(a, b, (((1,), (1,)), ((), ())), preferred_element_type=F32)


def _dot_tn(a, b):
    return lax.dot_general(a, b, (((0,), (0,)), ((), ())), preferred_element_type=F32)


def _dot_exact_rhs(a, b_bf16, pieces):
    parts = _split3(a)[:pieces]
    out = _dot(parts[0], b_bf16)
    for p in parts[1:]:
        out = out + _dot(p, b_bf16)
    return out


def _dot_exact_lhs(a_bf16, b):
    b1, b2, b3 = _split3(b)
    return _dot(a_bf16, b1) + (_dot(a_bf16, b2) + _dot(a_bf16, b3))


def _dot3(a, b):
    a1, a2, _ = _split3(a)
    b1, b2, _ = _split3(b)
    return _dot(a1, b1) + (_dot(a1, b2) + _dot(a2, b1))


def _silu(v):
    return v * jax.nn.sigmoid(v)


def _softplus(v):
    return jnp.maximum(v, 0.0) + jnp.log(1.0 + jnp.exp(-jnp.abs(v)))


def _rotswap(t):
    lane = lax.broadcasted_iota(jnp.int32, t.shape, 1)
    return jnp.where((lane & 63) < 32, pltpu.roll(t, 96, 1), pltpu.roll(t, 32, 1))


def _rope(t, cos, sin_signed):
    return t * cos + _rotswap(t) * sin_signed


def _mm_kernel(a_ref, b_ref, o_ref):
    o_ref[...] = _dot(a_ref[...].astype(BF16), b_ref[...].astype(BF16)).astype(o_ref.dtype)


def _matmul(a, b, layer, out_dtype, tm, tn, name):
    m, k = a.shape
    n = b.shape[2]
    tm = min(tm, m)
    tn = min(tn, n)
    return pl.pallas_call(
        _mm_kernel,
        grid=(pl.cdiv(m, tm), pl.cdiv(n, tn)),
        in_specs=[pl.BlockSpec((tm, k), lambda i, j: (i, 0)),
                  pl.BlockSpec((None, k, tn), lambda i, j: (layer, 0, j))],
        out_specs=pl.BlockSpec((tm, tn), lambda i, j: (i, j)),
        out_shape=jax.ShapeDtypeStruct((m, n), out_dtype),
        compiler_params=_cparams("parallel", "arbitrary"),
        name=name,
    )(a, b)


def _mod_row(i, tiles_per_sample, n_samples):
    return jnp.where(i % tiles_per_sample == 0, n_samples, i // tiles_per_sample)


def _ln_mod_kernel(ctx_ref, x_ref, mod_ref, xx_ref, h_ref):
    def emit(x):
        xx_ref[...] = x
        mu = jnp.mean(x, axis=-1, keepdims=True)
        xc = x - mu
        var = jnp.mean(xc * xc, axis=-1, keepdims=True)
        m = mod_ref[0]
        h_ref[...] = (xc * lax.rsqrt(var + LN_EPS) * (1.0 + m[1:2]) + m[0:1]).astype(h_ref.dtype)

    @pl.when(pl.program_id(1) == 0)
    def _():
        emit(ctx_ref[...])

    @pl.when(pl.program_id(1) != 0)
    def _():
        emit(x_ref[...])


def _ln_mod(x, ctx, mod):
    nb, n_lat, d = x.shape
    assert ctx.shape[1] == ROW_TILE
    tps = 1 + n_lat // ROW_TILE
    rows = pl.BlockSpec((ROW_TILE, d), lambda b, i: (b * tps + i, 0))
    return pl.pallas_call(
        _ln_mod_kernel,
        grid=(nb, tps),
        in_specs=[pl.BlockSpec((None, ROW_TILE, d), lambda b, i: (b, 0, 0)),
                  pl.BlockSpec((None, ROW_TILE, d), lambda b, i: (b, jnp.maximum(i - 1, 0), 0)),
                  pl.BlockSpec((1, 3, d), lambda b, i: (jnp.where(i == 0, nb, b), 0, 0))],
        out_specs=[rows, rows],
        out_shape=[jax.ShapeDtypeStruct((nb * tps * ROW_TILE, d), F32),
                   jax.ShapeDtypeStruct((nb * tps * ROW_TILE, d), BF16)],
        compiler_params=_cparams("parallel", "arbitrary"),
        name="ln_modulate",
    )(ctx, x, mod)


IN_TILE = 1024
IN_ROWS = 1152
SUBLANE = 8


def _in_proj_plan():
    off, start = [], 0
    for size in IN_SPLITS:
        off.append(start)
        start += size
    (o_sz, o_xbc, o_dt, o_cq, o_ckv, o_kpe, o_mz, o_gq, o_gk, o_gv, o_gz, o_hy, o_hz, o_mg) = off
    wide = [(P_MERGE, o_mg, N_BRANCH * D_MODEL), (P_HY_X0, o_hy, 3 * HY_WIDTH), (P_HY_Z, o_hz, HY_WIDTH),
            (P_SSD_Z, o_sz, SSD_WIDTH), (P_SSD_X, o_xbc, SSD_WIDTH), (P_MLA_Z, o_mz, MLA_WIDTH),
            (P_GQA_Q, o_gq, GQA_WIDTH), (P_GQA_Z, o_gz, GQA_WIDTH)]
    n_tiles = -(-P_WIDTH // IN_TILE)
    src = [None] * n_tiles
    for p_off, s_off, width in wide:
        for t in range(width // IN_TILE):
            src[p_off // IN_TILE + t] = s_off + t * IN_TILE
    tail_tiles = [j for j in range(n_tiles) if src[j] is None]
    assert tail_tiles[0] * IN_TILE == P_MLA_CQ
    tail_src = [(o_cq, MLA_Q_LORA), (o_xbc + SSD_WIDTH, 2 * SSD_GROUPS * SSD_STATE), (o_ckv, MLA_KV_LORA),
                (o_gk, 2 * GQA_KV_WIDTH), (o_kpe, MLA_ROPE), (o_dt, 2 * SSD_HEADS)]
    return src, tail_tiles, tail_src


def _in_proj_kernel(start_ref, tail_idx_ref, a_ref, win_ref, tail_ref, o_ref, wb, *, n_ctx, tiles_per_sample):
    @pl.when(pl.program_id(1) == 0)
    def _():
        from_tail = tail_idx_ref[pl.program_id(0)] >= 0

        @pl.when(jnp.logical_not(from_tail))
        def _():
            wb[...] = win_ref[0].astype(wb.dtype)

        @pl.when(from_tail)
        def _():
            wb[...] = tail_ref[...].astype(wb.dtype)

    skip = jnp.logical_and(tail_idx_ref[pl.program_id(0)] == -2, pl.program_id(1) % tiles_per_sample == 0)

    @pl.when(skip)
    def _():
        o_ref[0:n_ctx, :] = jnp.zeros((n_ctx, o_ref.shape[1]), o_ref.dtype)
        o_ref[n_ctx:, :] = _dot_nt(a_ref[n_ctx:, :], wb[...]).astype(o_ref.dtype)

    @pl.when(jnp.logical_not(skip))
    def _():
        o_ref[...] = _dot_nt(a_ref[...], wb[...]).astype(o_ref.dtype)


def _in_proj(h, w_in, layer, tb, n_ctx, ctx_outputs):
    tt, d = h.shape
    w_t = jnp.swapaxes(w_in, 1, 2)
    src, tail_tiles, tail_src = _in_proj_plan()
    tail = jnp.concatenate([w_t[layer, s:s + w, :] for s, w in tail_src], axis=0)
    tail = jnp.pad(tail, ((0, len(tail_tiles) * IN_TILE - tail.shape[0]), (0, 0)))
    assert all(s is None or s % SUBLANE == 0 for s in src)
    rows = IN_ROWS if tb % IN_ROWS == 0 else tb // 2
    assert tb % rows == 0 and rows % (2 * SUBLANE) == 0 and rows > n_ctx
    starts = [0 if s is None else s // SUBLANE for s in src]
    wide_code = lambda j: -1 if ctx_outputs or j == P_SSD_X // IN_TILE else -2
    tidx = [tail_tiles.index(j) if s is None else wide_code(j) for j, s in enumerate(src)]
    as_i32 = lambda v: jnp.asarray(v, jnp.int32)
    gs = pltpu.PrefetchScalarGridSpec(
        num_scalar_prefetch=2, grid=(len(src), tt // rows),
        in_specs=[pl.BlockSpec((rows, d), lambda j, i, st, ti: (i, 0)),
                  pl.BlockSpec((pl.Element(1), pl.Element(IN_TILE), pl.Element(d)),
                               lambda j, i, st, ti: (layer, st[j] * SUBLANE, 0)),
                  pl.BlockSpec((IN_TILE, d), lambda j, i, st, ti: (jnp.maximum(ti[j], 0), 0),
                               pipeline_mode=pl.Buffered(1))],
        out_specs=pl.BlockSpec((rows, IN_TILE), lambda j, i, st, ti: (i, j)),
        scratch_shapes=[pltpu.VMEM((IN_TILE, d), BF16)])
    return pl.pallas_call(
        functools.partial(_in_proj_kernel, n_ctx=n_ctx, tiles_per_sample=tb // rows), grid_spec=gs,
        out_shape=jax.ShapeDtypeStruct((tt, P_WIDTH), BF16),
        compiler_params=_cparams("parallel", "arbitrary"),
        name="in_proj",
    )(as_i32(starts), as_i32(tidx), h, w_t, tail)


def _conv_chunk(src_ref, w, bias, c, n_ctx_chunks, n_chunks):
    tb = n_chunks * CHUNK
    r0 = pl.multiple_of(c * CHUNK, CHUNK)
    cur = src_ref[pl.ds(r0, CHUNK), :].astype(F32)
    rp = pl.multiple_of(jnp.maximum(r0 - 16, 0), 16)
    rn = pl.multiple_of(jnp.minimum(r0 + CHUNK, tb - 16), 16)
    prev_row = src_ref[pl.ds(rp, 16), :].astype(F32)[15:16]
    next_row = src_ref[pl.ds(rn, 16), :].astype(F32)[0:1]
    has_prev = jnp.logical_and(c != 0, c != n_ctx_chunks)
    has_next = jnp.logical_and(c != n_ctx_chunks - 1, c != n_chunks - 1)
    prev_row = jnp.where(has_prev, prev_row, 0.0)
    next_row = jnp.where(has_next, next_row, 0.0)
    row = lax.broadcasted_iota(jnp.int32, cur.shape, 0)
    sp = jnp.where(row == 0, prev_row, pltpu.roll(cur, 1, 0))
    sn = jnp.where(row == CHUNK - 1, next_row, pltpu.roll(cur, CHUNK - 1, 0))
    return sp * w[0:1] + cur * w[1:2] + sn * w[2:3] + bias


def _ssd_kernel(x_ref, b_ref, c_ref, sm_ref, z_ref, cwx_ref, cbx_ref, cwb_ref, cbb_ref, cwc_ref, cbc_ref,
                dtb_ref, alog_ref, dsk_ref, nw_ref, o_ref, ux, ub, uc, yacc, ybwd, st, st_b,
                *, n_ctx_chunks, n_chunks):
    g = pl.program_id(1)
    hp = ux.shape[1]
    nh = hp // SSD_HEAD_DIM

    def conv_body(c, carry):
        r0 = pl.multiple_of(c * CHUNK, CHUNK)
        ux[pl.ds(r0, CHUNK), :] = _silu(_conv_chunk(x_ref, cwx_ref[...], cbx_ref[...], c, n_ctx_chunks,
                                                    n_chunks)).astype(ux.dtype)
        ub[pl.ds(r0, CHUNK), :] = _silu(_conv_chunk(b_ref, cwb_ref[...], cbb_ref[...], c, n_ctx_chunks,
                                                    n_chunks)).astype(ub.dtype)
        uc[pl.ds(r0, CHUNK), :] = _silu(_conv_chunk(c_ref, cwc_ref[...], cbc_ref[...], c, n_ctx_chunks,
                                                    n_chunks)).astype(uc.dtype)
        return carry

    lax.fori_loop(0, n_chunks, conv_body, 0)

    ri = lax.broadcasted_iota(jnp.int32, (CHUNK, CHUNK), 0)
    ci = lax.broadcasted_iota(jnp.int32, (CHUNK, CHUNK), 1)
    lane128 = lax.broadcasted_iota(jnp.int32, (CHUNK, LANE), 1)
    a_all = -jnp.exp(alog_ref[0]) * math.log2(math.e)
    dtb = dtb_ref[0]

    def direction_consts(dirn):
        tri = (ci <= ri) if dirn == 0 else (ci >= ri)
        tri_bf = jnp.where(tri, 1.0, 0.0).astype(BF16)
        expand = jnp.where(lax.broadcasted_iota(jnp.int32, (LANE, hp), 0)
                           == dirn * 16 + (lax.broadcasted_iota(jnp.int32, (LANE, hp), 1) >> 6),
                           1.0, 0.0).astype(BF16)
        return tri, tri_bf, expand

    consts = (direction_consts(0), direction_consts(1))

    def chunk_step(dirn, c, st):
        tri, tri_bf, expand = consts[dirn]
        last = CHUNK - 1 if dirn == 0 else 0
        r0 = pl.multiple_of(c * CHUNK, CHUNK)
        xs = ux[pl.ds(r0, CHUNK), :].astype(F32)
        bc = ub[pl.ds(r0, CHUNK), :]
        cc = uc[pl.ds(r0, CHUNK), :]
        sm = sm_ref[pl.ds(r0, CHUNK), :].astype(F32)
        sm = jnp.where(g == 0, pltpu.roll(sm, LANE - SMALL_DT, 1), pltpu.roll(sm, LANE - SMALL_DT - nh, 1))
        dt = _softplus(sm + dtb)
        la = dt * a_all
        cum = _dot_exact_lhs(tri_bf, la)
        cx = _dot_exact_rhs(cum, expand, 2)
        dtx = _dot_exact_rhs(dt, expand, 2)
        cx_last = cx[last:last + 1]
        ecum = jnp.exp2(cx)
        dec_end = jnp.exp2(cx_last - cx)
        xd = xs * dtx
        xd_bf = xd.astype(BF16)
        gmat = _dot_nt(cc, bc)
        cum_t = cum.T
        cols = []
        for j in range(nh // 2):
            sc = []
            for h in (2 * j, 2 * j + 1):
                k = dirn * 16 + h
                diff = cum[:, k:k + 1] - cum_t[k:k + 1, :]
                lmat = jnp.exp2(jnp.where(tri, diff, NEG_INF))
                sc.append((gmat * lmat).astype(BF16))
            xcol = xd_bf[:, j * LANE:(j + 1) * LANE]
            lo = lane128 < SSD_HEAD_DIM
            rhs = jnp.concatenate([jnp.where(lo, xcol, jnp.zeros_like(xcol)),
                                   jnp.where(lo, jnp.zeros_like(xcol), xcol)], axis=0)
            cols.append(_dot(jnp.concatenate(sc, axis=1), rhs))
        y = jnp.concatenate(cols, axis=1)
        s_prev = st[...]
        y = y + _dot(cc, s_prev.astype(BF16)) * ecum
        st[...] = jnp.exp2(cx_last) * s_prev + _dot_tn(bc, (xd * dec_end).astype(BF16))
        return y, xs, r0

    st[...] = jnp.zeros_like(st)
    st_b[...] = jnp.zeros_like(st_b)

    def scan_body(i, carry):
        y_f, xs_f, r0_f = chunk_step(0, i, st)
        yacc[pl.ds(r0_f, CHUNK), :] = y_f + xs_f * dsk_ref[...]
        c_b = jnp.where(i < n_ctx_chunks, n_ctx_chunks - 1 - i, n_chunks - 1 + n_ctx_chunks - i)
        y_b, _, r0_b = chunk_step(1, c_b, st_b)
        ybwd[pl.ds(r0_b, CHUNK), :] = y_b
        return carry

    lax.fori_loop(0, n_chunks, scan_body, 0, unroll=3)

    def gate_body(c, carry):
        r0 = pl.multiple_of(c * CHUNK, CHUNK)
        tot = (yacc[pl.ds(r0, CHUNK), :] + ybwd[pl.ds(r0, CHUNK), :]) * _silu(z_ref[pl.ds(r0, CHUNK), :].astype(F32))
        ms = jnp.mean(tot * tot, axis=-1, keepdims=True)
        o_ref[pl.ds(r0, CHUNK), :] = (tot * lax.rsqrt(ms + LN_EPS) * nw_ref[...]).astype(o_ref.dtype)
        return carry

    lax.fori_loop(0, n_chunks, gate_body, 0)


def _ssd(p3, conv_w, conv_b, dt_bias, a_log, d_skip, norm_w, n_ctx):
    nb, tb, _ = p3.shape
    hp = SSD_WIDTH // SSD_GROUPS
    n_chunks = tb // CHUNK
    gn = SSD_GROUPS * SSD_STATE
    cwx, cwb, cwc = conv_w[:, :SSD_WIDTH], conv_w[:, SSD_WIDTH:SSD_WIDTH + gn], conv_w[:, SSD_WIDTH + gn:]
    cb = conv_b[None, :]
    cbx, cbb, cbc = cb[:, :SSD_WIDTH], cb[:, SSD_WIDTH:SSD_WIDTH + gn], cb[:, SSD_WIDTH + gn:]
    hg = SSD_HEADS // SSD_GROUPS

    def per_group(v):
        v = v.reshape(2, SSD_GROUPS, hg).transpose(1, 0, 2)
        v = jnp.pad(v, ((0, 0), (0, 0), (0, 16 - hg))).reshape(SSD_GROUPS, 1, 32)
        return jnp.pad(v, ((0, 0), (0, 0), (0, LANE - 32)))

    dsk = jnp.repeat(d_skip, SSD_HEAD_DIM)[None, :]
    kern = functools.partial(_ssd_kernel, n_ctx_chunks=n_ctx // CHUNK, n_chunks=n_chunks)
    col = lambda off, w: (lambda b, g: (b, 0, off // w + g))
    par = lambda b, g: (0, g)
    return pl.pallas_call(
        kern,
        grid=(nb, SSD_GROUPS),
        in_specs=[pl.BlockSpec((None, tb, hp), col(P_SSD_X, hp)),
                  pl.BlockSpec((None, tb, SSD_STATE), col(P_SSD_B, SSD_STATE)),
                  pl.BlockSpec((None, tb, SSD_STATE), col(P_SSD_C, SSD_STATE)),
                  pl.BlockSpec((None, tb, LANE), lambda b, g: (b, 0, P_SMALL // LANE)),
                  pl.BlockSpec((None, tb, hp), col(P_SSD_Z, hp)),
                  pl.BlockSpec((3, hp), par), pl.BlockSpec((1, hp), par),
                  pl.BlockSpec((3, SSD_STATE), par), pl.BlockSpec((1, SSD_STATE), par),
                  pl.BlockSpec((3, SSD_STATE), par), pl.BlockSpec((1, SSD_STATE), par),
                  pl.BlockSpec((1, 1, LANE), lambda b, g: (g, 0, 0)),
                  pl.BlockSpec((1, 1, LANE), lambda b, g: (g, 0, 0)),
                  pl.BlockSpec((1, hp), par), pl.BlockSpec((1, hp), par)],
        out_specs=pl.BlockSpec((None, tb, hp), lambda b, g: (b, 0, g)),
        out_shape=jax.ShapeDtypeStruct((nb, tb, SSD_WIDTH), BF16),
        scratch_shapes=[pltpu.VMEM((tb, hp), BF16), pltpu.VMEM((tb, SSD_STATE), BF16),
                        pltpu.VMEM((tb, SSD_STATE), BF16), pltpu.VMEM((tb, hp), F32),
                        pltpu.VMEM((tb, hp), F32), pltpu.VMEM((SSD_STATE, hp), F32),
                        pltpu.VMEM((SSD_STATE, hp), F32)],
        compiler_params=_cparams("parallel", "parallel"),
        name="ssd_bidir",
    )(p3, p3, p3, p3, p3, cwx, cbx, cwb, cbb, cwc, cbc, per_group(dt_bias), per_group(a_log), dsk,
      norm_w[None, :])


def _rope_tables(n_ctx, n_lat):
    rows = n_lat // GRID_W
    row = jnp.repeat(jnp.arange(rows, dtype=F32), GRID_W)
    colp = jnp.tile(jnp.arange(GRID_W, dtype=F32), rows)
    n_freq = ROPE_DIM // 4
    inv_freq = ROPE_BASE ** (-jnp.arange(n_freq, dtype=F32) / n_freq)
    ang = jnp.concatenate([row[:, None] * inv_freq[None, :], colp[:, None] * inv_freq[None, :]], axis=-1)
    cos, sin = jnp.cos(ang), jnp.sin(ang)
    cos2 = jnp.tile(cos, (1, 4))
    sin2 = jnp.tile(jnp.concatenate([-sin, sin], axis=-1), (1, 2))
    cos2 = jnp.concatenate([jnp.ones((n_ctx, LANE), F32), cos2], axis=0)
    sin2 = jnp.concatenate([jnp.zeros((n_ctx, LANE), F32), sin2], axis=0)
    return cos2, sin2


def _attn_prep_kernel(cq_ref, ckv_ref, sm_ref, gk_ref, gv_ref, cos_ref, sin_ref, qn_ref, wq_ref, kn_ref,
                      wk_ref, wv_ref, q_ref, k_ref, v_ref, kg_ref, vg_ref):
    cos, sin = cos_ref[...], sin_ref[...]
    lane = lax.broadcasted_iota(jnp.int32, cos.shape, 1)
    scale = (MLA_NOPE + MLA_ROPE) ** -0.5 * math.log2(math.e)
    cq = cq_ref[...].astype(F32)
    cqn = cq * lax.rsqrt(jnp.mean(cq * cq, axis=-1, keepdims=True) + LN_EPS) * qn_ref[...]
    q = _dot(cqn.astype(BF16), wq_ref[...])
    ckv = ckv_ref[...].astype(F32)
    ckvn = (ckv * lax.rsqrt(jnp.mean(ckv * ckv, axis=-1, keepdims=True) + LN_EPS) * kn_ref[...]).astype(BF16)
    kn = _dot(ckvn, wk_ref[...])
    v_ref[...] = _dot(ckvn, wv_ref[...]).astype(v_ref.dtype)
    kpe = jnp.where(lane < MLA_ROPE, _rope(sm_ref[...].astype(F32), cos, sin), 0.0).astype(k_ref.dtype)
    for h in range(MLA_HEADS):
        o = 2 * h * LANE
        q_ref[:, o:o + LANE] = (q[:, o:o + LANE] * scale).astype(q_ref.dtype)
        q_ref[:, o + LANE:o + 2 * LANE] = (_rope(q[:, o + LANE:o + 2 * LANE], cos, sin) * scale).astype(q_ref.dtype)
        k_ref[:, o:o + LANE] = kn[:, h * LANE:(h + 1) * LANE].astype(k_ref.dtype)
        k_ref[:, o + LANE:o + 2 * LANE] = kpe
    kg_ref[...] = _rope(gk_ref[...].astype(F32), cos, sin).astype(kg_ref.dtype)
    vg_ref[...] = gv_ref[...].astype(F32).T.astype(vg_ref.dtype)


def _attn_prep(p, cos2, sin2, q_norm, w_uq, kv_norm, w_ukv, tps):
    tt = p.shape[0]
    hq = MLA_NOPE + MLA_ROPE
    wq = w_uq.reshape(MLA_Q_LORA, MLA_HEADS, hq)
    wq = jnp.pad(wq, ((0, 0), (0, 0), (0, 2 * LANE - hq))).reshape(MLA_Q_LORA, MLA_HEADS * 2 * LANE).astype(BF16)
    wkv = w_ukv.reshape(MLA_KV_LORA, MLA_HEADS, MLA_NOPE + MLA_V)
    wk = wkv[:, :, :MLA_NOPE].reshape(MLA_KV_LORA, MLA_HEADS * MLA_NOPE).astype(BF16)
    wv = wkv[:, :, MLA_NOPE:].reshape(MLA_KV_LORA, MLA_HEADS * MLA_V).astype(BF16)
    full = lambda shape: pl.BlockSpec(shape, lambda i: (0, 0))
    pcol = lambda off, w: pl.BlockSpec((ROW_TILE, w), lambda i: (i, off // w))
    rows = lambda w: pl.BlockSpec((ROW_TILE, w), lambda i: (i, 0))
    tab = pl.BlockSpec((ROW_TILE, LANE), lambda i: (i % tps, 0))
    return pl.pallas_call(
        _attn_prep_kernel,
        grid=(tt // ROW_TILE,),
        in_specs=[pcol(P_MLA_CQ, MLA_Q_LORA), pcol(P_MLA_CKV, MLA_KV_LORA), pcol(P_SMALL, LANE),
                  pcol(P_GQA_K, LANE), pcol(P_GQA_V, LANE), tab, tab,
                  full((1, MLA_Q_LORA)), full(wq.shape), full((1, MLA_KV_LORA)), full(wk.shape), full(wv.shape)],
        out_specs=[rows(MLA_HEADS * 2 * LANE), rows(MLA_HEADS * 2 * LANE), rows(MLA_WIDTH),
                   rows(LANE), pl.BlockSpec((LANE, ROW_TILE), lambda i: (0, i))],
        out_shape=[jax.ShapeDtypeStruct((tt, MLA_HEADS * 2 * LANE), BF16),
                   jax.ShapeDtypeStruct((tt, MLA_HEADS * 2 * LANE), BF16),
                   jax.ShapeDtypeStruct((tt, MLA_WIDTH), BF16),
                   jax.ShapeDtypeStruct((tt, LANE), BF16),
                   jax.ShapeDtypeStruct((LANE, tt), BF16)],
        compiler_params=_cparams("parallel"),
        name="attn_prep",
    )(p, p, p, p, p, cos2, sin2, q_norm[None, :], wq, kv_norm[None, :], wk, wv)


MLA_Q_CHUNK = 256


def _mla_kernel(q_ref, k_ref, v_ref, z_ref, o_ref, *, n_ctx):
    tb = k_ref.shape[0]
    ones_col = jnp.where(lax.broadcasted_iota(jnp.int32, (tb, LANE), 1) == 0, 1.0, 0.0).astype(BF16)
    v_ext = jnp.concatenate([v_ref[...], ones_col], axis=1)

    def attend(r0, rows, n_keys):
        s = _dot_nt(q_ref[r0:r0 + rows, :], k_ref[0:n_keys, :])
        m = jnp.max(s, axis=-1, keepdims=True)
        p = jnp.exp2(s - m).astype(BF16)
        o_ext = _dot(p, v_ext[0:n_keys])
        o = o_ext[:, :MLA_V] / o_ext[:, MLA_V:MLA_V + 1]
        o_ref[r0:r0 + rows, :] = (o * _silu(z_ref[r0:r0 + rows, :].astype(F32))).astype(o_ref.dtype)

    attend(0, n_ctx, n_ctx)
    for r0 in range(n_ctx, tb, MLA_Q_CHUNK):
        attend(r0, MLA_Q_CHUNK, tb)


def _mla(qm3, km3, vm3, p3, n_ctx):
    nb, tb, _ = p3.shape
    assert (tb - n_ctx) % MLA_Q_CHUNK == 0
    blk = lambda w, off: pl.BlockSpec((None, tb, w), lambda b, h: (b, 0, off // w + h))
    return pl.pallas_call(
        functools.partial(_mla_kernel, n_ctx=n_ctx),
        grid=(nb, MLA_HEADS),
        in_specs=[blk(2 * LANE, 0), blk(2 * LANE, 0), blk(LANE, 0), blk(LANE, P_MLA_Z)],
        out_specs=blk(LANE, 0),
        out_shape=jax.ShapeDtypeStruct((nb, tb, MLA_WIDTH), BF16),
        compiler_params=_cparams("parallel", "parallel"),
        name="mla_attention",
    )(qm3, km3, vm3, p3)


GQA_HEAD_BATCH = 8

def _gqa_kernel(sink_ref, q_ref, kp_ref, ko_ref, kn_ref, kc_ref, vp_ref, vo_ref, vn_ref, vc_ref, cos_ref,
                sin_ref, z_ref, o_ref, *, n_ctx_blocks):
    blk = pl.program_id(1)
    n_blk = pl.num_programs(1)
    is_lat = blk >= n_ctx_blocks
    has_prev = blk > n_ctx_blocks
    has_next = jnp.logical_and(is_lat, blk < n_blk - 1)
    cos, sin = cos_ref[...], sin_ref[...]
    log2e = math.log2(math.e)
    scale = GQA_HEAD_DIM ** -0.5 * log2e
    w = q_ref.shape[0]
    hd = GQA_HEAD_DIM
    hpk = GQA_HEADS // GQA_KV_HEADS
    q = jnp.concatenate([_rope(q_ref[:, c * LANE:(c + 1) * LANE].astype(F32), cos, sin) * scale
                         for c in range(GQA_WIDTH // LANE)], axis=1)
    qt = q.T.astype(BF16)
    kall = jnp.concatenate([kp_ref[...], ko_ref[...], kn_ref[...], kc_ref[...]], axis=0)
    vt = jnp.concatenate([vp_ref[...], vo_ref[...], vn_ref[...], vc_ref[...]], axis=1)
    kj = lax.broadcasted_iota(jnp.int32, (w, w), 0)
    qi = lax.broadcasted_iota(jnp.int32, (w, w), 1)
    m_prev = jnp.logical_and(kj >= qi, has_prev)
    m_next = jnp.logical_and(kj <= qi, has_next)
    zeros = jnp.zeros((hd, w), BF16)
    ones_rows = jnp.ones((16, kall.shape[0]), BF16)
    nbat = GQA_HEAD_BATCH
    tile = lambda a: jnp.concatenate([a] * nbat, axis=1)
    m_prev, m_next = tile(m_prev), tile(m_next)
    outs = []
    for h0 in range(0, GQA_HEADS, nbat):
        kh = h0 // hpk
        heads = range(h0, h0 + nbat)
        rhs = jnp.concatenate(
            [jnp.concatenate([qt[h * hd:(h + 1) * hd], zeros] if kh == 0 else [zeros, qt[h * hd:(h + 1) * hd]],
                             axis=0) for h in heads], axis=1)
        st = _dot(kall, rhs)
        s_p = jnp.where(m_prev, st[0:w], NEG_INF)
        s_o = jnp.where(is_lat, st[w:2 * w], NEG_INF)
        s_n = jnp.where(m_next, st[2 * w:3 * w], NEG_INF)
        s = jnp.concatenate([s_p, s_o, s_n, st[3 * w:]], axis=0)
        sink = jnp.concatenate([jnp.full((1, w), sink_ref[h] * log2e, F32) for h in heads], axis=1)
        m = jnp.maximum(jnp.max(s, axis=0, keepdims=True), sink)
        pr = jnp.exp2(s - m).astype(BF16)
        ot = _dot(jnp.concatenate([vt[kh * hd:(kh + 1) * hd], ones_rows], axis=0), pr)
        l = ot[hd:hd + 1] + jnp.exp2(sink - m)
        ot = ot[:hd] / l
        outs.extend(ot[:, i * w:(i + 1) * w] for i in range(nbat))
    y = jnp.concatenate(outs, axis=0).T
    o_ref[...] = (y * _silu(z_ref[...].astype(F32))).astype(o_ref.dtype)


def _gqa(p, kg, vgt, cos2, sin2, sink, nb, tb, n_ctx):
    w = WINDOW
    bps = tb // w
    row = lambda b, i: b * bps + i
    prev = lambda b, i: b * bps + jnp.maximum(i - 1, 0)
    nxt = lambda b, i: b * bps + jnp.minimum(i + 1, bps - 1)
    kspec = lambda f: pl.BlockSpec((w, LANE), lambda b, i: (f(b, i), 0))
    vspec = lambda f: pl.BlockSpec((LANE, w), lambda b, i: (0, f(b, i)))
    kctx = pl.BlockSpec((n_ctx, LANE), lambda b, i: (b * (tb // n_ctx), 0))
    vctx = pl.BlockSpec((LANE, n_ctx), lambda b, i: (0, b * (tb // n_ctx)))
    tab = pl.BlockSpec((w, LANE), lambda b, i: (i, 0))
    return pl.pallas_call(
        functools.partial(_gqa_kernel, n_ctx_blocks=n_ctx // w),
        grid=(nb, bps),
        in_specs=[pl.BlockSpec(memory_space=pltpu.SMEM),
                  pl.BlockSpec((w, GQA_WIDTH), lambda b, i: (row(b, i), P_GQA_Q // GQA_WIDTH)),
                  kspec(prev), kspec(row), kspec(nxt), kctx, vspec(prev), vspec(row), vspec(nxt), vctx, tab, tab,
                  pl.BlockSpec((w, GQA_WIDTH), lambda b, i: (row(b, i), P_GQA_Z // GQA_WIDTH))],
        out_specs=pl.BlockSpec((w, GQA_WIDTH), lambda b, i: (row(b, i), 0)),
        out_shape=jax.ShapeDtypeStruct((nb * tb, GQA_WIDTH), BF16),
        compiler_params=_cparams("parallel", "arbitrary"),
        name="gqa_window_attention",
    )(sink, p, kg, kg, kg, kg, vgt, vgt, vgt, vgt, cos2, sin2, p)


def _dft_tables(n):
    m = jnp.arange(n, dtype=jnp.int32)[None, :]

    def small(kvals):
        ang = ((kvals[:, None] * m) % (2 * n)).astype(F32) * (math.pi / n)
        return jnp.cos(ang), jnp.sin(ang)

    ca, sa = small(64 * jnp.arange(n // 64, dtype=jnp.int32))
    cb, sb = small(jnp.arange(64, dtype=jnp.int32))
    cos_t = ca[:, None, :] * cb[None, :, :] - sa[:, None, :] * sb[None, :, :]
    sin_t = sa[:, None, :] * cb[None, :, :] + ca[:, None, :] * sb[None, :, :]
    return cos_t.reshape(n, n).astype(BF16), sin_t.reshape(n, n).astype(BF16)


def _hyena_feats(n):
    t = jnp.linspace(0.0, 1.0, n, dtype=F32)[:, None]
    w_ang = (2.0 * math.pi / n) * jnp.arange(n, dtype=F32)[:, None]
    bands = jnp.linspace(1e-4, HY_BANDS - 1, HY_BANDS, dtype=F32)[None, :]
    feats = jnp.concatenate([t, jnp.cos(bands * w_ang), -jnp.sin(bands * w_ang)], axis=-1)
    return jnp.pad(feats, ((0, 0), (0, LANE - HY_POS_EMB)))


def _hyena_filter_kernel(feats_ref, w1_ref, b1_ref, w2_ref, b2_ref, w3f_ref, w3b_ref, freq_ref, dl_ref,
                         cos_ref, sin_ref, kr_ref, ks_ref, kq_ref, hs, hd, *, n):
    kt = pl.program_id(0)
    tk = cos_ref.shape[0]

    @pl.when(kt == 0)
    def _():
        feats = feats_ref[...]
        t = feats[:, 0:1]
        hdn = jnp.sin(freq_ref[0:1, :] * (_dot3(feats, w1_ref[...]) + b1_ref[...]))
        hdn = jnp.sin(freq_ref[1:2, :] * (_dot3(hdn, w2_ref[...]) + b2_ref[...]))
        cbw = 512
        for c0 in range(0, HY_WIDTH, cbw):
            cs = slice(c0, c0 + cbw)
            decay = jnp.exp(-t * dl_ref[:, cs])
            hf = _dot3(hdn, w3f_ref[:, cs]) * decay
            hb = _dot3(hdn, w3b_ref[:, cs]) * decay
            row = lax.broadcasted_iota(jnp.int32, hb.shape, 0)
            hb = jnp.where(row == 0, 0.0, hb)
            hsum = hf + hb
            hs[:, cs] = hsum.astype(hs.dtype)
            hd[:, cs] = (hf - hb).astype(hd.dtype)
            alt = jnp.where((row & 1) == 0, 1.0, -1.0)
            kq_ref[:, cs] = jnp.sum(hsum * alt, axis=0, keepdims=True) * (1.0 / (2 * n))

    kidx = kt * tk + lax.broadcasted_iota(jnp.int32, (tk, 1), 0)
    wk = jnp.where(kidx == 0, 0.5 / n, 1.0 / n)
    kr_ref[...] = _dot(cos_ref[...], hs[...]) * wk
    ks_ref[...] = _dot(sin_ref[...], hd[...]) * wk


def _hyena_filter(n, cos_t, sin_t, w1, b1, w2, b2, w3, freq):
    tk = min(256, n)
    deltas = jnp.abs(jnp.linspace(math.log(HY_DECAY_TARGET) / HY_FAST_DECAY,
                                  math.log(HY_DECAY_TARGET) / HY_SLOW_DECAY, HY_WIDTH, dtype=F32))[None, :]
    w1p = jnp.pad(w1, ((0, LANE - HY_POS_EMB), (0, 0)))
    full = lambda a: pl.BlockSpec(a.shape, lambda k: (0,) * a.ndim)
    bank = lambda b: pl.BlockSpec((HY_FILTER_HIDDEN, HY_WIDTH), lambda k: (0, b))
    kern = functools.partial(_hyena_filter_kernel, n=n)
    feats = _hyena_feats(n)
    b1r, b2r = b1[None, :], b2[None, :]
    tile = pl.BlockSpec((tk, n), lambda k: (k, 0))
    spec = pl.BlockSpec((tk, HY_WIDTH), lambda k: (k, 0))
    return pl.pallas_call(
        kern,
        grid=(n // tk,),
        in_specs=[full(feats), full(w1p), full(b1r), full(w2), full(b2r), bank(0), bank(1), full(freq),
                  full(deltas), tile, tile],
        out_specs=[spec, spec, pl.BlockSpec((1, HY_WIDTH), lambda k: (0, 0))],
        out_shape=[jax.ShapeDtypeStruct((n, HY_WIDTH), F32), jax.ShapeDtypeStruct((n, HY_WIDTH), F32),
                   jax.ShapeDtypeStruct((1, HY_WIDTH), F32)],
        scratch_shapes=[pltpu.VMEM((n, HY_WIDTH), BF16), pltpu.VMEM((n, HY_WIDTH), BF16)],
        compiler_params=_cparams("arbitrary"),
        name="hyena_filter",
    )(feats, w1p, b1r, w2, b2r, w3, w3, freq, deltas, cos_t, sin_t)


def _spectral_product(a, bq, kr, ks):
    return (a * kr - bq * ks).astype(BF16), (a * ks + bq * kr).astype(BF16)


HY_TILE = 512


def _hyena_kernel(x0_ref, x1_ref, v_ref, z_ref, cw0_ref, cb0_ref, cw1_ref, cb1_ref, cwv_ref, cbv_ref, d_ref,
                  cr_ref, sr_ref, kr_ref, ks_ref, kq_ref, ccx_ref, scx_ref, krx_ref, ksx_ref, kqx_ref,
                  o_ref, u, yr_s, ys_s, unyq, *, n_ctx, n_chunks, with_ctx):
    kt = pl.program_id(2)
    nk = pl.num_programs(2) // 2
    n_ctx_chunks = n_ctx // CHUNK
    tb = n_chunks * CHUNK
    tk = cr_ref.shape[0]

    def alt_sign(shape):
        return jnp.where((lax.broadcasted_iota(jnp.int32, shape, 0) & 1) == 0, 1.0, -1.0)

    def gate_chunk(c):
        x0c = _conv_chunk(x0_ref, cw0_ref[...], cb0_ref[...], c, n_ctx_chunks, n_chunks)
        r0 = pl.multiple_of(c * CHUNK, CHUNK)
        return x0c * _silu(z_ref[pl.ds(r0, CHUNK), :].astype(F32))

    @pl.when(kt == 0)
    def _():
        def conv_body(c, carry):
            r0 = pl.multiple_of(c * CHUNK, CHUNK)
            x1c = _conv_chunk(x1_ref, cw1_ref[...], cb1_ref[...], c, n_ctx_chunks, n_chunks)
            vc = _conv_chunk(v_ref, cwv_ref[...], cbv_ref[...], c, n_ctx_chunks, n_chunks)
            u[pl.ds(r0, CHUNK), :] = (x1c * vc).astype(u.dtype)
            return carry

        lax.fori_loop(0, n_chunks, conv_body, 0)
        ulf = u[n_ctx:tb, :].astype(F32)
        unyq[...] = jnp.sum(ulf * alt_sign(ulf.shape), axis=0, keepdims=True) * kq_ref[...]

    @pl.when(kt < nk)
    def _():
        ul = u[n_ctx:tb, :]
        yr, ys = _spectral_product(_dot(cr_ref[...], ul), _dot(sr_ref[...], ul), kr_ref[...], ks_ref[...])
        f0 = pl.multiple_of(kt * tk, tk)
        yr_s[pl.ds(f0, tk), :] = yr
        ys_s[pl.ds(f0, tk), :] = ys

    @pl.when(kt >= nk)
    def _():
        t = kt - nk
        y = _dot(cr_ref[...], yr_s[...]) + _dot(sr_ref[...], ys_s[...])
        for j in range(tk // CHUNK):
            c = n_ctx_chunks + t * (tk // CHUNK) + j
            r0 = pl.multiple_of(c * CHUNK, CHUNK)
            uf = u[pl.ds(r0, CHUNK), :].astype(F32)
            yj = y[j * CHUNK:(j + 1) * CHUNK] + alt_sign(uf.shape) * unyq[...] + uf * d_ref[...]
            o_ref[pl.ds(r0, CHUNK), :] = (yj * gate_chunk(c)).astype(o_ref.dtype)

    @pl.when(kt == 2 * nk - 1)
    def _():
        if with_ctx:
            uc = u[0:n_ctx, :]
            ucf = uc.astype(F32)
            alt = alt_sign(ucf.shape)
            yr, ys = _spectral_product(_dot(ccx_ref[...], uc), _dot(scx_ref[...], uc), krx_ref[...], ksx_ref[...])
            y = _dot(ccx_ref[...], yr) + _dot(scx_ref[...], ys)
            y = y + alt * (jnp.sum(ucf * alt, axis=0, keepdims=True) * kqx_ref[...]) + ucf * d_ref[...]
            for c in range(n_ctx_chunks):
                o_ref[c * CHUNK:(c + 1) * CHUNK, :] = (y[c * CHUNK:(c + 1) * CHUNK] * gate_chunk(c)).astype(o_ref.dtype)
        else:
            o_ref[0:n_ctx, :] = jnp.zeros((n_ctx, o_ref.shape[1]), o_ref.dtype)


def _hyena(p3, conv_w, conv_b, d_skip, tabs, filt, tabs_c, filt_c, n_ctx, with_ctx):
    nb, tb, _ = p3.shape
    n_lat = tb - n_ctx
    cbw = 512
    tk = HY_TILE
    nk = n_lat // tk
    cos_t, sin_t = tabs
    kr, ks, kq = filt
    cos_c, sin_c = tabs_c
    krc, ksc, kqc = filt_c
    cb = conv_b[None, :]
    pcol = lambda off: pl.BlockSpec((None, tb, cbw), lambda b, c, k: (b, 0, off // cbw + c))
    par = lambda rows, sec: pl.BlockSpec((rows, cbw), lambda b, c, k: (0, sec * (HY_WIDTH // cbw) + c))
    kern = functools.partial(_hyena_kernel, n_ctx=n_ctx, n_chunks=tb // CHUNK, with_ctx=with_ctx)
    small = lambda a: pl.BlockSpec(a.shape, lambda b, c, k: (0, 0))
    chan = lambda rows: pl.BlockSpec((rows, cbw), lambda b, c, k: (0, c))
    table = pl.BlockSpec((tk, n_lat), lambda b, c, k: (k % nk, 0))
    spec = pl.BlockSpec((tk, cbw), lambda b, c, k: (jnp.minimum(k, nk - 1), c))
    return pl.pallas_call(
        kern,
        grid=(nb, HY_WIDTH // cbw, 2 * nk),
        in_specs=[pcol(P_HY_X0), pcol(P_HY_X1), pcol(P_HY_V), pcol(P_HY_Z),
                  par(3, 0), par(1, 0), par(3, 1), par(1, 1), par(3, 2), par(1, 2), chan(1),
                  table, table, spec, spec, chan(1), small(cos_c), small(sin_c), chan(n_ctx), chan(n_ctx), chan(1)],
        out_specs=pl.BlockSpec((None, tb, cbw), lambda b, c, k: (b, 0, c)),
        out_shape=jax.ShapeDtypeStruct((nb, tb, HY_WIDTH), BF16),
        scratch_shapes=[pltpu.VMEM((tb, cbw), BF16), pltpu.VMEM((n_lat, cbw), BF16), pltpu.VMEM((n_lat, cbw), BF16),
                        pltpu.VMEM((1, cbw), F32)],
        compiler_params=_cparams("parallel", "parallel", "arbitrary"),
        name="hyena_dft_conv",
    )(p3, p3, p3, p3, conv_w, cb, conv_w, cb, conv_w, cb, d_skip[None, :], cos_t, sin_t,
      kr, ks, kq, cos_c, sin_c, krc, ksc, kqc)


def _mix_kernel(g0_ref, g1_ref, g2_ref, g3_ref, lg_ref, w_ref, o_ref, acc):
    k = pl.program_id(1)
    for kk, g_ref in enumerate((g0_ref, g1_ref, g2_ref, g3_ref)):
        @pl.when(k == kk)
        def _(g_ref=g_ref, kk=kk):
            gate = 0.5 * jnp.tanh(0.5 * lg_ref[...].astype(F32)) + 0.5
            contrib = gate * _dot(g_ref[...], w_ref[0])
            if kk == 0:
                acc[...] = contrib
            else:
                acc[...] += contrib

    @pl.when(k == N_BRANCH - 1)
    def _():
        o_ref[...] = acc[...].astype(o_ref.dtype)


def _mix(gated, p, w_branch_bf):
    tt = p.shape[0]
    tm = math.gcd(tt, 1024)
    g_spec = pl.BlockSpec((tm, BRANCH_WIDTH), lambda i, k: (i, 0))
    return pl.pallas_call(
        _mix_kernel,
        grid=(tt // tm, N_BRANCH),
        in_specs=[g_spec, g_spec, g_spec, g_spec,
                  pl.BlockSpec((tm, D_MODEL), lambda i, k: (i, P_MERGE // D_MODEL + k)),
                  pl.BlockSpec((1, BRANCH_WIDTH, D_MODEL), lambda i, k: (k, 0, 0))],
        out_specs=pl.BlockSpec((tm, D_MODEL), lambda i, k: (i, 0)),
        out_shape=jax.ShapeDtypeStruct((tt, D_MODEL), BF16),
        scratch_shapes=[pltpu.VMEM((tm, D_MODEL), F32)],
        compiler_params=_cparams("parallel", "arbitrary"),
        name="branch_mix",
    )(*gated, p, w_branch_bf)


def _out_kernel(a_ref, w_ref, x_ref, mod_ref, g_ref, b_ref, *rest, with_next):
    mixed = _dot(a_ref[...], w_ref[...])
    r = DEEPNORM_ALPHA * x_ref[...] + mod_ref[0][2:3] * mixed
    mu = jnp.mean(r, axis=-1, keepdims=True)
    rc = r - mu
    var = jnp.mean(rc * rc, axis=-1, keepdims=True)
    xn = rc * lax.rsqrt(var + LN_EPS) * g_ref[...] + b_ref[...]
    if with_next:
        modn_ref, o_ref, h_ref = rest
        o_ref[...] = xn
        mu = jnp.mean(xn, axis=-1, keepdims=True)
        xc = xn - mu
        var = jnp.mean(xc * xc, axis=-1, keepdims=True)
        mn = modn_ref[0]
        h_ref[...] = (xc * lax.rsqrt(var + LN_EPS) * (1.0 + mn[1:2]) + mn[0:1]).astype(h_ref.dtype)
    else:
        (o_ref,) = rest
        o_ref[...] = xn


def _out_next(mixed, w_out_bf, xx, mod, ln_g, ln_b, mod_next, tps, nb):
    tt, d = xx.shape
    tile = pl.BlockSpec((ROW_TILE, d), lambda i: (i, 0))
    modspec = pl.BlockSpec((1, 3, d), lambda i: (_mod_row(i, tps, nb), 0, 0))
    vec = pl.BlockSpec((1, d), lambda i: (0, 0))
    return pl.pallas_call(
        functools.partial(_out_kernel, with_next=True),
        grid=(tt // ROW_TILE,),
        in_specs=[tile, pl.BlockSpec((d, d), lambda i: (0, 0)), tile, modspec, vec, vec, modspec],
        out_specs=[tile, tile],
        out_shape=[jax.ShapeDtypeStruct((tt, d), F32), jax.ShapeDtypeStruct((tt, d), BF16)],
        compiler_params=_cparams("parallel"),
        name="out_proj_norm_next",
    )(mixed, w_out_bf, xx, mod, ln_g[None, :], ln_b[None, :], mod_next)


def _out_last(mixed, w_out_bf, xx, mod, ln_g, ln_b, tps, nb, n_ctx):
    tt, d = xx.shape
    off = n_ctx // ROW_TILE
    tile = pl.BlockSpec((ROW_TILE, d), lambda b, j: (b * tps + off + j, 0))
    vec = pl.BlockSpec((1, d), lambda b, j: (0, 0))
    return pl.pallas_call(
        functools.partial(_out_kernel, with_next=False),
        grid=(nb, tps - off),
        in_specs=[tile, pl.BlockSpec((d, d), lambda b, j: (0, 0)), tile,
                  pl.BlockSpec((1, 3, d), lambda b, j: (b, 0, 0)), vec, vec],
        out_specs=pl.BlockSpec((None, ROW_TILE, d), lambda b, j: (b, j, 0)),
        out_shape=jax.ShapeDtypeStruct((nb, (tps - off) * ROW_TILE, d), F32),
        compiler_params=_cparams("parallel", "parallel"),
        name="out_proj_norm_last",
    )(mixed, w_out_bf, xx, mod, ln_g[None, :], ln_b[None, :])


def kernel(x, c, ctx, c_ctx, w_ada, b_ada, w_in, ssd_conv_w, ssd_conv_b, ssd_dt_bias, ssd_a_log, ssd_d,
           ssd_norm_w, mla_q_norm, mla_w_uq, mla_kv_norm, mla_w_ukv, gqa_sink, hy_conv_w, hy_conv_b,
           hy_w1, hy_b1, hy_w2, hy_b2, hy_w3, hy_freq, hy_d, w_branch, w_out, ln_g, ln_b):
    nb, n_lat, d = x.shape
    n_ctx = ctx.shape[1]
    tb = n_ctx + n_lat
    tt = nb * tb
    tps = tb // ROW_TILE
    assert d == D_MODEL and n_ctx == ROW_TILE and n_lat % 512 == 0 and n_lat % GRID_W == 0

    cos2, sin2 = _rope_tables(n_ctx, n_lat)
    tabs, tabs_c = _dft_tables(n_lat), _dft_tables(n_ctx)
    cvec = jax.nn.silu(jnp.concatenate([c, c_ctx[None]], axis=0))
    mods = [(_matmul(cvec, w_ada, i, F32, 8, 512, "adaln_mod") + b_ada[i]).reshape(nb + 1, 3, d)
            for i in range(DEPTH)]

    xx, h = _ln_mod(x, ctx, mods[0])
    out = None
    for i in range(DEPTH):
        last = i == DEPTH - 1
        p = _in_proj(h, w_in, i, tb, n_ctx, not last)
        p3 = p.reshape(nb, tb, P_WIDTH)
        g_ssd = _ssd(p3, ssd_conv_w[i], ssd_conv_b[i], ssd_dt_bias[i], ssd_a_log[i], ssd_d[i], ssd_norm_w[i],
                     n_ctx).reshape(tt, SSD_WIDTH)
        qm, km, vm, kg, vg = _attn_prep(p, cos2, sin2, mla_q_norm[i], mla_w_uq[i], mla_kv_norm[i], mla_w_ukv[i],
                                        tps)
        r3 = lambda a: a.reshape(nb, tb, a.shape[-1])
        g_mla = _mla(r3(qm), r3(km), r3(vm), p3, n_ctx).reshape(tt, MLA_WIDTH)
        g_gqa = _gqa(p, kg, vg, cos2, sin2, gqa_sink[i], nb, tb, n_ctx)
        fargs = (hy_w1[i], hy_b1[i], hy_w2[i], hy_b2[i], hy_w3[i], hy_freq[i])
        filt = _hyena_filter(n_lat, *tabs, *fargs)
        filt_c = _hyena_filter(n_ctx, *tabs_c, *fargs)
        g_hy = _hyena(p3, hy_conv_w[i], hy_conv_b[i], hy_d[i], tabs, filt, tabs_c, filt_c, n_ctx,
                      not last).reshape(tt, HY_WIDTH)
        mixed = _mix((g_ssd, g_mla, g_gqa, g_hy), p, w_branch[i].astype(BF16))
        w_out_bf = w_out[i].astype(BF16)
        if last:
            out = _out_last(mixed, w_out_bf, xx, mods[i], ln_g[i], ln_b[i], tps, nb, n_ctx)
        else:
            xx, h = _out_next(mixed, w_out_bf, xx, mods[i], ln_g[i], ln_b[i], mods[i + 1], tps, nb)
    return out
```
